```python
import math
import jax, jax.numpy as jnp
from jax import lax
import numpy as np

D_MODEL = 1024
BATCH = 8
SEQ = 2048
DEPTH = 4

N_MIXERS = 2
N_ATTN_LAYERS = (DEPTH + N_MIXERS - 1) // N_MIXERS
N_CONV_LAYERS = DEPTH // N_MIXERS
HEAD_DIM = 64
N_Q_HEADS = D_MODEL // HEAD_DIM
N_KV_HEADS = 4
GQA_GROUP = N_Q_HEADS // N_KV_HEADS
WINDOW = 128
Q_DIM = N_Q_HEADS * HEAD_DIM
KV_DIM = N_KV_HEADS * HEAD_DIM
QKV_DIM = Q_DIM + 2 * KV_DIM
NUM_BUCKETS = 32
MAX_EXACT = NUM_BUCKETS // 2
MAX_DISTANCE = 128
CONV_CHANNELS = D_MODEL
CONV_WIDTH = 31
N_EXPERTS = 32
TOP_K = 4
D_EXPERT = D_MODEL
SWIGLU_LIMIT = 7.0
SWIGLU_ALPHA = 1.702
EXPERT_BLOCK = 128
NORM_EPS = 1e-5
NEG_INF = -1e30

kernel_name = "hybrid_swa_conformer_moe_adaln"


def _rmsnorm(x, g):
    xf = x.astype(jnp.float32)
    y = xf * lax.rsqrt(jnp.mean(xf * xf, axis=-1, keepdims=True) + NORM_EPS)
    return (y * g.astype(jnp.float32)).astype(x.dtype)


def _layernorm(x, g, b):
    xf = x.astype(jnp.float32)
    mu = jnp.mean(xf, axis=-1, keepdims=True)
    var = jnp.mean(jnp.square(xf - mu), axis=-1, keepdims=True)
    y = (xf - mu) * lax.rsqrt(var + NORM_EPS)
    return (y * g.astype(jnp.float32) + b.astype(jnp.float32)).astype(x.dtype)


def _band_geometry():
    dist = jnp.arange(WINDOW)[:, None] + WINDOW - jnp.arange(2 * WINDOW)[None, :]
    n = jnp.maximum(dist, 0)
    nf = jnp.maximum(n, 1).astype(jnp.float32)
    large = MAX_EXACT + (jnp.log(nf / MAX_EXACT) / math.log(MAX_DISTANCE / MAX_EXACT)
                         * (NUM_BUCKETS - MAX_EXACT)).astype(jnp.int32)
    large = jnp.minimum(large, NUM_BUCKETS - 1)
    bucket = jnp.where(n < MAX_EXACT, n, large)
    return dist, bucket


def _rel_bias(table):
    _, bucket = _band_geometry()
    bias = table[bucket]
    return jnp.transpose(bias, (2, 0, 1)).reshape(N_KV_HEADS, GQA_GROUP, WINDOW, 2 * WINDOW)


def _sliding_window_attention(h, w_qkv, b_qkv, w_o, b_o, sinks, rel_bias):
    bsz, seq, _ = h.shape
    nb = seq // WINDOW
    qkv = h @ w_qkv + b_qkv
    q, k, v = jnp.split(qkv, [Q_DIM, Q_DIM + KV_DIM], axis=-1)
    q = q.reshape(bsz, nb, WINDOW, N_KV_HEADS, GQA_GROUP, HEAD_DIM) * (HEAD_DIM ** -0.5)
    k = k.reshape(bsz, nb, WINDOW, N_KV_HEADS, HEAD_DIM)
    v = v.reshape(bsz, nb, WINDOW, N_KV_HEADS, HEAD_DIM)
    pad = ((0, 0), (1, 0), (0, 0), (0, 0), (0, 0))
    kw = jnp.concatenate([jnp.pad(k, pad)[:, :-1], k], axis=2)
    vw = jnp.concatenate([jnp.pad(v, pad)[:, :-1], v], axis=2)
    scores = jnp.einsum('bnqhgd,bnshd->bnhgqs', q, kw).astype(jnp.float32)
    scores = scores + rel_bias.astype(jnp.float32)[None, None]
    dist, _ = _band_geometry()
    in_window = (dist >= 0) & (dist < WINDOW)
    key_pos = jnp.arange(nb)[:, None] * WINDOW + jnp.arange(2 * WINDOW)[None, :] - WINDOW
    valid = in_window[None] & (key_pos >= 0)[:, None, :]
    scores = jnp.where(valid[None, :, None, None], scores, NEG_INF)
    sink = sinks.astype(jnp.float32).reshape(N_KV_HEADS, GQA_GROUP)[None, None, :, :, None, None]
    sink = jnp.broadcast_to(sink, scores.shape[:-1] + (1,))
    probs = jax.nn.softmax(jnp.concatenate([scores, sink], axis=-1), axis=-1)[..., :-1]
    out = jnp.einsum('bnhgqs,bnshd->bnqhgd', probs.astype(vw.dtype), vw)
    return out.reshape(bsz, seq, Q_DIM) @ w_o + b_o


def _conformer_conv(h, w_in, b_in, w_dw, b_dw, ln_g, ln_b, w_out, b_out):
    u = h @ w_in + b_in
    a, g = jnp.split(u, 2, axis=-1)
    u = a * jax.nn.sigmoid(g)
    y = lax.conv_general_dilated(u, w_dw[:, None, :], window_strides=(1,),
                                 padding=[(CONV_WIDTH - 1, 0)],
                                 dimension_numbers=('NWC', 'WIO', 'NWC'),
                                 feature_group_count=CONV_CHANNELS) + b_dw
    y = jax.nn.silu(_layernorm(y, ln_g, ln_b))
    return y @ w_out + b_out


def _clamped_swiglu(u):
    glu, lin = jnp.split(u, 2, axis=-1)
    glu = jnp.minimum(glu, SWIGLU_LIMIT)
    lin = jnp.clip(lin, -SWIGLU_LIMIT, SWIGLU_LIMIT)
    return glu * jax.nn.sigmoid(SWIGLU_ALPHA * glu) * (lin + 1.0)


def _moe(h, w_r, b_r, w_in, b_in, w_out, b_out):
    bsz, seq, d = h.shape
    n_tok = bsz * seq
    n_slot = n_tok * TOP_K
    hf = h.reshape(n_tok, d)
    logits = hf @ w_r + b_r
    top_val, top_idx = lax.top_k(logits, TOP_K)
    gates = jax.nn.softmax(top_val.astype(jnp.float32), axis=-1).astype(h.dtype)
    flat_e = top_idx.reshape(-1)
    flat_t = jnp.repeat(jnp.arange(n_tok, dtype=jnp.int32), TOP_K)
    flat_g = gates.reshape(-1)
    order = jnp.argsort(flat_e)
    se = flat_e[order]
    counts = jnp.bincount(flat_e, length=N_EXPERTS)
    padded = (counts + EXPERT_BLOCK - 1) // EXPERT_BLOCK * EXPERT_BLOCK
    ustart = jnp.cumsum(counts) - counts
    pend = jnp.cumsum(padded)
    pstart = pend - padded
    dest = pstart[se] + jnp.arange(n_slot) - ustart[se]
    n_pad = n_slot + N_EXPERTS * EXPERT_BLOCK
    n_blk = n_pad // EXPERT_BLOCK
    tok_pad = jnp.zeros((n_pad,), jnp.int32).at[dest].set(flat_t[order])
    g_pad = jnp.zeros((n_pad,), h.dtype).at[dest].set(flat_g[order])
    blk_e = jnp.minimum(jnp.searchsorted(pend, jnp.arange(n_blk) * EXPERT_BLOCK, side='right'),
                        N_EXPERTS - 1)
    xs = hf[tok_pad].reshape(n_blk, EXPERT_BLOCK, d)

    def expert_block(args):
        xb, e = args
        u = xb @ w_in[e] + b_in[e]
        return _clamped_swiglu(u) @ w_out[e] + b_out[e]

    ys = lax.map(expert_block, (xs, blk_e)).reshape(n_pad, d)
    out = jnp.zeros((n_tok, d), h.dtype).at[tok_pad].add(ys * g_pad[:, None])
    return out.reshape(bsz, seq, d)


def setup_inputs(seed: int = 0) -> dict:
    key = jax.random.key(seed)
    ks = jax.random.split(key, 32)
    D, E, F = D_MODEL, N_EXPERTS, D_EXPERT
    nrm = lambda k, shape, s: jax.random.normal(k, shape, jnp.float32) * s
    return {
        "x": nrm(ks[0], (BATCH, SEQ, D), 1.0),
        "c": nrm(ks[1], (BATCH, D), 1.0),
        "w_mod": nrm(ks[2], (DEPTH, D, 6 * D), 0.5 * D ** -0.5),
        "b_mod": nrm(ks[3], (DEPTH, 6 * D), 0.02),
        "norm_mix_g": 1.0 + nrm(ks[4], (DEPTH, D), 0.02),
        "norm_ffn_g": 1.0 + nrm(ks[5], (DEPTH, D), 0.02),
        "attn_w_qkv": nrm(ks[6], (N_ATTN_LAYERS, D, QKV_DIM), D ** -0.5),
        "attn_b_qkv": nrm(ks[7], (N_ATTN_LAYERS, QKV_DIM), 0.02),
        "attn_w_o": nrm(ks[8], (N_ATTN_LAYERS, Q_DIM, D), Q_DIM ** -0.5),
        "attn_b_o": nrm(ks[9], (N_ATTN_LAYERS, D), 0.02),
        "attn_sinks": nrm(ks[10], (N_ATTN_LAYERS, N_Q_HEADS), 0.5),
        "rel_bias_table": nrm(ks[11], (NUM_BUCKETS, N_Q_HEADS), 0.5),
        "conv_w_in": nrm(ks[12], (N_CONV_LAYERS, D, 2 * CONV_CHANNELS), D ** -0.5),
        "conv_b_in": nrm(ks[13], (N_CONV_LAYERS, 2 * CONV_CHANNELS), 0.02),
        "conv_w_dw": nrm(ks[14], (N_CONV_LAYERS, CONV_WIDTH, CONV_CHANNELS), CONV_WIDTH ** -0.5),
        "conv_b_dw": nrm(ks[15], (N_CONV_LAYERS, CONV_CHANNELS), 0.02),
        "conv_ln_g": 1.0 + nrm(ks[16], (N_CONV_LAYERS, CONV_CHANNELS), 0.02),
        "conv_ln_b": nrm(ks[17], (N_CONV_LAYERS, CONV_CHANNELS), 0.02),
        "conv_w_out": nrm(ks[18], (N_CONV_LAYERS, CONV_CHANNELS, D), CONV_CHANNELS ** -0.5),
        "conv_b_out": nrm(ks[19], (N_CONV_LAYERS, D), 0.02),
        "router_w": nrm(ks[20], (DEPTH, D, E), D ** -0.5),
        "router_b": nrm(ks[21], (DEPTH, E), 0.01),
        "moe_w_in": nrm(ks[22], (DEPTH, E, D, 2 * F), D ** -0.5),
        "moe_b_in": nrm(ks[23], (DEPTH, E, 2 * F), 0.02),
        "moe_w_out": nrm(ks[24], (DEPTH, E, F, D), F ** -0.5),
        "moe_b_out": nrm(ks[25], (DEPTH, E, D), 0.02),
        "final_norm_g": 1.0 + nrm(ks[26], (D,), 0.02),
    }


def reference(x, c, w_mod, b_mod, norm_mix_g, norm_ffn_g, attn_w_qkv, attn_b_qkv, attn_w_o, attn_b_o,
              attn_sinks, rel_bias_table, conv_w_in, conv_b_in, conv_w_dw, conv_b_dw, conv_ln_g, conv_ln_b,
              conv_w_out, conv_b_out, router_w, router_b, moe_w_in, moe_b_in, moe_w_out, moe_b_out,
              final_norm_g):
    rel_bias = _rel_bias(rel_bias_table)
    c_act = jax.nn.silu(c)
    for i in range(DEPTH):
        mod = c_act @ w_mod[i] + b_mod[i]
        sh_m, sc_m, g_m, sh_f, sc_f, g_f = jnp.split(mod, 6, axis=-1)
        h = _rmsnorm(x, norm_mix_g[i]) * (1.0 + sc_m[:, None]) + sh_m[:, None]
        j = i // N_MIXERS
        if i % N_MIXERS == 0:
            m = _sliding_window_attention(h, attn_w_qkv[j], attn_b_qkv[j], attn_w_o[j], attn_b_o[j],
                                          attn_sinks[j], rel_bias)
        else:
            m = _conformer_conv(h, conv_w_in[j], conv_b_in[j], conv_w_dw[j], conv_b_dw[j],
                                conv_ln_g[j], conv_ln_b[j], conv_w_out[j], conv_b_out[j])
        x = x + g_m[:, None] * m
        h = _rmsnorm(x, norm_ffn_g[i]) * (1.0 + sc_f[:, None]) + sh_f[:, None]
        x = x + g_f[:, None] * _moe(h, router_w[i], router_b[i], moe_w_in[i], moe_b_in[i],
                                     moe_w_out[i], moe_b_out[i])
    return _rmsnorm(x, final_norm_g)
```

```python
import functools
import math

import jax
import jax.numpy as jnp
from jax import lax
from jax.experimental import pallas as pl
from jax.experimental.pallas import tpu as pltpu

D_MODEL = 1024
BATCH = 8
SEQ = 2048
DEPTH = 4
N_TOK = BATCH * SEQ
HEAD_DIM = 64
N_Q_HEADS = 16
N_KV_HEADS = 4
GQA_GROUP = 4
WINDOW = 128
Q_DIM = 1024
KV_DIM = 256
QKV_DIM = 1536
NUM_BUCKETS = 32
MAX_EXACT = 16
MAX_DISTANCE = 128
CONV_WIDTH = 31
N_EXPERTS = 32
TOP_K = 4
D_EXPERT = 1024
SWIGLU_LIMIT = 7.0
SWIGLU_ALPHA = 1.702
NORM_EPS = 1e-5
NEG_INF = -1e30

LANES = 128
SUBLANES = 8
ROW_TILES = D_MODEL // LANES
N_SLOT = N_TOK * TOP_K

TM = 512
TM_ATTN = WINDOW
TD = 512
TC = 256
BM = 256
N_BLK = N_SLOT // BM
N_ITEMS = N_BLK + N_EXPERTS
CONV_HALO = 32
CONV_ROWS = 64

VMEM_LIMIT = 56 * 1024 * 1024


def _cparams(*sem):
    return pltpu.CompilerParams(dimension_semantics=sem, vmem_limit_bytes=VMEM_LIMIT)


def _mod_kernel(c_ref, w_ref, b_ref, o_ref):
    c = c_ref[...]
    c_act = (c * jax.nn.sigmoid(c)).astype(jnp.bfloat16)
    o_ref[...] = jnp.dot(c_act, w_ref[...].astype(jnp.bfloat16),
                         preferred_element_type=jnp.float32) + b_ref[...]


def _modulation(c, w_mod, b_mod):
    out = pl.pallas_call(
        _mod_kernel,
        grid=(DEPTH, 6),
        in_specs=[
            pl.BlockSpec((BATCH, D_MODEL), lambda i, j: (0, 0)),
            pl.BlockSpec((None, D_MODEL, D_MODEL), lambda i, j: (i, 0, j)),
            pl.BlockSpec((None, None, 1, D_MODEL), lambda i, j: (i, j, 0, 0)),
        ],
        out_specs=pl.BlockSpec((None, None, BATCH, D_MODEL), lambda i, j: (i, j, 0, 0)),
        out_shape=jax.ShapeDtypeStruct((DEPTH, 6, BATCH, D_MODEL), jnp.float32),
        compiler_params=_cparams("arbitrary", "arbitrary"),
        name="adaln_modulation",
    )(c, w_mod, b_mod.reshape(DEPTH, 6, 1, D_MODEL))
    return jnp.transpose(out, (0, 2, 1, 3))


def _mod_spec(layer):
    return pl.BlockSpec((None, None, 6, D_MODEL), lambda i: (layer, i // (SEQ // TM), 0, 0))


def _rms_modulate(x, g, scale, shift):
    y = x * lax.rsqrt(jnp.mean(x * x, axis=-1, keepdims=True) + NORM_EPS)
    return (y * g) * (1.0 + scale) + shift


def _in_proj_kernel(x_ref, mod_ref, g_ref, w_ref, b_ref, o_ref, *, glu):
    h = _rms_modulate(x_ref[...], g_ref[...], mod_ref[1:2, :], mod_ref[0:1, :])
    u = jnp.dot(h.astype(jnp.bfloat16), w_ref[...], preferred_element_type=jnp.float32) + b_ref[...]
    if glu:
        half = u.shape[-1] // 2
        u = u[:, :half] * jax.nn.sigmoid(u[:, half:])
    o_ref[...] = u.astype(o_ref.dtype)


def _in_proj(x, mod, g, w, b, layer, *, glu, out_dtype):
    n_in = w.shape[-1]
    n_out = n_in // 2 if glu else n_in
    return pl.pallas_call(
        functools.partial(_in_proj_kernel, glu=glu),
        grid=(N_TOK // TM,),
        in_specs=[
            pl.BlockSpec((TM, D_MODEL), lambda i: (i, 0)),
            _mod_spec(layer),
            pl.BlockSpec((1, D_MODEL), lambda i: (0, 0)),
            pl.BlockSpec((D_MODEL, n_in), lambda i: (0, 0)),
            pl.BlockSpec((1, n_in), lambda i: (0, 0)),
        ],
        out_specs=pl.BlockSpec((TM, n_out), lambda i: (i, 0)),
        out_shape=jax.ShapeDtypeStruct((N_TOK, n_out), out_dtype),
        compiler_params=_cparams("arbitrary"),
        name="mixer_in_proj_glu" if glu else "mixer_in_proj",
    )(x, mod, g.reshape(1, D_MODEL), w, b.reshape(1, n_in))


def _band_geometry():
    dist = jnp.arange(WINDOW)[:, None] + WINDOW - jnp.arange(2 * WINDOW)[None, :]
    n = jnp.maximum(dist, 0)
    nf = jnp.maximum(n, 1).astype(jnp.float32)
    large = MAX_EXACT + (jnp.log(nf / MAX_EXACT) / math.log(MAX_DISTANCE / MAX_EXACT)
                         * (NUM_BUCKETS - MAX_EXACT)).astype(jnp.int32)
    large = jnp.minimum(large, NUM_BUCKETS - 1)
    bucket = jnp.where(n < MAX_EXACT, n, large)
    return dist, bucket


def _bias_kernel(table_ref, bucket_ref, dist_ref, o_ref):
    h = pl.program_id(0)
    bucket = bucket_ref[...]
    dist = dist_ref[...]
    acc = jnp.zeros(bucket.shape, jnp.float32)
    for b in range(NUM_BUCKETS):
        acc = jnp.where(bucket == b, table_ref[b, h], acc)
    in_window = (dist >= 0) & (dist < WINDOW)
    o_ref[...] = jnp.where(in_window, acc, NEG_INF)


def _rel_bias(table):
    dist, bucket = _band_geometry()
    return pl.pallas_call(
        _bias_kernel,
        grid=(N_Q_HEADS,),
        in_specs=[
            pl.BlockSpec(memory_space=pltpu.SMEM),
            pl.BlockSpec((WINDOW, 2 * WINDOW), lambda h: (0, 0)),
            pl.BlockSpec((WINDOW, 2 * WINDOW), lambda h: (0, 0)),
        ],
        out_specs=pl.BlockSpec((None, WINDOW, 2 * WINDOW), lambda h: (h, 0, 0)),
        out_shape=jax.ShapeDtypeStruct((N_Q_HEADS, WINDOW, 2 * WINDOW), jnp.float32),
        compiler_params=_cparams("arbitrary"),
        name="rel_bias",
    )(table, bucket.astype(jnp.int32), dist.astype(jnp.int32))


def _attn_kernel(sink_ref, q_ref, kv_ref, kvp_ref, bias_ref, o_ref, *, layer):
    i = pl.program_id(0)
    first_block = (i % (SEQ // WINDOW)) == 0
    q = q_ref[...]
    kv = jnp.concatenate([kvp_ref[...], kv_ref[...]], axis=0)
    col = lax.broadcasted_iota(jnp.int32, (WINDOW, 2 * WINDOW), 1)
    prev_ok = jnp.logical_or(col >= WINDOW, jnp.logical_not(first_block))
    outs = []
    for hk in range(N_KV_HEADS):
        k = kv[:, hk * HEAD_DIM:(hk + 1) * HEAD_DIM]
        v = kv[:, KV_DIM + hk * HEAD_DIM:KV_DIM + (hk + 1) * HEAD_DIM]
        for g in range(GQA_GROUP):
            hq = hk * GQA_GROUP + g
            qh = q[:, hq * HEAD_DIM:(hq + 1) * HEAD_DIM] * (HEAD_DIM ** -0.5)
            s = lax.dot_general(qh, k, (((1,), (1,)), ((), ())), preferred_element_type=jnp.float32)
            s = jnp.where(prev_ok, s + bias_ref[hq], NEG_INF)
            sink = sink_ref[layer, hq]
            m = jnp.maximum(jnp.max(s, axis=-1, keepdims=True), sink)
            p = jnp.exp(s - m)
            denom = jnp.sum(p, axis=-1, keepdims=True) + jnp.exp(sink - m)
            o = jnp.dot(p.astype(jnp.bfloat16), v, preferred_element_type=jnp.float32)
            outs.append((o / denom).astype(jnp.bfloat16))
    o_ref[...] = jnp.concatenate(outs, axis=-1)


def _attention(qkv, bias, sinks, layer):
    kv_col = Q_DIM // (2 * KV_DIM)
    return pl.pallas_call(
        functools.partial(_attn_kernel, layer=layer),
        grid=(N_TOK // WINDOW,),
        in_specs=[
            pl.BlockSpec(memory_space=pltpu.SMEM),
            pl.BlockSpec((WINDOW, Q_DIM), lambda i: (i, 0)),
            pl.BlockSpec((WINDOW, 2 * KV_DIM), lambda i: (i, kv_col)),
            pl.BlockSpec((WINDOW, 2 * KV_DIM), lambda i: (jnp.maximum(i - 1, 0), kv_col)),
            pl.BlockSpec((N_Q_HEADS, WINDOW, 2 * WINDOW), lambda i: (0, 0, 0)),
        ],
        out_specs=pl.BlockSpec((WINDOW, Q_DIM), lambda i: (i, 0)),
        out_shape=jax.ShapeDtypeStruct((N_TOK, Q_DIM), jnp.bfloat16),
        compiler_params=_cparams("arbitrary"),
        name="swa_attention",
    )(sinks, qkv, qkv, qkv, bias)


def _split_bf16(a):
    hi = a.astype(jnp.bfloat16)
    lo = (a - hi.astype(jnp.float32)).astype(jnp.bfloat16)
    return hi, lo


def _residual_ffn_router(m, x, mod, g_ffn, wr_hi, wr_lo, b_r, xo_ref, h_ref, idx_ref, gate_ref):
    x_new = x + mod[2:3, :] * m
    xo_ref[...] = x_new
    h = _rms_modulate(x_new, g_ffn, mod[4:5, :], mod[3:4, :])
    rows = h.shape[0]
    for s in range(ROW_TILES):
        h_ref[pl.ds(s, rows, stride=ROW_TILES), :] = h[:, s * LANES:(s + 1) * LANES]
    h_hi, h_lo = _split_bf16(h)
    logits = (jnp.dot(h_hi, wr_hi, preferred_element_type=jnp.float32)
              + jnp.dot(h_hi, wr_lo, preferred_element_type=jnp.float32)
              + jnp.dot(h_lo, wr_hi, preferred_element_type=jnp.float32)) + b_r
    lane = lax.broadcasted_iota(jnp.int32, logits.shape, 1)
    vals, ids = [], []
    for _ in range(TOP_K):
        best = jnp.max(logits, axis=-1, keepdims=True)
        arg = jnp.min(jnp.where(logits == best, lane, N_EXPERTS), axis=-1, keepdims=True)
        vals.append(best)
        ids.append(arg)
        logits = jnp.where(lane == arg, -jnp.inf, logits)
    exps = [jnp.exp(v - vals[0]) for v in vals]
    total = exps[0] + exps[1] + exps[2] + exps[3]
    idx_ref[...] = jnp.concatenate(ids, axis=-1)
    gate_ref[...] = jnp.concatenate([e / total for e in exps], axis=-1)


def _attn_out_kernel(a_ref, wo_ref, bo_ref, x_ref, mod_ref, g_ref, wrh_ref, wrl_ref, br_ref,
                     xo_ref, h_ref, idx_ref, gate_ref):
    m = jnp.dot(a_ref[...], wo_ref[...], preferred_element_type=jnp.float32) + bo_ref[...]
    _residual_ffn_router(m, x_ref[...], mod_ref[...], g_ref[...], wrh_ref[...], wrl_ref[...], br_ref[...],
                         xo_ref, h_ref, idx_ref, gate_ref)


def _epilogue_out_specs():
    return [
        pl.BlockSpec((TM, D_MODEL), lambda i: (i, 0)),
        pl.BlockSpec((TM * ROW_TILES, LANES), lambda i: (i, 0)),
        pl.BlockSpec((TM, TOP_K), lambda i: (i, 0)),
        pl.BlockSpec((TM, TOP_K), lambda i: (i, 0)),
    ]


def _epilogue_out_shapes():
    return [
        jax.ShapeDtypeStruct((N_TOK, D_MODEL), jnp.float32),
        jax.ShapeDtypeStruct((N_TOK * ROW_TILES, LANES), jnp.float32),
        jax.ShapeDtypeStruct((N_TOK, TOP_K), jnp.int32),
        jax.ShapeDtypeStruct((N_TOK, TOP_K), jnp.float32),
    ]


def _router_operands(w_r, b_r):
    hi, lo = _split_bf16(w_r)
    return hi, lo, b_r.reshape(1, N_EXPERTS)


def _router_specs():
    return [
        pl.BlockSpec((D_MODEL, N_EXPERTS), lambda i: (0, 0)),
        pl.BlockSpec((D_MODEL, N_EXPERTS), lambda i: (0, 0)),
        pl.BlockSpec((1, N_EXPERTS), lambda i: (0, 0)),
    ]


def _attn_out(attn, w_o, b_o, x, mod, g_ffn, w_r, b_r, layer):
    return pl.pallas_call(
        _attn_out_kernel,
        grid=(N_TOK // TM,),
        in_specs=[
            pl.BlockSpec((TM, Q_DIM), lambda i: (i, 0)),
            pl.BlockSpec((Q_DIM, D_MODEL), lambda i: (0, 0)),
            pl.BlockSpec((1, D_MODEL), lambda i: (0, 0)),
            pl.BlockSpec((TM, D_MODEL), lambda i: (i, 0)),
            _mod_spec(layer),
            pl.BlockSpec((1, D_MODEL), lambda i: (0, 0)),
        ] + _router_specs(),
        out_specs=_epilogue_out_specs(),
        out_shape=_epilogue_out_shapes(),
        compiler_params=_cparams("arbitrary"),
        name="attn_out_router",
    )(attn, w_o, b_o.reshape(1, D_MODEL), x, mod, g_ffn.reshape(1, D_MODEL), *_router_operands(w_r, b_r))


def _conv_out_kernel(u_ref, up_ref, wdw_ref, bdw_ref, lng_ref, lnb_ref, wo_ref, bo_ref,
                     x_ref, mod_ref, g_ref, wrh_ref, wrl_ref, br_ref,
                     xo_ref, h_ref, idx_ref, gate_ref, ext_ref, y_ref):
    i = pl.program_id(0)
    first_tile = (i % (SEQ // TM)) == 0
    ext_ref[0:CONV_HALO, :] = jnp.where(first_tile, 0.0, up_ref[...])
    ext_ref[CONV_HALO:, :] = u_ref[...]
    lead = CONV_HALO - (CONV_WIDTH - 1)
    span = CONV_ROWS + CONV_HALO

    def chunk(r, carry):
        r0 = pl.multiple_of(r * CONV_ROWS, CONV_ROWS)
        for cb in range(ROW_TILES):
            cols = slice(cb * LANES, (cb + 1) * LANES)
            e = ext_ref[pl.ds(r0, span), cols]
            acc = jnp.zeros((CONV_ROWS, LANES), jnp.float32)
            for sub in range(SUBLANES):
                shifted = e if sub == 0 else pltpu.roll(e, span - sub, 0)
                for a in range(span // SUBLANES):
                    tap = a * SUBLANES + sub - lead
                    if 0 <= tap < CONV_WIDTH:
                        acc = acc + wdw_ref[tap:tap + 1, cols] * shifted[a * SUBLANES:a * SUBLANES + CONV_ROWS]
            y_ref[pl.ds(r0, CONV_ROWS), cols] = acc
        return carry

    lax.fori_loop(0, TM // CONV_ROWS, chunk, 0)
    y = y_ref[...] + bdw_ref[...]
    mu = jnp.mean(y, axis=-1, keepdims=True)
    var = jnp.mean(jnp.square(y - mu), axis=-1, keepdims=True)
    z = (y - mu) * lax.rsqrt(var + NORM_EPS) * lng_ref[...] + lnb_ref[...]
    z = z * jax.nn.sigmoid(z)
    m = jnp.dot(z.astype(jnp.bfloat16), wo_ref[...], preferred_element_type=jnp.float32) + bo_ref[...]
    _residual_ffn_router(m, x_ref[...], mod_ref[...], g_ref[...], wrh_ref[...], wrl_ref[...], br_ref[...],
                         xo_ref, h_ref, idx_ref, gate_ref)


def _conv_out(u, w_dw, b_dw, ln_g, ln_b, w_o, b_o, x, mod, g_ffn, w_r, b_r, layer):
    halo_blocks = TM // CONV_HALO
    row = lambda a: a.reshape(1, D_MODEL)
    return pl.pallas_call(
        _conv_out_kernel,
        grid=(N_TOK // TM,),
        in_specs=[
            pl.BlockSpec((TM, D_MODEL), lambda i: (i, 0)),
            pl.BlockSpec((CONV_HALO, D_MODEL), lambda i: (jnp.maximum(i * halo_blocks - 1, 0), 0)),
            pl.BlockSpec((CONV_WIDTH, D_MODEL), lambda i: (0, 0)),
            pl.BlockSpec((1, D_MODEL), lambda i: (0, 0)),
            pl.BlockSpec((1, D_MODEL), lambda i: (0, 0)),
            pl.BlockSpec((1, D_MODEL), lambda i: (0, 0)),
            pl.BlockSpec((D_MODEL, D_MODEL), lambda i: (0, 0)),
            pl.BlockSpec((1, D_MODEL), lambda i: (0, 0)),
            pl.BlockSpec((TM, D_MODEL), lambda i: (i, 0)),
            _mod_spec(layer),
            pl.BlockSpec((1, D_MODEL), lambda i: (0, 0)),
        ] + _router_specs(),
        out_specs=_epilogue_out_specs(),
        out_shape=_epilogue_out_shapes(),
        scratch_shapes=[
            pltpu.VMEM((TM + CONV_HALO, D_MODEL), jnp.float32),
            pltpu.VMEM((TM, D_MODEL), jnp.float32),
        ],
        compiler_params=_cparams("arbitrary"),
        name="conv_out_router",
    )(u, u, w_dw, row(b_dw), row(ln_g), row(ln_b), w_o, row(b_o), x, mod, row(g_ffn),
      *_router_operands(w_r, b_r))


def _routing(idx):
    experts = jnp.arange(N_EXPERTS, dtype=jnp.int32)
    onehot = (idx[:, :, None] == experts[None, None, :]).astype(jnp.int32).sum(axis=1)
    before = jnp.cumsum(onehot, axis=0) - onehot
    counts = onehot.sum(axis=0)
    ends = jnp.cumsum(counts)
    starts = ends - counts
    pos = starts[idx] + jnp.take_along_axis(before, idx, axis=1)

    blk_lo = jnp.arange(N_BLK, dtype=jnp.int32)[:, None] * BM
    lo = jnp.maximum(blk_lo, starts[None, :])
    hi = jnp.minimum(blk_lo + BM, ends[None, :])
    valid = (lo < hi).reshape(-1)
    order = jnp.cumsum(valid.astype(jnp.int32))
    n_valid = order[-1]
    item = jnp.arange(N_ITEMS, dtype=jnp.int32)
    pair = jnp.searchsorted(order, jnp.minimum(item, n_valid - 1) + 1, side="left").astype(jnp.int32)
    live = item < n_valid
    item_blk = pair // N_EXPERTS
    item_exp = pair % N_EXPERTS
    item_lo = jnp.where(live, lo.reshape(-1)[pair] - item_blk * BM, 0)
    item_hi = jnp.where(live, hi.reshape(-1)[pair] - item_blk * BM, 0)
    return pos.astype(jnp.int32), item_blk, item_exp, item_lo.astype(jnp.int32), item_hi.astype(jnp.int32)


def _row_copy(src, src_row, dst, dst_row, sem):
    return pltpu.make_async_copy(
        src.at[pl.ds(pl.multiple_of(src_row * ROW_TILES, ROW_TILES), ROW_TILES), :],
        dst.at[pl.ds(pl.multiple_of(dst_row * ROW_TILES, ROW_TILES), ROW_TILES), :],
        sem)


def _dispatch_kernel(pos_ref, h_hbm, xs_hbm, sem):
    base = pl.program_id(0) * TD

    def issue(j, carry):
        _row_copy(h_hbm, base + j // TOP_K, xs_hbm, pos_ref[0, 0, j], sem).start()
        return carry

    lax.fori_loop(0, TD * TOP_K, issue, 0)
    n = TD * TOP_K * ROW_TILES
    pltpu.make_async_copy(h_hbm.at[pl.ds(0, n), :], xs_hbm.at[pl.ds(0, n), :], sem).wait()


def _dispatch(h_rows, pos):
    return pl.pallas_call(
        _dispatch_kernel,
        grid=(N_TOK // TD,),
        in_specs=[
            pl.BlockSpec((1, 1, TD * TOP_K), lambda i: (i, 0, 0), memory_space=pltpu.SMEM),
            pl.BlockSpec(memory_space=pl.ANY),
        ],
        out_specs=pl.BlockSpec(memory_space=pl.ANY),
        out_shape=jax.ShapeDtypeStruct((N_SLOT * ROW_TILES, LANES), jnp.float32),
        scratch_shapes=[pltpu.SemaphoreType.DMA(())],
        compiler_params=_cparams("arbitrary"),
        name="moe_dispatch",
    )(pos.reshape(N_TOK // TD, 1, TD * TOP_K), h_rows)


def _gmm_kernel(blk_ref, exp_ref, lo_ref, hi_ref, xs_ref, win_ref, bin_ref, wout_ref, bout_ref,
                ys_ref, win_bf, wout_bf):
    i = pl.program_id(0)
    new_expert = jnp.logical_or(i == 0, exp_ref[i] != exp_ref[jnp.maximum(i - 1, 0)])

    @pl.when(new_expert)
    def _():
        win_bf[...] = win_ref[...].astype(jnp.bfloat16)
        wout_bf[...] = wout_ref[...].astype(jnp.bfloat16)

    lo = lo_ref[i]
    hi = hi_ref[i]

    @pl.when(hi > lo)
    def _():
        x = jnp.concatenate(
            [xs_ref[pl.ds(s, BM, stride=ROW_TILES), :] for s in range(ROW_TILES)], axis=-1)
        u = jnp.dot(x.astype(jnp.bfloat16), win_bf[...], preferred_element_type=jnp.float32) + bin_ref[...]
        glu = jnp.minimum(u[:, :D_EXPERT], SWIGLU_LIMIT)
        lin = jnp.clip(u[:, D_EXPERT:], -SWIGLU_LIMIT, SWIGLU_LIMIT)
        act = glu * jax.nn.sigmoid(SWIGLU_ALPHA * glu) * (lin + 1.0)
        y = jnp.dot(act.astype(jnp.bfloat16), wout_bf[...], preferred_element_type=jnp.float32) + bout_ref[...]
        row = lax.broadcasted_iota(jnp.int32, (BM, LANES), 0)
        mine = jnp.logical_and(row >= lo, row < hi)
        keep_old = lo > 0
        for s in range(ROW_TILES):
            dst = ys_ref.at[pl.ds(s, BM, stride=ROW_TILES), :]
            old = jnp.where(keep_old, dst[...], 0.0)
            dst[...] = jnp.where(mine, y[:, s * LANES:(s + 1) * LANES], old)


def _grouped_mlp(xs, item_blk, item_exp, item_lo, item_hi, w_in, b_in, w_out, b_out, layer):
    grid_spec = pltpu.PrefetchScalarGridSpec(
        num_scalar_prefetch=4,
        grid=(N_ITEMS,),
        in_specs=[
            pl.BlockSpec((BM * ROW_TILES, LANES), lambda i, blk, ex, lo, hi: (blk[i], 0)),
            pl.BlockSpec((None, None, D_MODEL, 2 * D_EXPERT), lambda i, blk, ex, lo, hi: (layer, ex[i], 0, 0)),
            pl.BlockSpec((None, None, 1, 2 * D_EXPERT), lambda i, blk, ex, lo, hi: (layer, ex[i], 0, 0)),
            pl.BlockSpec((None, None, D_EXPERT, D_MODEL), lambda i, blk, ex, lo, hi: (layer, ex[i], 0, 0)),
            pl.BlockSpec((None, None, 1, D_MODEL), lambda i, blk, ex, lo, hi: (layer, ex[i], 0, 0)),
        ],
        out_specs=pl.BlockSpec((BM * ROW_TILES, LANES), lambda i, blk, ex, lo, hi: (blk[i], 0)),
        scratch_shapes=[
            pltpu.VMEM((D_MODEL, 2 * D_EXPERT), jnp.bfloat16),
            pltpu.VMEM((D_EXPERT, D_MODEL), jnp.bfloat16),
        ],
    )
    return pl.pallas_call(
        _gmm_kernel,
        grid_spec=grid_spec,
        out_shape=jax.ShapeDtypeStruct((N_SLOT * ROW_TILES, LANES), jnp.float32),
        compiler_params=_cparams("arbitrary"),
        name="moe_grouped_mlp",
    )(item_blk, item_exp, item_lo, item_hi, xs, w_in,
      b_in.reshape(DEPTH, N_EXPERTS, 1, 2 * D_EXPERT), w_out, b_out.reshape(DEPTH, N_EXPERTS, 1, D_MODEL))


def _combine_kernel(pos_ref, ys_hbm, gate_ref, x_ref, mod_ref, gfin_ref, o_ref, buf, sem, *, final_norm):
    def issue(j, carry):
        _row_copy(ys_hbm, pos_ref[0, 0, j], buf, (j % TOP_K) * TC + j // TOP_K, sem).start()
        return carry

    lax.fori_loop(0, TC * TOP_K, issue, 0)
    n = TC * TOP_K * ROW_TILES
    pltpu.make_async_copy(ys_hbm.at[pl.ds(0, n), :], buf, sem).wait()

    gates = gate_ref[...]
    cols = []
    for s in range(ROW_TILES):
        acc = jnp.zeros((TC, LANES), jnp.float32)
        for k in range(TOP_K):
            acc = acc + gates[:, k:k + 1] * buf[pl.ds(k * TC * ROW_TILES + s, TC, stride=ROW_TILES), :]
        cols.append(acc)
    x_new = x_ref[...] + mod_ref[5:6, :] * jnp.concatenate(cols, axis=-1)
    if final_norm:
        x_new = x_new * lax.rsqrt(jnp.mean(x_new * x_new, axis=-1, keepdims=True) + NORM_EPS) * gfin_ref[...]
    o_ref[...] = x_new


def _combine(ys, pos, gates, x, mod, g_final, layer, *, final_norm):
    return pl.pallas_call(
        functools.partial(_combine_kernel, final_norm=final_norm),
        grid=(N_TOK // TC,),
        in_specs=[
            pl.BlockSpec((1, 1, TC * TOP_K), lambda i: (i, 0, 0), memory_space=pltpu.SMEM),
            pl.BlockSpec(memory_space=pl.ANY),
            pl.BlockSpec((TC, TOP_K), lambda i: (i, 0)),
            pl.BlockSpec((TC, D_MODEL), lambda i: (i, 0)),
            pl.BlockSpec((None, None, 6, D_MODEL), lambda i: (layer, i // (SEQ // TC), 0, 0)),
            pl.BlockSpec((1, D_MODEL), lambda i: (0, 0)),
        ],
        out_specs=pl.BlockSpec((TC, D_MODEL), lambda i: (i, 0)),
        out_shape=jax.ShapeDtypeStruct((N_TOK, D_MODEL), jnp.float32),
        scratch_shapes=[
            pltpu.VMEM((TC * TOP_K * ROW_TILES, LANES), jnp.float32),
            pltpu.SemaphoreType.DMA(()),
        ],
        compiler_params=_cparams("arbitrary"),
        name="moe_combine",
    )(pos.reshape(N_TOK // TC, 1, TC * TOP_K), ys, gates, x, mod, g_final.reshape(1, D_MODEL))


def kernel(x, c, w_mod, b_mod, norm_mix_g, norm_ffn_g, attn_w_qkv, attn_b_qkv, attn_w_o, attn_b_o,
           attn_sinks, rel_bias_table, conv_w_in, conv_b_in, conv_w_dw, conv_b_dw, conv_ln_g, conv_ln_b,
           conv_w_out, conv_b_out, router_w, router_b, moe_w_in, moe_b_in, moe_w_out, moe_b_out,
           final_norm_g):
    bf16 = jnp.bfloat16
    mod = _modulation(c, w_mod, b_mod)
    bias = _rel_bias(rel_bias_table)
    xf = x.reshape(N_TOK, D_MODEL)
    for i in range(DEPTH):
        j = i // 2
        if i % 2 == 0:
            qkv = _in_proj(xf, mod, norm_mix_g[i], attn_w_qkv[j].astype(bf16), attn_b_qkv[j], i,
                           glu=False, out_dtype=bf16)
            attn = _attention(qkv, bias, attn_sinks, j)
            xf, h_rows, idx, gates = _attn_out(attn, attn_w_o[j].astype(bf16), attn_b_o[j], xf, mod,
                                               norm_ffn_g[i], router_w[i], router_b[i], i)
        else:
            u = _in_proj(xf, mod, norm_mix_g[i], conv_w_in[j].astype(bf16), conv_b_in[j], i,
                         glu=True, out_dtype=jnp.float32)
            xf, h_rows, idx, gates = _conv_out(u, conv_w_dw[j], conv_b_dw[j], conv_ln_g[j], conv_ln_b[j],
                                               conv_w_out[j].astype(bf16), conv_b_out[j], xf, mod,
                                               norm_ffn_g[i], router_w[i], router_b[i], i)
        pos, item_blk, item_exp, item_lo, item_hi = _routing(idx)
        xs = _dispatch(h_rows, pos)
        ys = _grouped_mlp(xs, item_blk, item_exp, item_lo, item_hi, moe_w_in, moe_b_in, moe_w_out,
                          moe_b_out, i)
        xf = _combine(ys, pos, gates, xf, mod, final_norm_g, i, final_norm=(i == DEPTH - 1))
    return xf.reshape(BATCH, SEQ, D_MODEL)
```

```python
import functools
import math

import jax
import jax.numpy as jnp
from jax import lax
from jax.experimental import pallas as pl
from jax.experimental.pallas import tpu as pltpu

D_MODEL = 1024
BATCH = 8
SEQ = 2048
DEPTH = 4
N_TOK = BATCH * SEQ
HEAD_DIM = 64
N_Q_HEADS = 16
N_KV_HEADS = 4
GQA_GROUP = 4
WINDOW = 128
Q_DIM = 1024
KV_DIM = 256
QKV_DIM = 1536
NUM_BUCKETS = 32
MAX_EXACT = 16
MAX_DISTANCE = 128
CONV_WIDTH = 31
N_EXPERTS = 32
TOP_K = 4
D_EXPERT = 1024
SWIGLU_LIMIT = 7.0
SWIGLU_ALPHA = 1.702
NORM_EPS = 1e-5
NEG_INF = -1e30

LANES = 128
SUBLANES = 8
ROW_TILES = D_MODEL // LANES
N_SLOT = N_TOK * TOP_K

TM = 512
TM_ATTN = WINDOW
TD = 512
TC = 256
BM = 256
N_BLK = N_SLOT // BM
N_ITEMS = N_BLK + N_EXPERTS
CONV_HALO = 32
CONV_ROWS = 64

VMEM_LIMIT = 56 * 1024 * 1024


def _cparams(*sem):
    return pltpu.CompilerParams(dimension_semantics=sem, vmem_limit_bytes=VMEM_LIMIT)


def _mod_kernel(c_ref, w_ref, b_ref, o_ref):
    c = c_ref[...]
    c_act = (c * jax.nn.sigmoid(c)).astype(jnp.bfloat16)
    o_ref[...] = jnp.dot(c_act, w_ref[...].astype(jnp.bfloat16),
                         preferred_element_type=jnp.float32) + b_ref[...]


def _modulation(c, w_mod, b_mod):
    out = pl.pallas_call(
        _mod_kernel,
        grid=(DEPTH, 6),
        in_specs=[
            pl.BlockSpec((BATCH, D_MODEL), lambda i, j: (0, 0)),
            pl.BlockSpec((None, D_MODEL, D_MODEL), lambda i, j: (i, 0, j)),
            pl.BlockSpec((None, None, 1, D_MODEL), lambda i, j: (i, j, 0, 0)),
        ],
        out_specs=pl.BlockSpec((None, None, BATCH, D_MODEL), lambda i, j: (i, j, 0, 0)),
        out_shape=jax.ShapeDtypeStruct((DEPTH, 6, BATCH, D_MODEL), jnp.float32),
        compiler_params=_cparams("arbitrary", "arbitrary"),
        name="adaln_modulation",
    )(c, w_mod, b_mod.reshape(DEPTH, 6, 1, D_MODEL))
    return jnp.transpose(out, (0, 2, 1, 3))


def _mod_spec(layer):
    return pl.BlockSpec((None, None, 6, D_MODEL), lambda i: (layer, i // (SEQ // TM), 0, 0))


def _rms_modulate(x, g, scale, shift):
    y = x * lax.rsqrt(jnp.mean(x * x, axis=-1, keepdims=True) + NORM_EPS)
    return (y * g) * (1.0 + scale) + shift


def _in_proj_kernel(x_ref, mod_ref, g_ref, w_ref, b_ref, o_ref, *, glu):
    h = _rms_modulate(x_ref[...], g_ref[...], mod_ref[1:2, :], mod_ref[0:1, :])
    u = jnp.dot(h.astype(jnp.bfloat16), w_ref[...], preferred_element_type=jnp.float32) + b_ref[...]
    if glu:
        half = u.shape[-1] // 2
        u = u[:, :half] * jax.nn.sigmoid(u[:, half:])
    o_ref[...] = u.astype(o_ref.dtype)


def _in_proj(x, mod, g, w, b, layer, *, glu, out_dtype):
    n_in = w.shape[-1]
    n_out = n_in // 2 if glu else n_in
    return pl.pallas_call(
        functools.partial(_in_proj_kernel, glu=glu),
        grid=(N_TOK // TM,),
        in_specs=[
            pl.BlockSpec((TM, D_MODEL), lambda i: (i, 0)),
            _mod_spec(layer),
            pl.BlockSpec((1, D_MODEL), lambda i: (0, 0)),
            pl.BlockSpec((D_MODEL, n_in), lambda i: (0, 0)),
            pl.BlockSpec((1, n_in), lambda i: (0, 0)),
        ],
        out_specs=pl.BlockSpec((TM, n_out), lambda i: (i, 0)),
        out_shape=jax.ShapeDtypeStruct((N_TOK, n_out), out_dtype),
        compiler_params=_cparams("arbitrary"),
        name="mixer_in_proj_glu" if glu else "mixer_in_proj",
    )(x, mod, g.reshape(1, D_MODEL), w, b.reshape(1, n_in))


def _band_geometry():
    dist = jnp.arange(WINDOW)[:, None] + WINDOW - jnp.arange(2 * WINDOW)[None, :]
    n = jnp.maximum(dist, 0)
    nf = jnp.maximum(n, 1).astype(jnp.float32)
    large = MAX_EXACT + (jnp.log(nf / MAX_EXACT) / math.log(MAX_DISTANCE / MAX_EXACT)
                         * (NUM_BUCKETS - MAX_EXACT)).astype(jnp.int32)
    large = jnp.minimum(large, NUM_BUCKETS - 1)
    bucket = jnp.where(n < MAX_EXACT, n, large)
    return dist, bucket


def _bias_kernel(table_ref, bucket_ref, dist_ref, o_ref):
    h = pl.program_id(0)
    bucket = bucket_ref[...]
    dist = dist_ref[...]
    acc = jnp.zeros(bucket.shape, jnp.float32)
    for b in range(NUM_BUCKETS):
        acc = jnp.where(bucket == b, table_ref[b, h], acc)
    in_window = (dist >= 0) & (dist < WINDOW)
    o_ref[...] = jnp.where(in_window, acc, NEG_INF)


def _rel_bias(table):
    dist, bucket = _band_geometry()
    return pl.pallas_call(
        _bias_kernel,
        grid=(N_Q_HEADS,),
        in_specs=[
            pl.BlockSpec(memory_space=pltpu.SMEM),
            pl.BlockSpec((WINDOW, 2 * WINDOW), lambda h: (0, 0)),
            pl.BlockSpec((WINDOW, 2 * WINDOW), lambda h: (0, 0)),
        ],
        out_specs=pl.BlockSpec((None, WINDOW, 2 * WINDOW), lambda h: (h, 0, 0)),
        out_shape=jax.ShapeDtypeStruct((N_Q_HEADS, WINDOW, 2 * WINDOW), jnp.float32),
        compiler_params=_cparams("arbitrary"),
        name="rel_bias",
    )(table, bucket.astype(jnp.int32), dist.astype(jnp.int32))


def _attn_kernel(sink_ref, q_ref, kv_ref, kvp_ref, bias_ref, o_ref, *, layer):
    i = pl.program_id(0)
    first_block = (i % (SEQ // WINDOW)) == 0
    q = q_ref[...]
    kv = jnp.concatenate([kvp_ref[...], kv_ref[...]], axis=0)
    col = lax.broadcasted_iota(jnp.int32, (WINDOW, 2 * WINDOW), 1)
    prev_ok = jnp.logical_or(col >= WINDOW, jnp.logical_not(first_block))
    outs = []
    for hk in range(N_KV_HEADS):
        k = kv[:, hk * HEAD_DIM:(hk + 1) * HEAD_DIM]
        v = kv[:, KV_DIM + hk * HEAD_DIM:KV_DIM + (hk + 1) * HEAD_DIM]
        for g in range(GQA_GROUP):
            hq = hk * GQA_GROUP + g
            qh = q[:, hq * HEAD_DIM:(hq + 1) * HEAD_DIM] * (HEAD_DIM ** -0.5)
            s = lax.dot_general(qh, k, (((1,), (1,)), ((), ())), preferred_element_type=jnp.float32)
            s = jnp.where(prev_ok, s + bias_ref[hq], NEG_INF)
            sink = sink_ref[layer, hq]
            m = jnp.maximum(jnp.max(s, axis=-1, keepdims=True), sink)
            p = jnp.exp(s - m)
            denom = jnp.sum(p, axis=-1, keepdims=True) + jnp.exp(sink - m)
            o = jnp.dot(p.astype(jnp.bfloat16), v, preferred_element_type=jnp.float32)
            outs.append((o / denom).astype(jnp.bfloat16))
    o_ref[...] = jnp.concatenate(outs, axis=-1)


def _attention(qkv, bias, sinks, layer):
    kv_col = Q_DIM // (2 * KV_DIM)
    return pl.pallas_call(
        functools.partial(_attn_kernel, layer=layer),
        grid=(N_TOK // WINDOW,),
        in_specs=[
            pl.BlockSpec(memory_space=pltpu.SMEM),
            pl.BlockSpec((WINDOW, Q_DIM), lambda i: (i, 0)),
            pl.BlockSpec((WINDOW, 2 * KV_DIM), lambda i: (i, kv_col)),
            pl.BlockSpec((WINDOW, 2 * KV_DIM), lambda i: (jnp.maximum(i - 1, 0), kv_col)),
            pl.BlockSpec((N_Q_HEADS, WINDOW, 2 * WINDOW), lambda i: (0, 0, 0)),
        ],
        out_specs=pl.BlockSpec((WINDOW, Q_DIM), lambda i: (i, 0)),
        out_shape=jax.ShapeDtypeStruct((N_TOK, Q_DIM), jnp.bfloat16),
        compiler_params=_cparams("arbitrary"),
        name="swa_attention",
    )(sinks, qkv, qkv, qkv, bias)


def _split_bf16(a):
    hi = a.astype(jnp.bfloat16)
    lo = (a - hi.astype(jnp.float32)).astype(jnp.bfloat16)
    return hi, lo


def _residual_ffn_router(m, x, mod, g_ffn, wr_hi, wr_lo, b_r, xo_ref, h_ref, idx_ref, gate_ref):
    x_new = x + mod[2:3, :] * m
    xo_ref[...] = x_new
    h = _rms_modulate(x_new, g_ffn, mod[4:5, :], mod[3:4, :])
    rows = h.shape[0]
    for s in range(ROW_TILES):
        h_ref[pl.ds(s, rows, stride=ROW_TILES), :] = h[:, s * LANES:(s + 1) * LANES]
    h_hi, h_lo = _split_bf16(h)
    logits = (jnp.dot(h_hi, wr_hi, preferred_element_type=jnp.float32)
              + jnp.dot(h_hi, wr_lo, preferred_element_type=jnp.float32)
              + jnp.dot(h_lo, wr_hi, preferred_element_type=jnp.float32)) + b_r
    lane = lax.broadcasted_iota(jnp.int32, logits.shape, 1)
    vals, ids = [], []
    for _ in range(TOP_K):
        best = jnp.max(logits, axis=-1, keepdims=True)
        arg = jnp.min(jnp.where(logits == best, lane, N_EXPERTS), axis=-1, keepdims=True)
        vals.append(best)
        ids.append(arg)
        logits = jnp.where(lane == arg, -jnp.inf, logits)
    exps = [jnp.exp(v - vals[0]) for v in vals]
    total = exps[0] + exps[1] + exps[2] + exps[3]
    idx_ref[...] = jnp.concatenate(ids, axis=-1)
    gate_ref[...] = jnp.concatenate([e / total for e in exps], axis=-1)


def _attn_out_kernel(a_ref, wo_ref, bo_ref, x_ref, mod_ref, g_ref, wrh_ref, wrl_ref, br_ref,
                     xo_ref, h_ref, idx_ref, gate_ref):
    m = jnp.dot(a_ref[...], wo_ref[...], preferred_element_type=jnp.float32) + bo_ref[...]
    _residual_ffn_router(m, x_ref[...], mod_ref[...], g_ref[...], wrh_ref[...], wrl_ref[...], br_ref[...],
                         xo_ref, h_ref, idx_ref, gate_ref)


def _epilogue_out_specs():
    return [
        pl.BlockSpec((TM, D_MODEL), lambda i: (i, 0)),
        pl.BlockSpec((TM * ROW_TILES, LANES), lambda i: (i, 0)),
        pl.BlockSpec((TM, TOP_K), lambda i: (i, 0)),
        pl.BlockSpec((TM, TOP_K), lambda i: (i, 0)),
    ]


def _epilogue_out_shapes():
    return [
        jax.ShapeDtypeStruct((N_TOK, D_MODEL), jnp.float32),
        jax.ShapeDtypeStruct((N_TOK * ROW_TILES, LANES), jnp.float32),
        jax.ShapeDtypeStruct((N_TOK, TOP_K), jnp.int32),
        jax.ShapeDtypeStruct((N_TOK, TOP_K), jnp.float32),
    ]


def _router_operands(w_r, b_r):
    hi, lo = _split_bf16(w_r)
    return hi, lo, b_r.reshape(1, N_EXPERTS)


def _router_specs():
    return [
        pl.BlockSpec((D_MODEL, N_EXPERTS), lambda i: (0, 0)),
        pl.BlockSpec((D_MODEL, N_EXPERTS), lambda i: (0, 0)),
        pl.BlockSpec((1, N_EXPERTS), lambda i: (0, 0)),
    ]


def _attn_out(attn, w_o, b_o, x, mod, g_ffn, w_r, b_r, layer):
    return pl.pallas_call(
        _attn_out_kernel,
        grid=(N_TOK // TM,),
        in_specs=[
            pl.BlockSpec((TM, Q_DIM), lambda i: (i, 0)),
            pl.BlockSpec((Q_DIM, D_MODEL), lambda i: (0, 0)),
            pl.BlockSpec((1, D_MODEL), lambda i: (0, 0)),
            pl.BlockSpec((TM, D_MODEL), lambda i: (i, 0)),
            _mod_spec(layer),
            pl.BlockSpec((1, D_MODEL), lambda i: (0, 0)),
        ] + _router_specs(),
        out_specs=_epilogue_out_specs(),
        out_shape=_epilogue_out_shapes(),
        compiler_params=_cparams("arbitrary"),
        name="attn_out_router",
    )(attn, w_o, b_o.reshape(1, D_MODEL), x, mod, g_ffn.reshape(1, D_MODEL), *_router_operands(w_r, b_r))


def _conv_out_kernel(u_ref, up_ref, wdw_ref, bdw_ref, lng_ref, lnb_ref, wo_ref, bo_ref,
                     x_ref, mod_ref, g_ref, wrh_ref, wrl_ref, br_ref,
                     xo_ref, h_ref, idx_ref, gate_ref, ext_ref, y_ref):
    i = pl.program_id(0)
    first_tile = (i % (SEQ // TM)) == 0
    ext_ref[0:CONV_HALO, :] = jnp.where(first_tile, 0.0, up_ref[...])
    ext_ref[CONV_HALO:, :] = u_ref[...]
    lead = CONV_HALO - (CONV_WIDTH - 1)
    span = CONV_ROWS + CONV_HALO

    def chunk(r, carry):
        r0 = pl.multiple_of(r * CONV_ROWS, CONV_ROWS)
        for cb in range(ROW_TILES):
            cols = slice(cb * LANES, (cb + 1) * LANES)
            e = ext_ref[pl.ds(r0, span), cols]
            acc = jnp.zeros((CONV_ROWS, LANES), jnp.float32)
            for sub in range(SUBLANES):
                shifted = e if sub == 0 else pltpu.roll(e, span - sub, 0)
                for a in range(span // SUBLANES):
                    tap = a * SUBLANES + sub - lead
                    if 0 <= tap < CONV_WIDTH:
                        acc = acc + wdw_ref[tap:tap + 1, cols] * shifted[a * SUBLANES:a * SUBLANES + CONV_ROWS]
            y_ref[pl.ds(r0, CONV_ROWS), cols] = acc
        return carry

    lax.fori_loop(0, TM // CONV_ROWS, chunk, 0)
    y = y_ref[...] + bdw_ref[...]
    mu = jnp.mean(y, axis=-1, keepdims=True)
    var = jnp.mean(jnp.square(y - mu), axis=-1, keepdims=True)
    z = (y - mu) * lax.rsqrt(var + NORM_EPS) * lng_ref[...] + lnb_ref[...]
    z = z * jax.nn.sigmoid(z)
    m = jnp.dot(z.astype(jnp.bfloat16), wo_ref[...], preferred_element_type=jnp.float32) + bo_ref[...]
    _residual_ffn_router(m, x_ref[...], mod_ref[...], g_ref[...], wrh_ref[...], wrl_ref[...], br_ref[...],
                         xo_ref, h_ref, idx_ref, gate_ref)


def _conv_out(u, w_dw, b_dw, ln_g, ln_b, w_o, b_o, x, mod, g_ffn, w_r, b_r, layer):
    halo_blocks = TM // CONV_HALO
    row = lambda a: a.reshape(1, D_MODEL)
    return pl.pallas_call(
        _conv_out_kernel,
        grid=(N_TOK // TM,),
        in_specs=[
            pl.BlockSpec((TM, D_MODEL), lambda i: (i, 0)),
            pl.BlockSpec((CONV_HALO, D_MODEL), lambda i: (jnp.maximum(i * halo_blocks - 1, 0), 0)),
            pl.BlockSpec((CONV_WIDTH, D_MODEL), lambda i: (0, 0)),
            pl.BlockSpec((1, D_MODEL), lambda i: (0, 0)),
            pl.BlockSpec((1, D_MODEL), lambda i: (0, 0)),
            pl.BlockSpec((1, D_MODEL), lambda i: (0, 0)),
            pl.BlockSpec((D_MODEL, D_MODEL), lambda i: (0, 0)),
            pl.BlockSpec((1, D_MODEL), lambda i: (0, 0)),
            pl.BlockSpec((TM, D_MODEL), lambda i: (i, 0)),
            _mod_spec(layer),
            pl.BlockSpec((1, D_MODEL), lambda i: (0, 0)),
        ] + _router_specs(),
        out_specs=_epilogue_out_specs(),
        out_shape=_epilogue_out_shapes(),
        scratch_shapes=[
            pltpu.VMEM((TM + CONV_HALO, D_MODEL), jnp.float32),
            pltpu.VMEM((TM, D_MODEL), jnp.float32),
        ],
        compiler_params=_cparams("arbitrary"),
        name="conv_out_router",
    )(u, u, w_dw, row(b_dw), row(ln_g), row(ln_b), w_o, row(b_o), x, mod, row(g_ffn),
      *_router_operands(w_r, b_r))


def _routing(idx):
    experts = jnp.arange(N_EXPERTS, dtype=jnp.int32)
    onehot = (idx[:, :, None] == experts[None, None, :]).astype(jnp.int32).sum(axis=1)
    before = jnp.cumsum(onehot, axis=0) - onehot
    counts = onehot.sum(axis=0)
    ends = jnp.cumsum(counts)
    starts = ends - counts
    pos = starts[idx] + jnp.take_along_axis(before, idx, axis=1)

    blk_lo = jnp.arange(N_BLK, dtype=jnp.int32)[:, None] * BM
    lo = jnp.maximum(blk_lo, starts[None, :])
    hi = jnp.minimum(blk_lo + BM, ends[None, :])
    valid = (lo < hi).reshape(-1)
    order = jnp.cumsum(valid.astype(jnp.int32))
    n_valid = order[-1]
    item = jnp.arange(N_ITEMS, dtype=jnp.int32)
    pair = jnp.searchsorted(order, jnp.minimum(item, n_valid - 1) + 1, side="left").astype(jnp.int32)
    live = item < n_valid
    item_blk = pair // N_EXPERTS
    item_exp = pair % N_EXPERTS
    item_lo = jnp.where(live, lo.reshape(-1)[pair] - item_blk * BM, 0)
    item_hi = jnp.where(live, hi.reshape(-1)[pair] - item_blk * BM, 0)
    return pos.astype(jnp.int32), item_blk, item_exp, item_lo.astype(jnp.int32), item_hi.astype(jnp.int32)


def _row_copy(src, src_row, dst, dst_row, sem):
    return pltpu.make_async_copy(
        src.at[pl.ds(pl.multiple_of(src_row * ROW_TILES, ROW_TILES), ROW_TILES), :],
        dst.at[pl.ds(pl.multiple_of(dst_row * ROW_TILES, ROW_TILES), ROW_TILES), :],
        sem)


def _dispatch_kernel(pos_ref, h_ref, xs_hbm, sem):
    def issue(j, carry):
        _row_copy(h_ref, j // TOP_K, xs_hbm, pos_ref[0, 0, j], sem).start()
        return carry

    lax.fori_loop(0, TD * TOP_K, issue, 0)
    n = TD * TOP_K * ROW_TILES
    pltpu.make_async_copy(xs_hbm.at[pl.ds(0, n), :], xs_hbm.at[pl.ds(0, n), :], sem).wait()


def _dispatch(h_rows, pos):
    return pl.pallas_call(
        _dispatch_kernel,
        grid=(N_TOK // TD,),
        in_specs=[
            pl.BlockSpec((1, 1, TD * TOP_K), lambda i: (i, 0, 0), memory_space=pltpu.SMEM),
            pl.BlockSpec((TD * ROW_TILES, LANES), lambda i: (i, 0)),
        ],
        out_specs=pl.BlockSpec(memory_space=pl.ANY),
        out_shape=jax.ShapeDtypeStruct((N_SLOT * ROW_TILES, LANES), jnp.float32),
        scratch_shapes=[pltpu.SemaphoreType.DMA(())],
        compiler_params=_cparams("arbitrary"),
        name="moe_dispatch",
    )(pos.reshape(N_TOK // TD, 1, TD * TOP_K), h_rows)


def _gmm_kernel(blk_ref, exp_ref, lo_ref, hi_ref, xs_ref, win_ref, bin_ref, wout_ref, bout_ref,
                ys_ref, win_bf, wout_bf):
    i = pl.program_id(0)
    new_expert = jnp.logical_or(i == 0, exp_ref[i] != exp_ref[jnp.maximum(i - 1, 0)])

    @pl.when(new_expert)
    def _():
        win_bf[...] = win_ref[...].astype(jnp.bfloat16)
        wout_bf[...] = wout_ref[...].astype(jnp.bfloat16)

    lo = lo_ref[i]
    hi = hi_ref[i]

    @pl.when(hi > lo)
    def _():
        x = jnp.concatenate(
            [xs_ref[pl.ds(s, BM, stride=ROW_TILES), :] for s in range(ROW_TILES)], axis=-1)
        u = jnp.dot(x.astype(jnp.bfloat16), win_bf[...], preferred_element_type=jnp.float32) + bin_ref[...]
        glu = jnp.minimum(u[:, :D_EXPERT], SWIGLU_LIMIT)
        lin = jnp.clip(u[:, D_EXPERT:], -SWIGLU_LIMIT, SWIGLU_LIMIT)
        act = glu * jax.nn.sigmoid(SWIGLU_ALPHA * glu) * (lin + 1.0)
        y = jnp.dot(act.astype(jnp.bfloat16), wout_bf[...], preferred_element_type=jnp.float32) + bout_ref[...]
        row = lax.broadcasted_iota(jnp.int32, (BM, LANES), 0)
        mine = jnp.logical_and(row >= lo, row < hi)
        keep_old = lo > 0
        for s in range(ROW_TILES):
            dst = ys_ref.at[pl.ds(s, BM, stride=ROW_TILES), :]
            old = jnp.where(keep_old, dst[...], 0.0)
            dst[...] = jnp.where(mine, y[:, s * LANES:(s + 1) * LANES], old)


def _grouped_mlp(xs, item_blk, item_exp, item_lo, item_hi, w_in, b_in, w_out, b_out, layer):
    grid_spec = pltpu.PrefetchScalarGridSpec(
        num_scalar_prefetch=4,
        grid=(N_ITEMS,),
        in_specs=[
            pl.BlockSpec((BM * ROW_TILES, LANES), lambda i, blk, ex, lo, hi: (blk[i], 0)),
            pl.BlockSpec((None, None, D_MODEL, 2 * D_EXPERT), lambda i, blk, ex, lo, hi: (layer, ex[i], 0, 0)),
            pl.BlockSpec((None, None, 1, 2 * D_EXPERT), lambda i, blk, ex, lo, hi: (layer, ex[i], 0, 0)),
            pl.BlockSpec((None, None, D_EXPERT, D_MODEL), lambda i, blk, ex, lo, hi: (layer, ex[i], 0, 0)),
            pl.BlockSpec((None, None, 1, D_MODEL), lambda i, blk, ex, lo, hi: (layer, ex[i], 0, 0)),
        ],
        out_specs=pl.BlockSpec((BM * ROW_TILES, LANES), lambda i, blk, ex, lo, hi: (blk[i], 0)),
        scratch_shapes=[
            pltpu.VMEM((D_MODEL, 2 * D_EXPERT), jnp.bfloat16),
            pltpu.VMEM((D_EXPERT, D_MODEL), jnp.bfloat16),
        ],
    )
    return pl.pallas_call(
        _gmm_kernel,
        grid_spec=grid_spec,
        out_shape=jax.ShapeDtypeStruct((N_SLOT * ROW_TILES, LANES), jnp.float32),
        compiler_params=_cparams("arbitrary"),
        name="moe_grouped_mlp",
    )(item_blk, item_exp, item_lo, item_hi, xs, w_in,
      b_in.reshape(DEPTH, N_EXPERTS, 1, 2 * D_EXPERT), w_out, b_out.reshape(DEPTH, N_EXPERTS, 1, D_MODEL))


def _combine_kernel(pos_ref, ys_hbm, gate_ref, x_ref, mod_ref, gfin_ref, o_ref, buf, sem, *, final_norm):
    def issue(j, carry):
        _row_copy(ys_hbm, pos_ref[0, 0, j], buf, (j % TOP_K) * TC + j // TOP_K, sem).start()
        return carry

    lax.fori_loop(0, TC * TOP_K, issue, 0)
    n = TC * TOP_K * ROW_TILES
    pltpu.make_async_copy(ys_hbm.at[pl.ds(0, n), :], buf, sem).wait()

    gates = gate_ref[...]
    cols = []
    for s in range(ROW_TILES):
        acc = jnp.zeros((TC, LANES), jnp.float32)
        for k in range(TOP_K):
            acc = acc + gates[:, k:k + 1] * buf[pl.ds(k * TC * ROW_TILES + s, TC, stride=ROW_TILES), :]
        cols.append(acc)
    x_new = x_ref[...] + mod_ref[5:6, :] * jnp.concatenate(cols, axis=-1)
    if final_norm:
        x_new = x_new * lax.rsqrt(jnp.mean(x_new * x_new, axis=-1, keepdims=True) + NORM_EPS) * gfin_ref[...]
    o_ref[...] = x_new


def _combine(ys, pos, gates, x, mod, g_final, layer, *, final_norm):
    return pl.pallas_call(
        functools.partial(_combine_kernel, final_norm=final_norm),
        grid=(N_TOK // TC,),
        in_specs=[
            pl.BlockSpec((1, 1, TC * TOP_K), lambda i: (i, 0, 0), memory_space=pltpu.SMEM),
            pl.BlockSpec(memory_space=pl.ANY),
            pl.BlockSpec((TC, TOP_K), lambda i: (i, 0)),
            pl.BlockSpec((TC, D_MODEL), lambda i: (i, 0)),
            pl.BlockSpec((None, None, 6, D_MODEL), lambda i: (layer, i // (SEQ // TC), 0, 0)),
            pl.BlockSpec((1, D_MODEL), lambda i: (0, 0)),
        ],
        out_specs=pl.BlockSpec((TC, D_MODEL), lambda i: (i, 0)),
        out_shape=jax.ShapeDtypeStruct((N_TOK, D_MODEL), jnp.float32),
        scratch_shapes=[
            pltpu.VMEM((TC * TOP_K * ROW_TILES, LANES), jnp.float32),
            pltpu.SemaphoreType.DMA(()),
        ],
        compiler_params=_cparams("arbitrary"),
        name="moe_combine",
    )(pos.reshape(N_TOK // TC, 1, TC * TOP_K), ys, gates, x, mod, g_final.reshape(1, D_MODEL))


def kernel(x, c, w_mod, b_mod, norm_mix_g, norm_ffn_g, attn_w_qkv, attn_b_qkv, attn_w_o, attn_b_o,
           attn_sinks, rel_bias_table, conv_w_in, conv_b_in, conv_w_dw, conv_b_dw, conv_ln_g, conv_ln_b,
           conv_w_out, conv_b_out, router_w, router_b, moe_w_in, moe_b_in, moe_w_out, moe_b_out,
           final_norm_g):
    bf16 = jnp.bfloat16
    mod = _modulation(c, w_mod, b_mod)
    bias = _rel_bias(rel_bias_table)
    xf = x.reshape(N_TOK, D_MODEL)
    for i in range(DEPTH):
        j = i // 2
        if i % 2 == 0:
            qkv = _in_proj(xf, mod, norm_mix_g[i], attn_w_qkv[j].astype(bf16), attn_b_qkv[j], i,
                           glu=False, out_dtype=bf16)
            attn = _attention(qkv, bias, attn_sinks, j)
            xf, h_rows, idx, gates = _attn_out(attn, attn_w_o[j].astype(bf16), attn_b_o[j], xf, mod,
                                               norm_ffn_g[i], router_w[i], router_b[i], i)
        else:
            u = _in_proj(xf, mod, norm_mix_g[i], conv_w_in[j].astype(bf16), conv_b_in[j], i,
                         glu=True, out_dtype=jnp.float32)
            xf, h_rows, idx, gates = _conv_out(u, conv_w_dw[j], conv_b_dw[j], conv_ln_g[j], conv_ln_b[j],
                                               conv_w_out[j].astype(bf16), conv_b_out[j], xf, mod,
                                               norm_ffn_g[i], router_w[i], router_b[i], i)
        pos, item_blk, item_exp, item_lo, item_hi = _routing(idx)
        xs = _dispatch(h_rows, pos)
        ys = _grouped_mlp(xs, item_blk, item_exp, item_lo, item_hi, moe_w_in, moe_b_in, moe_w_out,
                          moe_b_out, i)
        xf = _combine(ys, pos, gates, xf, mod, final_norm_g, i, final_norm=(i == DEPTH - 1))
    return xf.reshape(BATCH, SEQ, D_MODEL)
```

```python
import functools
import math

import jax
import jax.numpy as jnp
from jax import lax
from jax.experimental import pallas as pl
from jax.experimental.pallas import tpu as pltpu

D_MODEL = 1024
BATCH = 8
SEQ = 2048
DEPTH = 4
N_TOK = BATCH * SEQ
HEAD_DIM = 64
N_Q_HEADS = 16
N_KV_HEADS = 4
GQA_GROUP = 4
WINDOW = 128
Q_DIM = 1024
KV_DIM = 256
QKV_DIM = 1536
NUM_BUCKETS = 32
MAX_EXACT = 16
MAX_DISTANCE = 128
CONV_WIDTH = 31
N_EXPERTS = 32
TOP_K = 4
D_EXPERT = 1024
SWIGLU_LIMIT = 7.0
SWIGLU_ALPHA = 1.702
NORM_EPS = 1e-5
NEG_INF = -1e30

LANES = 128
SUBLANES = 8
ROW_TILES = D_MODEL // LANES
N_SLOT = N_TOK * TOP_K

TM = 512
TM_ATTN = WINDOW
TC = 512
BM = 256
N_BLKP = N_SLOT // BM + N_EXPERTS
N_PAD = N_BLKP * BM
CONV_HALO = 32
CONV_ROWS = 64

VMEM_LIMIT = 56 * 1024 * 1024


def _cparams(*sem):
    return pltpu.CompilerParams(dimension_semantics=sem, vmem_limit_bytes=VMEM_LIMIT)


def _mod_kernel(c_ref, w_ref, b_ref, o_ref):
    c = c_ref[...]
    c_act = (c * jax.nn.sigmoid(c)).astype(jnp.bfloat16)
    o_ref[...] = jnp.dot(c_act, w_ref[...].astype(jnp.bfloat16),
                         preferred_element_type=jnp.float32) + b_ref[...]


def _modulation(c, w_mod, b_mod):
    out = pl.pallas_call(
        _mod_kernel,
        grid=(DEPTH, 6),
        in_specs=[
            pl.BlockSpec((BATCH, D_MODEL), lambda i, j: (0, 0)),
            pl.BlockSpec((None, D_MODEL, D_MODEL), lambda i, j: (i, 0, j)),
            pl.BlockSpec((None, None, 1, D_MODEL), lambda i, j: (i, j, 0, 0)),
        ],
        out_specs=pl.BlockSpec((None, None, BATCH, D_MODEL), lambda i, j: (i, j, 0, 0)),
        out_shape=jax.ShapeDtypeStruct((DEPTH, 6, BATCH, D_MODEL), jnp.float32),
        compiler_params=_cparams("arbitrary", "arbitrary"),
        name="adaln_modulation",
    )(c, w_mod, b_mod.reshape(DEPTH, 6, 1, D_MODEL))
    return jnp.transpose(out, (0, 2, 1, 3))


def _mod_spec(layer):
    return pl.BlockSpec((None, None, 6, D_MODEL), lambda i: (layer, i // (SEQ // TM), 0, 0))


def _rms_modulate(x, g, scale, shift):
    y = x * lax.rsqrt(jnp.mean(x * x, axis=-1, keepdims=True) + NORM_EPS)
    return (y * g) * (1.0 + scale) + shift


def _in_proj_kernel(x_ref, mod_ref, g_ref, w_ref, b_ref, o_ref, *, glu):
    h = _rms_modulate(x_ref[...], g_ref[...], mod_ref[1:2, :], mod_ref[0:1, :])
    u = jnp.dot(h.astype(jnp.bfloat16), w_ref[...], preferred_element_type=jnp.float32) + b_ref[...]
    if glu:
        half = u.shape[-1] // 2
        u = u[:, :half] * jax.nn.sigmoid(u[:, half:])
    o_ref[...] = u.astype(o_ref.dtype)


def _in_proj(x, mod, g, w, b, layer, *, glu, out_dtype):
    n_in = w.shape[-1]
    n_out = n_in // 2 if glu else n_in
    return pl.pallas_call(
        functools.partial(_in_proj_kernel, glu=glu),
        grid=(N_TOK // TM,),
        in_specs=[
            pl.BlockSpec((TM, D_MODEL), lambda i: (i, 0)),
            _mod_spec(layer),
            pl.BlockSpec((1, D_MODEL), lambda i: (0, 0)),
            pl.BlockSpec((D_MODEL, n_in), lambda i: (0, 0)),
            pl.BlockSpec((1, n_in), lambda i: (0, 0)),
        ],
        out_specs=pl.BlockSpec((TM, n_out), lambda i: (i, 0)),
        out_shape=jax.ShapeDtypeStruct((N_TOK, n_out), out_dtype),
        compiler_params=_cparams("arbitrary"),
        name="mixer_in_proj_glu" if glu else "mixer_in_proj",
    )(x, mod, g.reshape(1, D_MODEL), w, b.reshape(1, n_in))


def _band_geometry():
    dist = jnp.arange(WINDOW)[:, None] + WINDOW - jnp.arange(2 * WINDOW)[None, :]
    n = jnp.maximum(dist, 0)
    nf = jnp.maximum(n, 1).astype(jnp.float32)
    large = MAX_EXACT + (jnp.log(nf / MAX_EXACT) / math.log(MAX_DISTANCE / MAX_EXACT)
                         * (NUM_BUCKETS - MAX_EXACT)).astype(jnp.int32)
    large = jnp.minimum(large, NUM_BUCKETS - 1)
    bucket = jnp.where(n < MAX_EXACT, n, large)
    return dist, bucket


def _bias_kernel(table_ref, bucket_ref, dist_ref, o_ref):
    h = pl.program_id(0)
    bucket = bucket_ref[...]
    dist = dist_ref[...]
    acc = jnp.zeros(bucket.shape, jnp.float32)
    for b in range(NUM_BUCKETS):
        acc = jnp.where(bucket == b, table_ref[b, h], acc)
    in_window = (dist >= 0) & (dist < WINDOW)
    o_ref[...] = jnp.where(in_window, acc, NEG_INF)


def _rel_bias(table):
    dist, bucket = _band_geometry()
    return pl.pallas_call(
        _bias_kernel,
        grid=(N_Q_HEADS,),
        in_specs=[
            pl.BlockSpec(memory_space=pltpu.SMEM),
            pl.BlockSpec((WINDOW, 2 * WINDOW), lambda h: (0, 0)),
            pl.BlockSpec((WINDOW, 2 * WINDOW), lambda h: (0, 0)),
        ],
        out_specs=pl.BlockSpec((None, WINDOW, 2 * WINDOW), lambda h: (h, 0, 0)),
        out_shape=jax.ShapeDtypeStruct((N_Q_HEADS, WINDOW, 2 * WINDOW), jnp.float32),
        compiler_params=_cparams("arbitrary"),
        name="rel_bias",
    )(table, bucket.astype(jnp.int32), dist.astype(jnp.int32))


def _attn_kernel(sink_ref, q_ref, kv_ref, kvp_ref, bias_ref, o_ref, *, layer):
    i = pl.program_id(0)
    first_block = (i % (SEQ // WINDOW)) == 0
    q = q_ref[...]
    kv = jnp.concatenate([kvp_ref[...], kv_ref[...]], axis=0)
    col = lax.broadcasted_iota(jnp.int32, (WINDOW, 2 * WINDOW), 1)
    prev_ok = jnp.logical_or(col >= WINDOW, jnp.logical_not(first_block))
    outs = []
    for hk in range(N_KV_HEADS):
        k = kv[:, hk * HEAD_DIM:(hk + 1) * HEAD_DIM]
        v = kv[:, KV_DIM + hk * HEAD_DIM:KV_DIM + (hk + 1) * HEAD_DIM]
        for g in range(GQA_GROUP):
            hq = hk * GQA_GROUP + g
            qh = q[:, hq * HEAD_DIM:(hq + 1) * HEAD_DIM] * (HEAD_DIM ** -0.5)
            s = lax.dot_general(qh, k, (((1,), (1,)), ((), ())), preferred_element_type=jnp.float32)
            s = jnp.where(prev_ok, s + bias_ref[hq], NEG_INF)
            sink = sink_ref[layer, hq]
            m = jnp.maximum(jnp.max(s, axis=-1, keepdims=True), sink)
            p = jnp.exp(s - m)
            denom = jnp.sum(p, axis=-1, keepdims=True) + jnp.exp(sink - m)
            o = jnp.dot(p.astype(jnp.bfloat16), v, preferred_element_type=jnp.float32)
            outs.append((o / denom).astype(jnp.bfloat16))
    o_ref[...] = jnp.concatenate(outs, axis=-1)


def _attention(qkv, bias, sinks, layer):
    kv_col = Q_DIM // (2 * KV_DIM)
    return pl.pallas_call(
        functools.partial(_attn_kernel, layer=layer),
        grid=(N_TOK // WINDOW,),
        in_specs=[
            pl.BlockSpec(memory_space=pltpu.SMEM),
            pl.BlockSpec((WINDOW, Q_DIM), lambda i: (i, 0)),
            pl.BlockSpec((WINDOW, 2 * KV_DIM), lambda i: (i, kv_col)),
            pl.BlockSpec((WINDOW, 2 * KV_DIM), lambda i: (jnp.maximum(i - 1, 0), kv_col)),
            pl.BlockSpec((N_Q_HEADS, WINDOW, 2 * WINDOW), lambda i: (0, 0, 0)),
        ],
        out_specs=pl.BlockSpec((WINDOW, Q_DIM), lambda i: (i, 0)),
        out_shape=jax.ShapeDtypeStruct((N_TOK, Q_DIM), jnp.bfloat16),
        compiler_params=_cparams("arbitrary"),
        name="swa_attention",
    )(sinks, qkv, qkv, qkv, bias)


def _split_bf16(a):
    hi = a.astype(jnp.bfloat16)
    lo = (a - hi.astype(jnp.float32)).astype(jnp.bfloat16)
    return hi, lo


def _residual_ffn_router(m, x, mod, g_ffn, wr_hi, wr_lo, b_r, outs, carry_ref):
    xo_ref, h_ref, idx_ref, rank_ref, gate_ref, cnt_ref = outs
    x_new = x + mod[2:3, :] * m
    xo_ref[...] = x_new
    h = _rms_modulate(x_new, g_ffn, mod[4:5, :], mod[3:4, :])
    rows = h.shape[0]
    for s in range(ROW_TILES):
        h_ref[pl.ds(s, rows, stride=ROW_TILES), :] = h[:, s * LANES:(s + 1) * LANES]
    h_hi, h_lo = _split_bf16(h)
    logits = (jnp.dot(h_hi, wr_hi, preferred_element_type=jnp.float32)
              + jnp.dot(h_hi, wr_lo, preferred_element_type=jnp.float32)
              + jnp.dot(h_lo, wr_hi, preferred_element_type=jnp.float32)) + b_r
    lane = lax.broadcasted_iota(jnp.int32, logits.shape, 1)
    vals, ids = [], []
    for _ in range(TOP_K):
        best = jnp.max(logits, axis=-1, keepdims=True)
        arg = jnp.min(jnp.where(logits == best, lane, N_EXPERTS), axis=-1, keepdims=True)
        vals.append(best)
        ids.append(arg)
        logits = jnp.where(lane == arg, -jnp.inf, logits)
    exps = [jnp.exp(v - vals[0]) for v in vals]
    total = exps[0] + exps[1] + exps[2] + exps[3]
    idx_ref[...] = jnp.concatenate(ids, axis=-1)
    gate_ref[...] = jnp.concatenate([e / total for e in exps], axis=-1)

    @pl.when(pl.program_id(0) == 0)
    def _():
        carry_ref[...] = jnp.zeros_like(carry_ref)

    member = jnp.zeros(logits.shape, jnp.float32)
    for arg in ids:
        member = member + (lane == arg).astype(jnp.float32)
    earlier = (lax.broadcasted_iota(jnp.int32, (rows, rows), 0)
               > lax.broadcasted_iota(jnp.int32, (rows, rows), 1)).astype(jnp.bfloat16)
    before = jnp.dot(earlier, member.astype(jnp.bfloat16), preferred_element_type=jnp.float32) + carry_ref[...]
    ranks = [jnp.sum(jnp.where(lane == arg, before, 0.0), axis=-1, keepdims=True) for arg in ids]
    rank_ref[...] = jnp.concatenate(ranks, axis=-1).astype(jnp.int32)
    carry_ref[...] = carry_ref[...] + jnp.sum(member, axis=0, keepdims=True)
    cnt_ref[...] = carry_ref[...]


N_EPILOGUE_OUTS = 6


def _attn_out_kernel(a_ref, wo_ref, bo_ref, x_ref, mod_ref, g_ref, wrh_ref, wrl_ref, br_ref, *rest):
    m = jnp.dot(a_ref[...], wo_ref[...], preferred_element_type=jnp.float32) + bo_ref[...]
    _residual_ffn_router(m, x_ref[...], mod_ref[...], g_ref[...], wrh_ref[...], wrl_ref[...], br_ref[...],
                         rest[:N_EPILOGUE_OUTS], rest[N_EPILOGUE_OUTS])


def _epilogue_out_specs():
    return [
        pl.BlockSpec((TM, D_MODEL), lambda i: (i, 0)),
        pl.BlockSpec((TM * ROW_TILES, LANES), lambda i: (i, 0)),
        pl.BlockSpec((TM, TOP_K), lambda i: (i, 0)),
        pl.BlockSpec((TM, TOP_K), lambda i: (i, 0)),
        pl.BlockSpec((TM, TOP_K), lambda i: (i, 0)),
        pl.BlockSpec((1, N_EXPERTS), lambda i: (0, 0)),
    ]


def _epilogue_out_shapes():
    return [
        jax.ShapeDtypeStruct((N_TOK, D_MODEL), jnp.float32),
        jax.ShapeDtypeStruct((N_TOK * ROW_TILES, LANES), jnp.float32),
        jax.ShapeDtypeStruct((N_TOK, TOP_K), jnp.int32),
        jax.ShapeDtypeStruct((N_TOK, TOP_K), jnp.int32),
        jax.ShapeDtypeStruct((N_TOK, TOP_K), jnp.float32),
        jax.ShapeDtypeStruct((1, N_EXPERTS), jnp.float32),
    ]


def _epilogue_scratch():
    return [pltpu.VMEM((1, N_EXPERTS), jnp.float32)]


def _router_operands(w_r, b_r):
    hi, lo = _split_bf16(w_r)
    return hi, lo, b_r.reshape(1, N_EXPERTS)


def _router_specs():
    return [
        pl.BlockSpec((D_MODEL, N_EXPERTS), lambda i: (0, 0)),
        pl.BlockSpec((D_MODEL, N_EXPERTS), lambda i: (0, 0)),
        pl.BlockSpec((1, N_EXPERTS), lambda i: (0, 0)),
    ]


def _attn_out(attn, w_o, b_o, x, mod, g_ffn, w_r, b_r, layer):
    return pl.pallas_call(
        _attn_out_kernel,
        grid=(N_TOK // TM,),
        in_specs=[
            pl.BlockSpec((TM, Q_DIM), lambda i: (i, 0)),
            pl.BlockSpec((Q_DIM, D_MODEL), lambda i: (0, 0)),
            pl.BlockSpec((1, D_MODEL), lambda i: (0, 0)),
            pl.BlockSpec((TM, D_MODEL), lambda i: (i, 0)),
            _mod_spec(layer),
            pl.BlockSpec((1, D_MODEL), lambda i: (0, 0)),
        ] + _router_specs(),
        out_specs=_epilogue_out_specs(),
        out_shape=_epilogue_out_shapes(),
        scratch_shapes=_epilogue_scratch(),
        compiler_params=_cparams("arbitrary"),
        name="attn_out_router",
    )(attn, w_o, b_o.reshape(1, D_MODEL), x, mod, g_ffn.reshape(1, D_MODEL), *_router_operands(w_r, b_r))


def _conv_out_kernel(u_ref, up_ref, wdw_ref, bdw_ref, lng_ref, lnb_ref, wo_ref, bo_ref,
                     x_ref, mod_ref, g_ref, wrh_ref, wrl_ref, br_ref, *rest):
    outs = rest[:N_EPILOGUE_OUTS]
    carry_ref, ext_ref, y_ref = rest[N_EPILOGUE_OUTS:]
    i = pl.program_id(0)
    first_tile = (i % (SEQ // TM)) == 0
    ext_ref[0:CONV_HALO, :] = jnp.where(first_tile, 0.0, up_ref[...])
    ext_ref[CONV_HALO:, :] = u_ref[...]
    lead = CONV_HALO - (CONV_WIDTH - 1)
    span = CONV_ROWS + CONV_HALO

    def chunk(r, carry):
        r0 = pl.multiple_of(r * CONV_ROWS, CONV_ROWS)
        for cb in range(ROW_TILES):
            cols = slice(cb * LANES, (cb + 1) * LANES)
            e = ext_ref[pl.ds(r0, span), cols]
            acc = jnp.zeros((CONV_ROWS, LANES), jnp.float32)
            for sub in range(SUBLANES):
                shifted = e if sub == 0 else pltpu.roll(e, span - sub, 0)
                for a in range(span // SUBLANES):
                    tap = a * SUBLANES + sub - lead
                    if 0 <= tap < CONV_WIDTH:
                        acc = acc + wdw_ref[tap:tap + 1, cols] * shifted[a * SUBLANES:a * SUBLANES + CONV_ROWS]
            y_ref[pl.ds(r0, CONV_ROWS), cols] = acc
        return carry

    lax.fori_loop(0, TM // CONV_ROWS, chunk, 0)
    y = y_ref[...] + bdw_ref[...]
    mu = jnp.mean(y, axis=-1, keepdims=True)
    var = jnp.mean(jnp.square(y - mu), axis=-1, keepdims=True)
    z = (y - mu) * lax.rsqrt(var + NORM_EPS) * lng_ref[...] + lnb_ref[...]
    z = z * jax.nn.sigmoid(z)
    m = jnp.dot(z.astype(jnp.bfloat16), wo_ref[...], preferred_element_type=jnp.float32) + bo_ref[...]
    _residual_ffn_router(m, x_ref[...], mod_ref[...], g_ref[...], wrh_ref[...], wrl_ref[...], br_ref[...],
                         outs, carry_ref)


def _conv_out(u, w_dw, b_dw, ln_g, ln_b, w_o, b_o, x, mod, g_ffn, w_r, b_r, layer):
    halo_blocks = TM // CONV_HALO
    row = lambda a: a.reshape(1, D_MODEL)
    return pl.pallas_call(
        _conv_out_kernel,
        grid=(N_TOK // TM,),
        in_specs=[
            pl.BlockSpec((TM, D_MODEL), lambda i: (i, 0)),
            pl.BlockSpec((CONV_HALO, D_MODEL), lambda i: (jnp.maximum(i * halo_blocks - 1, 0), 0)),
            pl.BlockSpec((CONV_WIDTH, D_MODEL), lambda i: (0, 0)),
            pl.BlockSpec((1, D_MODEL), lambda i: (0, 0)),
            pl.BlockSpec((1, D_MODEL), lambda i: (0, 0)),
            pl.BlockSpec((1, D_MODEL), lambda i: (0, 0)),
            pl.BlockSpec((D_MODEL, D_MODEL), lambda i: (0, 0)),
            pl.BlockSpec((1, D_MODEL), lambda i: (0, 0)),
            pl.BlockSpec((TM, D_MODEL), lambda i: (i, 0)),
            _mod_spec(layer),
            pl.BlockSpec((1, D_MODEL), lambda i: (0, 0)),
        ] + _router_specs(),
        out_specs=_epilogue_out_specs(),
        out_shape=_epilogue_out_shapes(),
        scratch_shapes=_epilogue_scratch() + [
            pltpu.VMEM((TM + CONV_HALO, D_MODEL), jnp.float32),
            pltpu.VMEM((TM, D_MODEL), jnp.float32),
        ],
        compiler_params=_cparams("arbitrary"),
        name="conv_out_router",
    )(u, u, w_dw, row(b_dw), row(ln_g), row(ln_b), w_o, row(b_o), x, mod, row(g_ffn),
      *_router_operands(w_r, b_r))


def _routing(idx, rank, cnt):
    counts = cnt.reshape(N_EXPERTS).astype(jnp.int32)
    padded = (counts + BM - 1) // BM * BM
    pend = jnp.cumsum(padded)
    pstart = pend - padded
    experts = jnp.arange(N_EXPERTS, dtype=jnp.int32)
    pos = jnp.sum(jnp.where(idx[:, :, None] == experts[None, None, :], pstart[None, None, :], 0), axis=-1) + rank
    slot = jnp.arange(TOP_K, dtype=jnp.int32)[None, :] * N_TOK + jnp.arange(N_TOK, dtype=jnp.int32)[:, None]
    inv = jnp.full((N_PAD,), N_SLOT, jnp.int32).at[pos.reshape(-1)].set(slot.reshape(-1), unique_indices=True)
    blk = jnp.arange(N_BLKP, dtype=jnp.int32)
    blk_exp = jnp.minimum(jnp.searchsorted(pend, blk * BM, side="right"), N_EXPERTS - 1).astype(jnp.int32)
    n_live = pend[-1] // BM
    blk_exp = jnp.where(blk < n_live, blk_exp, blk_exp[n_live - 1])
    return inv, blk_exp


def _row_slice(row):
    return pl.ds(pl.multiple_of(row * ROW_TILES, ROW_TILES), ROW_TILES)


def _moe_kernel(exp_ref, inv_ref, h_hbm, win_ref, bin_ref, wout_ref, bout_ref, yk_hbm,
                xbuf0, xbuf1, ybuf0, ybuf1, win_bf, wout_bf, gsem, ssem):
    b = pl.program_id(0)
    last = pl.num_programs(0) - 1
    xbufs = (xbuf0, xbuf1)
    ybufs = (ybuf0, ybuf1)

    def gather(block, buf):
        for r in range(BM):
            s = inv_ref[block * BM + r]
            tok = jnp.minimum(s, N_SLOT - 1) & (N_TOK - 1)
            pltpu.make_async_copy(h_hbm.at[_row_slice(tok), :], xbufs[buf].at[_row_slice(r), :],
                                  gsem.at[buf]).start(priority=r % 2)

    def scatter(block, buf):
        for r in range(BM):
            s = inv_ref[block * BM + r]
            dst = jnp.where(s < N_SLOT, s, N_SLOT + buf * BM + r)
            pltpu.make_async_copy(ybufs[buf].at[_row_slice(r), :], yk_hbm.at[_row_slice(dst), :],
                                  ssem.at[buf]).start(priority=r % 2)

    def wait_rows(buffer, sem):
        pltpu.make_async_copy(buffer, buffer, sem).wait()

    new_expert = jnp.logical_or(b == 0, exp_ref[b] != exp_ref[jnp.maximum(b - 1, 0)])

    @pl.when(new_expert)
    def _():
        win_bf[...] = win_ref[...].astype(jnp.bfloat16)
        wout_bf[...] = wout_ref[...].astype(jnp.bfloat16)

    @pl.when(b == 0)
    def _():
        gather(0, 0)
        ybuf0[...] = jnp.zeros_like(ybuf0)
        ybuf1[...] = jnp.zeros_like(ybuf1)
        pltpu.make_async_copy(ybuf0, yk_hbm.at[pl.ds(N_SLOT * ROW_TILES, BM * ROW_TILES), :],
                              ssem.at[0]).start()

    def step(cur):
        nxt = 1 - cur
        wait_rows(xbufs[cur], gsem.at[cur])
        wait_rows(ybufs[cur], ssem.at[cur])
        gather(jnp.minimum(b + 1, last), nxt)
        scatter(jnp.maximum(b - 1, 0), nxt)

        xs = xbufs[cur]
        x = jnp.concatenate([xs[pl.ds(s, BM, stride=ROW_TILES), :] for s in range(ROW_TILES)], axis=-1)
        u = jnp.dot(x.astype(jnp.bfloat16), win_bf[...], preferred_element_type=jnp.float32) + bin_ref[...]
        glu = jnp.minimum(u[:, :D_EXPERT], SWIGLU_LIMIT)
        lin = jnp.clip(u[:, D_EXPERT:], -SWIGLU_LIMIT, SWIGLU_LIMIT)
        act = glu * jax.nn.sigmoid(SWIGLU_ALPHA * glu) * (lin + 1.0)
        y = jnp.dot(act.astype(jnp.bfloat16), wout_bf[...], preferred_element_type=jnp.float32) + bout_ref[...]
        for s in range(ROW_TILES):
            ybufs[cur][pl.ds(s, BM, stride=ROW_TILES), :] = y[:, s * LANES:(s + 1) * LANES]

        @pl.when(b == last)
        def _():
            scatter(b, cur)
            wait_rows(ybufs[nxt], ssem.at[nxt])
            wait_rows(ybufs[cur], ssem.at[cur])
            wait_rows(xbufs[nxt], gsem.at[nxt])

    for parity in range(2):
        pl.when(b % 2 == parity)(functools.partial(step, parity))


def _moe_experts(h_rows, inv, blk_exp, w_in, b_in, w_out, b_out, layer):
    by_expert = lambda b, ex, inv_: (layer, ex[b], 0, 0)
    grid_spec = pltpu.PrefetchScalarGridSpec(
        num_scalar_prefetch=2,
        grid=(N_BLKP,),
        in_specs=[
            pl.BlockSpec(memory_space=pl.ANY),
            pl.BlockSpec((None, None, D_MODEL, 2 * D_EXPERT), by_expert),
            pl.BlockSpec((None, None, 1, 2 * D_EXPERT), by_expert),
            pl.BlockSpec((None, None, D_EXPERT, D_MODEL), by_expert),
            pl.BlockSpec((None, None, 1, D_MODEL), by_expert),
        ],
        out_specs=pl.BlockSpec(memory_space=pl.ANY),
        scratch_shapes=[
            pltpu.VMEM((BM * ROW_TILES, LANES), jnp.float32),
            pltpu.VMEM((BM * ROW_TILES, LANES), jnp.float32),
            pltpu.VMEM((BM * ROW_TILES, LANES), jnp.float32),
            pltpu.VMEM((BM * ROW_TILES, LANES), jnp.float32),
            pltpu.VMEM((D_MODEL, 2 * D_EXPERT), jnp.bfloat16),
            pltpu.VMEM((D_EXPERT, D_MODEL), jnp.bfloat16),
            pltpu.SemaphoreType.DMA((2,)),
            pltpu.SemaphoreType.DMA((2,)),
        ],
    )
    return pl.pallas_call(
        _moe_kernel,
        grid_spec=grid_spec,
        out_shape=jax.ShapeDtypeStruct(((N_SLOT + 2 * BM) * ROW_TILES, LANES), jnp.float32),
        compiler_params=_cparams("arbitrary"),
        name="moe_experts",
    )(blk_exp, inv, h_rows, w_in, b_in.reshape(DEPTH, N_EXPERTS, 1, 2 * D_EXPERT), w_out,
      b_out.reshape(DEPTH, N_EXPERTS, 1, D_MODEL))


def _combine_kernel(y0_ref, y1_ref, y2_ref, y3_ref, gate_ref, x_ref, mod_ref, gfin_ref, o_ref, *, final_norm):
    gates = gate_ref[...]
    cols = []
    for s in range(ROW_TILES):
        acc = jnp.zeros((TC, LANES), jnp.float32)
        for k, y_ref in enumerate((y0_ref, y1_ref, y2_ref, y3_ref)):
            acc = acc + gates[:, k:k + 1] * y_ref[pl.ds(s, TC, stride=ROW_TILES), :]
        cols.append(acc)
    x_new = x_ref[...] + mod_ref[5:6, :] * jnp.concatenate(cols, axis=-1)
    if final_norm:
        x_new = x_new * lax.rsqrt(jnp.mean(x_new * x_new, axis=-1, keepdims=True) + NORM_EPS) * gfin_ref[...]
    o_ref[...] = x_new


def _combine(yk, gates, x, mod, g_final, layer, *, final_norm):
    tiles = N_TOK // TC
    slot_spec = lambda k: pl.BlockSpec((TC * ROW_TILES, LANES), lambda i: (k * tiles + i, 0))
    return pl.pallas_call(
        functools.partial(_combine_kernel, final_norm=final_norm),
        grid=(tiles,),
        in_specs=[slot_spec(k) for k in range(TOP_K)] + [
            pl.BlockSpec((TC, TOP_K), lambda i: (i, 0)),
            pl.BlockSpec((TC, D_MODEL), lambda i: (i, 0)),
            pl.BlockSpec((None, None, 6, D_MODEL), lambda i: (layer, i // (SEQ // TC), 0, 0)),
            pl.BlockSpec((1, D_MODEL), lambda i: (0, 0)),
        ],
        out_specs=pl.BlockSpec((TC, D_MODEL), lambda i: (i, 0)),
        out_shape=jax.ShapeDtypeStruct((N_TOK, D_MODEL), jnp.float32),
        compiler_params=_cparams("arbitrary"),
        name="moe_combine",
    )(yk, yk, yk, yk, gates, x, mod, g_final.reshape(1, D_MODEL))


def kernel(x, c, w_mod, b_mod, norm_mix_g, norm_ffn_g, attn_w_qkv, attn_b_qkv, attn_w_o, attn_b_o,
           attn_sinks, rel_bias_table, conv_w_in, conv_b_in, conv_w_dw, conv_b_dw, conv_ln_g, conv_ln_b,
           conv_w_out, conv_b_out, router_w, router_b, moe_w_in, moe_b_in, moe_w_out, moe_b_out,
           final_norm_g):
    bf16 = jnp.bfloat16
    mod = _modulation(c, w_mod, b_mod)
    bias = _rel_bias(rel_bias_table)
    xf = x.reshape(N_TOK, D_MODEL)
    for i in range(DEPTH):
        j = i // 2
        if i % 2 == 0:
            qkv = _in_proj(xf, mod, norm_mix_g[i], attn_w_qkv[j].astype(bf16), attn_b_qkv[j], i,
                           glu=False, out_dtype=bf16)
            attn = _attention(qkv, bias, attn_sinks, j)
            xf, h_rows, idx, rank, gates, cnt = _attn_out(attn, attn_w_o[j].astype(bf16), attn_b_o[j], xf, mod,
                                               norm_ffn_g[i], router_w[i], router_b[i], i)
        else:
            u = _in_proj(xf, mod, norm_mix_g[i], conv_w_in[j].astype(bf16), conv_b_in[j], i,
                         glu=True, out_dtype=jnp.float32)
            xf, h_rows, idx, rank, gates, cnt = _conv_out(u, conv_w_dw[j], conv_b_dw[j], conv_ln_g[j], conv_ln_b[j],
                                               conv_w_out[j].astype(bf16), conv_b_out[j], xf, mod,
                                               norm_ffn_g[i], router_w[i], router_b[i], i)
        inv, blk_exp = _routing(idx, rank, cnt)
        yk = _moe_experts(h_rows, inv, blk_exp, moe_w_in, moe_b_in, moe_w_out, moe_b_out, i)
        xf = _combine(yk, gates, xf, mod, final_norm_g, i, final_norm=(i == DEPTH - 1))
    return xf.reshape(BATCH, SEQ, D_MODEL)
```

```python
import functools
import math

import jax
import jax.numpy as jnp
from jax import lax
from jax.experimental import pallas as pl
from jax.experimental.pallas import tpu as pltpu

D_MODEL = 1024
BATCH = 8
SEQ = 2048
DEPTH = 4
N_TOK = BATCH * SEQ
HEAD_DIM = 64
N_Q_HEADS = 16
N_KV_HEADS = 4
GQA_GROUP = 4
WINDOW = 128
Q_DIM = 1024
KV_DIM = 256
QKV_DIM = 1536
NUM_BUCKETS = 32
MAX_EXACT = 16
MAX_DISTANCE = 128
CONV_WIDTH = 31
N_EXPERTS = 32
TOP_K = 4
D_EXPERT = 1024
SWIGLU_LIMIT = 7.0
SWIGLU_ALPHA = 1.702
NORM_EPS = 1e-5
NEG_INF = -1e30

LANES = 128
SUBLANES = 8
ROW_TILES = D_MODEL // LANES
N_SLOT = N_TOK * TOP_K

TM = 512
TM_ATTN = WINDOW
N_TILES = N_TOK // TM
SEG = 16
TILE_ROWS = -(-(TM * TOP_K + N_EXPERTS * (SEG - 1)) // 256) * 256
TILE_UNITS = TILE_ROWS // SEG
BM = 256
BLK_UNITS = BM // SEG
SLACK_UNITS = TILE_UNITS - TM * TOP_K // SEG


def _bits_down(n):
    return tuple(1 << i for i in reversed(range(n.bit_length())))
N_BLKP = N_TILES * TILE_ROWS // BM + N_EXPERTS
LOCAL_ROWS = N_TILES * TILE_ROWS
CONV_HALO = 32
CONV_ROWS = 64

VMEM_LIMIT = 56 * 1024 * 1024


def _cparams(*sem):
    return pltpu.CompilerParams(dimension_semantics=sem, vmem_limit_bytes=VMEM_LIMIT)


def _mod_kernel(c_ref, w_ref, b_ref, o_ref):
    c = c_ref[...]
    c_act = (c * jax.nn.sigmoid(c)).astype(jnp.bfloat16)
    o_ref[...] = jnp.dot(c_act, w_ref[...].astype(jnp.bfloat16),
                         preferred_element_type=jnp.float32) + b_ref[...]


def _modulation(c, w_mod, b_mod):
    out = pl.pallas_call(
        _mod_kernel,
        grid=(DEPTH, 6),
        in_specs=[
            pl.BlockSpec((BATCH, D_MODEL), lambda i, j: (0, 0)),
            pl.BlockSpec((None, D_MODEL, D_MODEL), lambda i, j: (i, 0, j)),
            pl.BlockSpec((None, None, 1, D_MODEL), lambda i, j: (i, j, 0, 0)),
        ],
        out_specs=pl.BlockSpec((None, None, BATCH, D_MODEL), lambda i, j: (i, j, 0, 0)),
        out_shape=jax.ShapeDtypeStruct((DEPTH, 6, BATCH, D_MODEL), jnp.float32),
        compiler_params=_cparams("arbitrary", "arbitrary"),
        name="adaln_modulation",
    )(c, w_mod, b_mod.reshape(DEPTH, 6, 1, D_MODEL))
    return jnp.transpose(out, (0, 2, 1, 3))


def _mod_spec(layer):
    return pl.BlockSpec((None, None, 6, D_MODEL), lambda i: (layer, i // (SEQ // TM), 0, 0))


def _rms_modulate(x, g, scale, shift):
    y = x * lax.rsqrt(jnp.mean(x * x, axis=-1, keepdims=True) + NORM_EPS)
    return (y * g) * (1.0 + scale) + shift


def _in_proj_kernel(x_ref, mod_ref, g_ref, w_ref, b_ref, o_ref, *, glu):
    h = _rms_modulate(x_ref[...], g_ref[...], mod_ref[1:2, :], mod_ref[0:1, :])
    u = jnp.dot(h.astype(jnp.bfloat16), w_ref[...], preferred_element_type=jnp.float32) + b_ref[...]
    if glu:
        half = u.shape[-1] // 2
        u = u[:, :half] * jax.nn.sigmoid(u[:, half:])
    o_ref[...] = u.astype(o_ref.dtype)


def _in_proj(x, mod, g, w, b, layer, *, glu, out_dtype):
    n_in = w.shape[-1]
    n_out = n_in // 2 if glu else n_in
    return pl.pallas_call(
        functools.partial(_in_proj_kernel, glu=glu),
        grid=(N_TOK // TM,),
        in_specs=[
            pl.BlockSpec((TM, D_MODEL), lambda i: (i, 0)),
            _mod_spec(layer),
            pl.BlockSpec((1, D_MODEL), lambda i: (0, 0)),
            pl.BlockSpec((D_MODEL, n_in), lambda i: (0, 0)),
            pl.BlockSpec((1, n_in), lambda i: (0, 0)),
        ],
        out_specs=pl.BlockSpec((TM, n_out), lambda i: (i, 0)),
        out_shape=jax.ShapeDtypeStruct((N_TOK, n_out), out_dtype),
        compiler_params=_cparams("arbitrary"),
        name="mixer_in_proj_glu" if glu else "mixer_in_proj",
    )(x, mod, g.reshape(1, D_MODEL), w, b.reshape(1, n_in))


def _band_geometry():
    dist = jnp.arange(WINDOW)[:, None] + WINDOW - jnp.arange(2 * WINDOW)[None, :]
    n = jnp.maximum(dist, 0)
    nf = jnp.maximum(n, 1).astype(jnp.float32)
    large = MAX_EXACT + (jnp.log(nf / MAX_EXACT) / math.log(MAX_DISTANCE / MAX_EXACT)
                         * (NUM_BUCKETS - MAX_EXACT)).astype(jnp.int32)
    large = jnp.minimum(large, NUM_BUCKETS - 1)
    bucket = jnp.where(n < MAX_EXACT, n, large)
    return dist, bucket


def _bias_kernel(table_ref, bucket_ref, dist_ref, o_ref):
    h = pl.program_id(0)
    bucket = bucket_ref[...]
    dist = dist_ref[...]
    acc = jnp.zeros(bucket.shape, jnp.float32)
    for b in range(NUM_BUCKETS):
        acc = jnp.where(bucket == b, table_ref[b, h], acc)
    in_window = (dist >= 0) & (dist < WINDOW)
    o_ref[...] = jnp.where(in_window, acc, NEG_INF)


def _rel_bias(table):
    dist, bucket = _band_geometry()
    return pl.pallas_call(
        _bias_kernel,
        grid=(N_Q_HEADS,),
        in_specs=[
            pl.BlockSpec(memory_space=pltpu.SMEM),
            pl.BlockSpec((WINDOW, 2 * WINDOW), lambda h: (0, 0)),
            pl.BlockSpec((WINDOW, 2 * WINDOW), lambda h: (0, 0)),
        ],
        out_specs=pl.BlockSpec((None, WINDOW, 2 * WINDOW), lambda h: (h, 0, 0)),
        out_shape=jax.ShapeDtypeStruct((N_Q_HEADS, WINDOW, 2 * WINDOW), jnp.float32),
        compiler_params=_cparams("arbitrary"),
        name="rel_bias",
    )(table, bucket.astype(jnp.int32), dist.astype(jnp.int32))


def _attn_kernel(sink_ref, q_ref, kv_ref, kvp_ref, bias_ref, o_ref, *, layer):
    i = pl.program_id(0)
    first_block = (i % (SEQ // WINDOW)) == 0
    q = q_ref[...]
    kv = jnp.concatenate([kvp_ref[...], kv_ref[...]], axis=0)
    col = lax.broadcasted_iota(jnp.int32, (WINDOW, 2 * WINDOW), 1)
    prev_ok = jnp.logical_or(col >= WINDOW, jnp.logical_not(first_block))
    outs = []
    for hk in range(N_KV_HEADS):
        k = kv[:, hk * HEAD_DIM:(hk + 1) * HEAD_DIM]
        v = kv[:, KV_DIM + hk * HEAD_DIM:KV_DIM + (hk + 1) * HEAD_DIM]
        for g in range(GQA_GROUP):
            hq = hk * GQA_GROUP + g
            qh = q[:, hq * HEAD_DIM:(hq + 1) * HEAD_DIM] * (HEAD_DIM ** -0.5)
            s = lax.dot_general(qh, k, (((1,), (1,)), ((), ())), preferred_element_type=jnp.float32)
            s = jnp.where(prev_ok, s + bias_ref[hq], NEG_INF)
            sink = sink_ref[layer, hq]
            m = jnp.maximum(jnp.max(s, axis=-1, keepdims=True), sink)
            p = jnp.exp(s - m)
            denom = jnp.sum(p, axis=-1, keepdims=True) + jnp.exp(sink - m)
            o = jnp.dot(p.astype(jnp.bfloat16), v, preferred_element_type=jnp.float32)
            outs.append((o / denom).astype(jnp.bfloat16))
    o_ref[...] = jnp.concatenate(outs, axis=-1)


def _attention(qkv, bias, sinks, layer):
    kv_col = Q_DIM // (2 * KV_DIM)
    return pl.pallas_call(
        functools.partial(_attn_kernel, layer=layer),
        grid=(N_TOK // WINDOW,),
        in_specs=[
            pl.BlockSpec(memory_space=pltpu.SMEM),
            pl.BlockSpec((WINDOW, Q_DIM), lambda i: (i, 0)),
            pl.BlockSpec((WINDOW, 2 * KV_DIM), lambda i: (i, kv_col)),
            pl.BlockSpec((WINDOW, 2 * KV_DIM), lambda i: (jnp.maximum(i - 1, 0), kv_col)),
            pl.BlockSpec((N_Q_HEADS, WINDOW, 2 * WINDOW), lambda i: (0, 0, 0)),
        ],
        out_specs=pl.BlockSpec((WINDOW, Q_DIM), lambda i: (i, 0)),
        out_shape=jax.ShapeDtypeStruct((N_TOK, Q_DIM), jnp.bfloat16),
        compiler_params=_cparams("arbitrary"),
        name="swa_attention",
    )(sinks, qkv, qkv, qkv, bias)


def _split_bf16(a):
    hi = a.astype(jnp.bfloat16)
    lo = (a - hi.astype(jnp.float32)).astype(jnp.bfloat16)
    return hi, lo


def _residual_ffn_router(m, x, mod, g_ffn, wr_hi, wr_lo, b_r, outs):
    xo_ref, xs_ref, pos_ref, gate_ref, units_ref = outs
    x_new = x + mod[2:3, :] * m
    xo_ref[...] = x_new
    h = _rms_modulate(x_new, g_ffn, mod[4:5, :], mod[3:4, :])
    rows = h.shape[0]
    h_hi, h_lo = _split_bf16(h)
    logits = (jnp.dot(h_hi, wr_hi, preferred_element_type=jnp.float32)
              + jnp.dot(h_hi, wr_lo, preferred_element_type=jnp.float32)
              + jnp.dot(h_lo, wr_hi, preferred_element_type=jnp.float32)) + b_r
    lane = lax.broadcasted_iota(jnp.int32, logits.shape, 1)
    vals, ids = [], []
    for _ in range(TOP_K):
        best = jnp.max(logits, axis=-1, keepdims=True)
        arg = jnp.min(jnp.where(logits == best, lane, N_EXPERTS), axis=-1, keepdims=True)
        vals.append(best)
        ids.append(arg)
        logits = jnp.where(lane == arg, -jnp.inf, logits)
    exps = [jnp.exp(v - vals[0]) for v in vals]
    total = exps[0] + exps[1] + exps[2] + exps[3]
    gate_ref[...] = jnp.concatenate([e / total for e in exps], axis=-1)

    member = jnp.zeros(logits.shape, jnp.float32)
    for arg in ids:
        member = member + (lane == arg).astype(jnp.float32)
    units = jnp.floor((jnp.sum(member, axis=0, keepdims=True) + (SEG - 1)) * (1.0 / SEG))
    units_ref[...] = units
    lower_experts = (lax.broadcasted_iota(jnp.int32, (N_EXPERTS, N_EXPERTS), 0)
                     < lax.broadcasted_iota(jnp.int32, (N_EXPERTS, N_EXPERTS), 1)).astype(jnp.bfloat16)
    seg_start = SEG * jnp.dot(jnp.broadcast_to(units, (SUBLANES, N_EXPERTS)).astype(jnp.bfloat16), lower_experts,
                              preferred_element_type=jnp.float32)[0:1, :]
    earlier = (lax.broadcasted_iota(jnp.int32, (rows, rows), 0)
               > lax.broadcasted_iota(jnp.int32, (rows, rows), 1)).astype(jnp.bfloat16)
    place = seg_start + jnp.dot(earlier, member.astype(jnp.bfloat16), preferred_element_type=jnp.float32)
    pos = [jnp.sum(jnp.where(lane == arg, place, 0.0), axis=-1, keepdims=True).astype(jnp.int32) for arg in ids]
    pos_ref[...] = jnp.concatenate(pos, axis=-1)

    q = lax.broadcasted_iota(jnp.int32, (rows, TILE_ROWS), 1)
    hit = (q == pos[0]) | (q == pos[1]) | (q == pos[2]) | (q == pos[3])
    onehot = jnp.where(hit, 1.0, 0.0).astype(jnp.bfloat16)
    h_bf = h.astype(jnp.bfloat16)
    half = D_MODEL // 2
    for c in range(2):
        xs = lax.dot_general(onehot, h_bf[:, c * half:(c + 1) * half], (((0,), (0,)), ((), ())),
                             preferred_element_type=jnp.float32)
        xs_ref[:, c * half:(c + 1) * half] = xs.astype(jnp.bfloat16)


N_EPILOGUE_OUTS = 5


def _attn_out_kernel(a_ref, wo_ref, bo_ref, x_ref, mod_ref, g_ref, wrh_ref, wrl_ref, br_ref, *outs):
    m = jnp.dot(a_ref[...], wo_ref[...], preferred_element_type=jnp.float32) + bo_ref[...]
    _residual_ffn_router(m, x_ref[...], mod_ref[...], g_ref[...], wrh_ref[...], wrl_ref[...], br_ref[...], outs)


def _epilogue_out_specs():
    return [
        pl.BlockSpec((TM, D_MODEL), lambda i: (i, 0)),
        pl.BlockSpec((TILE_ROWS, D_MODEL), lambda i: (i, 0)),
        pl.BlockSpec((TM, TOP_K), lambda i: (i, 0)),
        pl.BlockSpec((TM, TOP_K), lambda i: (i, 0)),
        pl.BlockSpec((None, 1, N_EXPERTS), lambda i: (i, 0, 0)),
    ]


def _epilogue_out_shapes():
    return [
        jax.ShapeDtypeStruct((N_TOK, D_MODEL), jnp.float32),
        jax.ShapeDtypeStruct((LOCAL_ROWS, D_MODEL), jnp.bfloat16),
        jax.ShapeDtypeStruct((N_TOK, TOP_K), jnp.int32),
        jax.ShapeDtypeStruct((N_TOK, TOP_K), jnp.float32),
        jax.ShapeDtypeStruct((N_TILES, 1, N_EXPERTS), jnp.float32),
    ]


def _router_operands(w_r, b_r):
    hi, lo = _split_bf16(w_r)
    return hi, lo, b_r.reshape(1, N_EXPERTS)


def _router_specs():
    return [
        pl.BlockSpec((D_MODEL, N_EXPERTS), lambda i: (0, 0)),
        pl.BlockSpec((D_MODEL, N_EXPERTS), lambda i: (0, 0)),
        pl.BlockSpec((1, N_EXPERTS), lambda i: (0, 0)),
    ]


def _attn_out(attn, w_o, b_o, x, mod, g_ffn, w_r, b_r, layer):
    return pl.pallas_call(
        _attn_out_kernel,
        grid=(N_TOK // TM,),
        in_specs=[
            pl.BlockSpec((TM, Q_DIM), lambda i: (i, 0)),
            pl.BlockSpec((Q_DIM, D_MODEL), lambda i: (0, 0)),
            pl.BlockSpec((1, D_MODEL), lambda i: (0, 0)),
            pl.BlockSpec((TM, D_MODEL), lambda i: (i, 0)),
            _mod_spec(layer),
            pl.BlockSpec((1, D_MODEL), lambda i: (0, 0)),
        ] + _router_specs(),
        out_specs=_epilogue_out_specs(),
        out_shape=_epilogue_out_shapes(),
        compiler_params=_cparams("arbitrary"),
        name="attn_out_router",
    )(attn, w_o, b_o.reshape(1, D_MODEL), x, mod, g_ffn.reshape(1, D_MODEL), *_router_operands(w_r, b_r))


def _conv_out_kernel(u_ref, up_ref, wdw_ref, bdw_ref, lng_ref, lnb_ref, wo_ref, bo_ref,
                     x_ref, mod_ref, g_ref, wrh_ref, wrl_ref, br_ref, *rest):
    outs = rest[:N_EPILOGUE_OUTS]
    ext_ref, y_ref = rest[N_EPILOGUE_OUTS:]
    i = pl.program_id(0)
    first_tile = (i % (SEQ // TM)) == 0
    ext_ref[0:CONV_HALO, :] = jnp.where(first_tile, 0.0, up_ref[...])
    ext_ref[CONV_HALO:, :] = u_ref[...]
    lead = CONV_HALO - (CONV_WIDTH - 1)
    span = CONV_ROWS + CONV_HALO

    def chunk(r, carry):
        r0 = pl.multiple_of(r * CONV_ROWS, CONV_ROWS)
        for cb in range(ROW_TILES):
            cols = slice(cb * LANES, (cb + 1) * LANES)
            e = ext_ref[pl.ds(r0, span), cols]
            acc = jnp.zeros((CONV_ROWS, LANES), jnp.float32)
            for sub in range(SUBLANES):
                shifted = e if sub == 0 else pltpu.roll(e, span - sub, 0)
                for a in range(span // SUBLANES):
                    tap = a * SUBLANES + sub - lead
                    if 0 <= tap < CONV_WIDTH:
                        acc = acc + wdw_ref[tap:tap + 1, cols] * shifted[a * SUBLANES:a * SUBLANES + CONV_ROWS]
            y_ref[pl.ds(r0, CONV_ROWS), cols] = acc
        return carry

    lax.fori_loop(0, TM // CONV_ROWS, chunk, 0)
    y = y_ref[...] + bdw_ref[...]
    mu = jnp.mean(y, axis=-1, keepdims=True)
    var = jnp.mean(jnp.square(y - mu), axis=-1, keepdims=True)
    z = (y - mu) * lax.rsqrt(var + NORM_EPS) * lng_ref[...] + lnb_ref[...]
    z = z * jax.nn.sigmoid(z)
    m = jnp.dot(z.astype(jnp.bfloat16), wo_ref[...], preferred_element_type=jnp.float32) + bo_ref[...]
    _residual_ffn_router(m, x_ref[...], mod_ref[...], g_ref[...], wrh_ref[...], wrl_ref[...], br_ref[...],
                         outs)


def _conv_out(u, w_dw, b_dw, ln_g, ln_b, w_o, b_o, x, mod, g_ffn, w_r, b_r, layer):
    halo_blocks = TM // CONV_HALO
    row = lambda a: a.reshape(1, D_MODEL)
    return pl.pallas_call(
        _conv_out_kernel,
        grid=(N_TOK // TM,),
        in_specs=[
            pl.BlockSpec((TM, D_MODEL), lambda i: (i, 0)),
            pl.BlockSpec((CONV_HALO, D_MODEL), lambda i: (jnp.maximum(i * halo_blocks - 1, 0), 0)),
            pl.BlockSpec((CONV_WIDTH, D_MODEL), lambda i: (0, 0)),
            pl.BlockSpec((1, D_MODEL), lambda i: (0, 0)),
            pl.BlockSpec((1, D_MODEL), lambda i: (0, 0)),
            pl.BlockSpec((1, D_MODEL), lambda i: (0, 0)),
            pl.BlockSpec((D_MODEL, D_MODEL), lambda i: (0, 0)),
            pl.BlockSpec((1, D_MODEL), lambda i: (0, 0)),
            pl.BlockSpec((TM, D_MODEL), lambda i: (i, 0)),
            _mod_spec(layer),
            pl.BlockSpec((1, D_MODEL), lambda i: (0, 0)),
        ] + _router_specs(),
        out_specs=_epilogue_out_specs(),
        out_shape=_epilogue_out_shapes(),
        scratch_shapes=[
            pltpu.VMEM((TM + CONV_HALO, D_MODEL), jnp.float32),
            pltpu.VMEM((TM, D_MODEL), jnp.float32),
        ],
        compiler_params=_cparams("arbitrary"),
        name="conv_out_router",
    )(u, u, w_dw, row(b_dw), row(ln_g), row(ln_b), w_o, row(b_o), x, mod, row(g_ffn),
      *_router_operands(w_r, b_r))


def _routing(units):
    per_tile = units.reshape(N_TILES, N_EXPERTS).astype(jnp.int32)
    seg_off = jnp.cumsum(per_tile, axis=1) - per_tile
    used = jnp.sum(per_tile, axis=1)
    zero = jnp.zeros((N_EXPERTS, 1), jnp.int32)
    cum = jnp.concatenate([zero, jnp.cumsum(per_tile.T, axis=1)], axis=1)
    total = cum[:, -1]
    n_blocks = (total + BLK_UNITS - 1) // BLK_UNITS
    blk_end = jnp.cumsum(n_blocks)
    blk_start = blk_end - n_blocks
    n_live = blk_end[-1]
    blk = jnp.arange(N_BLKP, dtype=jnp.int32)
    blk_exp = jnp.minimum(jnp.sum(blk[:, None] >= blk_end[None, :], axis=1), N_EXPERTS - 1)
    blk_exp = jnp.where(blk < n_live, blk_exp, blk_exp[n_live - 1]).astype(jnp.int32)
    blk_at = (blk - blk_start[blk_exp]) * BLK_UNITS
    rows = cum[blk_exp]
    tile_lo = jnp.clip(jnp.sum(rows <= blk_at[:, None], axis=1) - 1, 0, N_TILES - 1)
    tile_hi = jnp.clip(jnp.sum(rows < (blk_at + BLK_UNITS)[:, None], axis=1), tile_lo, N_TILES)
    i32 = lambda a: a.astype(jnp.int32)
    return dict(blk_exp=blk_exp, blk_at=i32(blk_at), tile_lo=i32(tile_lo), tile_hi=i32(tile_hi),
                cum=i32(cum.reshape(-1)), seg_off=i32(seg_off.reshape(-1)), total=i32(total),
                n_live=i32(n_live.reshape(1)), used=i32(used))


def _unit_rows(unit, n_units):
    return pl.ds(pl.multiple_of(unit * SEG, SEG), n_units * SEG)


def _moe_kernel(exp_ref, at_ref, lo_ref, hi_ref, cum_ref, off_ref, total_ref, live_ref, used_ref,
                xs_hbm, win_ref, bin_ref, wout_ref, bout_ref, ys_hbm,
                xbuf0, xbuf1, ybuf0, ybuf1, zbuf, win_bf, wout_bf, gsem, ssem, zsem):
    b = pl.program_id(0)
    last = pl.num_programs(0) - 1
    xbufs = (xbuf0, xbuf1)
    ybufs = (ybuf0, ybuf1)

    def for_pieces(n, piece, most=BLK_UNITS):
        for size in _bits_down(most):
            @pl.when((n & size) != 0)
            def _():
                piece(n & ~(2 * size - 1), size)

    def zero_tail(tile, act):
        used = used_ref[tile]

        def piece(start, size):
            act(pltpu.make_async_copy(zbuf.at[_unit_rows(0, size), :],
                                      ys_hbm.at[_unit_rows(tile * TILE_UNITS + used + start, size), :], zsem))

        for_pieces(TILE_UNITS - used, piece, SLACK_UNITS)

    def block_segments(block, copy, fill):
        e = exp_ref[block]
        at = at_ref[block]

        def per_tile(j, carry):
            seg_lo = cum_ref[e * (N_TILES + 1) + j]
            seg_hi = cum_ref[e * (N_TILES + 1) + j + 1]
            lo = jnp.maximum(seg_lo, at)
            n = jnp.maximum(jnp.minimum(seg_hi, at + BLK_UNITS) - lo, 0)
            local = j * TILE_UNITS + off_ref[j * N_EXPERTS + e] + (lo - seg_lo)
            for_pieces(n, lambda start, size: copy(local + start, lo - at + start, size))
            return carry

        lax.fori_loop(lo_ref[block], hi_ref[block], per_tile, 0)
        filled = jnp.clip(total_ref[e] - at, 0, BLK_UNITS)
        for_pieces(BLK_UNITS - filled, lambda start, size: fill(filled + start, size))

    def gather(block, buf):
        def copy(local, unit, size):
            pltpu.make_async_copy(xs_hbm.at[_unit_rows(local, size), :], xbufs[buf].at[_unit_rows(unit, size), :],
                                  gsem.at[buf]).start()

        block_segments(block, copy, lambda unit, size: copy(0, unit, size))

    def scatter(block, buf):
        def copy(local, unit, size):
            pltpu.make_async_copy(ybufs[buf].at[_unit_rows(unit, size), :], ys_hbm.at[_unit_rows(local, size), :],
                                  ssem.at[buf]).start()

        trash = N_TILES * TILE_UNITS + buf * BLK_UNITS
        block_segments(block, copy, lambda unit, size: copy(trash + unit, unit, size))

    def wait_rows(buffer, sem):
        pltpu.make_async_copy(buffer, buffer, sem).wait()

    new_expert = jnp.logical_or(b == 0, exp_ref[b] != exp_ref[jnp.maximum(b - 1, 0)])

    @pl.when(new_expert)
    def _():
        win_bf[...] = win_ref[...].astype(jnp.bfloat16)
        wout_bf[...] = wout_ref[...].astype(jnp.bfloat16)

    @pl.when(b == 0)
    def _():
        gather(0, 0)
        ybuf0[...] = jnp.zeros_like(ybuf0)
        ybuf1[...] = jnp.zeros_like(ybuf1)
        zbuf[...] = jnp.zeros_like(zbuf)
        pltpu.make_async_copy(ybuf0, ys_hbm.at[_unit_rows(N_TILES * TILE_UNITS, BLK_UNITS), :], ssem.at[0]).start()
        trash1 = pltpu.make_async_copy(zbuf.at[_unit_rows(0, BLK_UNITS), :],
                                       ys_hbm.at[_unit_rows(N_TILES * TILE_UNITS + BLK_UNITS, BLK_UNITS), :], zsem)
        trash1.start()
        trash1.wait()

    @pl.when(jnp.logical_and(b >= 1, b <= N_TILES))
    def _():
        zero_tail(b - 1, lambda copy: copy.wait())

    @pl.when(b < N_TILES)
    def _():
        zero_tail(b, lambda copy: copy.start())

    def step(cur):
        nxt = 1 - cur
        wait_rows(xbufs[cur], gsem.at[cur])
        wait_rows(ybufs[cur], ssem.at[cur])
        gather(jnp.minimum(b + 1, last), nxt)
        scatter(jnp.maximum(b - 1, 0), nxt)

        @pl.when(b < live_ref[0])
        def _():
            u = jnp.dot(xbufs[cur][...], win_bf[...], preferred_element_type=jnp.float32) + bin_ref[...]
            glu = jnp.minimum(u[:, :D_EXPERT], SWIGLU_LIMIT)
            lin = jnp.clip(u[:, D_EXPERT:], -SWIGLU_LIMIT, SWIGLU_LIMIT)
            act = glu * jax.nn.sigmoid(SWIGLU_ALPHA * glu) * (lin + 1.0)
            y = jnp.dot(act.astype(jnp.bfloat16), wout_bf[...], preferred_element_type=jnp.float32) + bout_ref[...]
            ybufs[cur][...] = y.astype(jnp.bfloat16)

        @pl.when(b == last)
        def _():
            scatter(b, cur)
            wait_rows(ybufs[nxt], ssem.at[nxt])
            wait_rows(ybufs[cur], ssem.at[cur])
            wait_rows(xbufs[nxt], gsem.at[nxt])

    for parity in range(2):
        pl.when(b % 2 == parity)(functools.partial(step, parity))


def _moe_experts(xs_local, route, w_in, b_in, w_out, b_out, layer):
    by_expert = lambda b, ex, *_: (layer, ex[b], 0, 0)
    tables = [route[k] for k in
              ("blk_exp", "blk_at", "tile_lo", "tile_hi", "cum", "seg_off", "total", "n_live", "used")]
    grid_spec = pltpu.PrefetchScalarGridSpec(
        num_scalar_prefetch=len(tables),
        grid=(N_BLKP,),
        in_specs=[
            pl.BlockSpec(memory_space=pl.ANY),
            pl.BlockSpec((None, None, D_MODEL, 2 * D_EXPERT), by_expert),
            pl.BlockSpec((None, None, 1, 2 * D_EXPERT), by_expert),
            pl.BlockSpec((None, None, D_EXPERT, D_MODEL), by_expert),
            pl.BlockSpec((None, None, 1, D_MODEL), by_expert),
        ],
        out_specs=pl.BlockSpec(memory_space=pl.ANY),
        scratch_shapes=[
            pltpu.VMEM((BM, D_MODEL), jnp.bfloat16),
            pltpu.VMEM((BM, D_MODEL), jnp.bfloat16),
            pltpu.VMEM((BM, D_MODEL), jnp.bfloat16),
            pltpu.VMEM((BM, D_MODEL), jnp.bfloat16),
            pltpu.VMEM((SLACK_UNITS * SEG, D_MODEL), jnp.bfloat16),
            pltpu.VMEM((D_MODEL, 2 * D_EXPERT), jnp.bfloat16),
            pltpu.VMEM((D_EXPERT, D_MODEL), jnp.bfloat16),
            pltpu.SemaphoreType.DMA((2,)),
            pltpu.SemaphoreType.DMA((2,)),
            pltpu.SemaphoreType.DMA(()),
        ],
    )
    return pl.pallas_call(
        _moe_kernel,
        grid_spec=grid_spec,
        out_shape=jax.ShapeDtypeStruct((LOCAL_ROWS + 2 * BM, D_MODEL), jnp.bfloat16),
        compiler_params=_cparams("arbitrary"),
        name="moe_experts",
    )(*tables, xs_local, w_in, b_in.reshape(DEPTH, N_EXPERTS, 1, 2 * D_EXPERT), w_out,
      b_out.reshape(DEPTH, N_EXPERTS, 1, D_MODEL))


def _combine_kernel(ys_ref, pos_ref, gate_ref, x_ref, mod_ref, gfin_ref, o_ref, *, final_norm):
    pos = pos_ref[...]
    gates = gate_ref[...]
    q = lax.broadcasted_iota(jnp.int32, (TM, TILE_ROWS), 1)
    weights = jnp.zeros((TM, TILE_ROWS), jnp.float32)
    for k in range(TOP_K):
        weights = jnp.where(q == pos[:, k:k + 1], gates[:, k:k + 1], weights)
    mixed = jnp.dot(weights.astype(jnp.bfloat16), ys_ref[...], preferred_element_type=jnp.float32)
    x_new = x_ref[...] + mod_ref[5:6, :] * mixed
    if final_norm:
        x_new = x_new * lax.rsqrt(jnp.mean(x_new * x_new, axis=-1, keepdims=True) + NORM_EPS) * gfin_ref[...]
    o_ref[...] = x_new


def _combine(ys_local, pos, gates, x, mod, g_final, layer, *, final_norm):
    return pl.pallas_call(
        functools.partial(_combine_kernel, final_norm=final_norm),
        grid=(N_TILES,),
        in_specs=[
            pl.BlockSpec((TILE_ROWS, D_MODEL), lambda i: (i, 0)),
            pl.BlockSpec((TM, TOP_K), lambda i: (i, 0)),
            pl.BlockSpec((TM, TOP_K), lambda i: (i, 0)),
            pl.BlockSpec((TM, D_MODEL), lambda i: (i, 0)),
            _mod_spec(layer),
            pl.BlockSpec((1, D_MODEL), lambda i: (0, 0)),
        ],
        out_specs=pl.BlockSpec((TM, D_MODEL), lambda i: (i, 0)),
        out_shape=jax.ShapeDtypeStruct((N_TOK, D_MODEL), jnp.float32),
        compiler_params=_cparams("arbitrary"),
        name="moe_combine",
    )(ys_local, pos, gates, x, mod, g_final.reshape(1, D_MODEL))


def kernel(x, c, w_mod, b_mod, norm_mix_g, norm_ffn_g, attn_w_qkv, attn_b_qkv, attn_w_o, attn_b_o,
           attn_sinks, rel_bias_table, conv_w_in, conv_b_in, conv_w_dw, conv_b_dw, conv_ln_g, conv_ln_b,
           conv_w_out, conv_b_out, router_w, router_b, moe_w_in, moe_b_in, moe_w_out, moe_b_out,
           final_norm_g):
    bf16 = jnp.bfloat16
    mod = _modulation(c, w_mod, b_mod)
    bias = _rel_bias(rel_bias_table)
    xf = x.reshape(N_TOK, D_MODEL)
    for i in range(DEPTH):
        j = i // 2
        if i % 2 == 0:
            qkv = _in_proj(xf, mod, norm_mix_g[i], attn_w_qkv[j].astype(bf16), attn_b_qkv[j], i,
                           glu=False, out_dtype=bf16)
            attn = _attention(qkv, bias, attn_sinks, j)
            xf, xs_local, pos, gates, units = _attn_out(attn, attn_w_o[j].astype(bf16), attn_b_o[j], xf, mod,
                                                        norm_ffn_g[i], router_w[i], router_b[i], i)
        else:
            u = _in_proj(xf, mod, norm_mix_g[i], conv_w_in[j].astype(bf16), conv_b_in[j], i,
                         glu=True, out_dtype=jnp.float32)
            xf, xs_local, pos, gates, units = _conv_out(u, conv_w_dw[j], conv_b_dw[j], conv_ln_g[j], conv_ln_b[j],
                                                        conv_w_out[j].astype(bf16), conv_b_out[j], xf, mod,
                                                        norm_ffn_g[i], router_w[i], router_b[i], i)
        route = _routing(units)
        ys_local = _moe_experts(xs_local, route, moe_w_in, moe_b_in, moe_w_out, moe_b_out, i)
        xf = _combine(ys_local, pos, gates, xf, mod, final_norm_g, i, final_norm=(i == DEPTH - 1))
    return xf.reshape(BATCH, SEQ, D_MODEL)
```

```python
import functools
import math

import jax
import jax.numpy as jnp
from jax import lax
from jax.experimental import pallas as pl
from jax.experimental.pallas import tpu as pltpu

D_MODEL = 1024
BATCH = 8
SEQ = 2048
DEPTH = 4
N_TOK = BATCH * SEQ
HEAD_DIM = 64
N_Q_HEADS = 16
N_KV_HEADS = 4
GQA_GROUP = 4
WINDOW = 128
Q_DIM = 1024
KV_DIM = 256
QKV_DIM = 1536
NUM_BUCKETS = 32
MAX_EXACT = 16
MAX_DISTANCE = 128
CONV_WIDTH = 31
N_EXPERTS = 32
TOP_K = 4
D_EXPERT = 1024
SWIGLU_LIMIT = 7.0
SWIGLU_ALPHA = 1.702
NORM_EPS = 1e-5
NEG_INF = -1e30

LANES = 128
SUBLANES = 8
ROW_TILES = D_MODEL // LANES
N_SLOT = N_TOK * TOP_K

TM = 512
TM_ATTN = WINDOW
N_TILES = N_TOK // TM
SEG = 16
TILE_ROWS = -(-(TM * TOP_K + N_EXPERTS * (SEG - 1)) // 256) * 256
TILE_UNITS = TILE_ROWS // SEG
BM = 256
BLK_UNITS = BM // SEG
SLACK_UNITS = TILE_UNITS - TM * TOP_K // SEG


def _bits_down(n):
    return tuple(1 << i for i in reversed(range(n.bit_length())))
N_BLKP = N_TILES * TILE_ROWS // BM + N_EXPERTS
LOCAL_ROWS = N_TILES * TILE_ROWS
CONV_HALO = 32
CONV_ROWS = 64

VMEM_LIMIT = 56 * 1024 * 1024


def _cparams(*sem):
    return pltpu.CompilerParams(dimension_semantics=sem, vmem_limit_bytes=VMEM_LIMIT)


def _mod_kernel(c_ref, w_ref, b_ref, o_ref):
    c = c_ref[...]
    c_act = (c * jax.nn.sigmoid(c)).astype(jnp.bfloat16)
    o_ref[...] = jnp.dot(c_act, w_ref[...].astype(jnp.bfloat16),
                         preferred_element_type=jnp.float32) + b_ref[...]


def _modulation(c, w_mod, b_mod):
    out = pl.pallas_call(
        _mod_kernel,
        grid=(DEPTH, 6),
        in_specs=[
            pl.BlockSpec((BATCH, D_MODEL), lambda i, j: (0, 0)),
            pl.BlockSpec((None, D_MODEL, D_MODEL), lambda i, j: (i, 0, j)),
            pl.BlockSpec((None, None, 1, D_MODEL), lambda i, j: (i, j, 0, 0)),
        ],
        out_specs=pl.BlockSpec((None, None, BATCH, D_MODEL), lambda i, j: (i, j, 0, 0)),
        out_shape=jax.ShapeDtypeStruct((DEPTH, 6, BATCH, D_MODEL), jnp.float32),
        compiler_params=_cparams("arbitrary", "arbitrary"),
        name="adaln_modulation",
    )(c, w_mod, b_mod.reshape(DEPTH, 6, 1, D_MODEL))
    return jnp.transpose(out, (0, 2, 1, 3))


def _mod_spec(layer):
    return pl.BlockSpec((None, None, 6, D_MODEL), lambda i: (layer, i // (SEQ // TM), 0, 0))


def _rms_modulate(x, g, scale, shift):
    y = x * lax.rsqrt(jnp.mean(x * x, axis=-1, keepdims=True) + NORM_EPS)
    return (y * g) * (1.0 + scale) + shift


def _in_proj_kernel(x_ref, mod_ref, g_ref, w_ref, b_ref, o_ref, *, glu):
    h = _rms_modulate(x_ref[...], g_ref[...], mod_ref[1:2, :], mod_ref[0:1, :])
    u = jnp.dot(h.astype(jnp.bfloat16), w_ref[...], preferred_element_type=jnp.float32) + b_ref[...]
    if glu:
        half = u.shape[-1] // 2
        u = u[:, :half] * jax.nn.sigmoid(u[:, half:])
    o_ref[...] = u.astype(o_ref.dtype)


def _in_proj(x, mod, g, w, b, layer, *, glu, out_dtype):
    n_in = w.shape[-1]
    n_out = n_in // 2 if glu else n_in
    return pl.pallas_call(
        functools.partial(_in_proj_kernel, glu=glu),
        grid=(N_TOK // TM,),
        in_specs=[
            pl.BlockSpec((TM, D_MODEL), lambda i: (i, 0)),
            _mod_spec(layer),
            pl.BlockSpec((1, D_MODEL), lambda i: (0, 0)),
            pl.BlockSpec((D_MODEL, n_in), lambda i: (0, 0)),
            pl.BlockSpec((1, n_in), lambda i: (0, 0)),
        ],
        out_specs=pl.BlockSpec((TM, n_out), lambda i: (i, 0)),
        out_shape=jax.ShapeDtypeStruct((N_TOK, n_out), out_dtype),
        compiler_params=_cparams("arbitrary"),
        name="mixer_in_proj_glu" if glu else "mixer_in_proj",
    )(x, mod, g.reshape(1, D_MODEL), w, b.reshape(1, n_in))


def _band_geometry():
    dist = jnp.arange(WINDOW)[:, None] + WINDOW - jnp.arange(2 * WINDOW)[None, :]
    n = jnp.maximum(dist, 0)
    nf = jnp.maximum(n, 1).astype(jnp.float32)
    large = MAX_EXACT + (jnp.log(nf / MAX_EXACT) / math.log(MAX_DISTANCE / MAX_EXACT)
                         * (NUM_BUCKETS - MAX_EXACT)).astype(jnp.int32)
    large = jnp.minimum(large, NUM_BUCKETS - 1)
    bucket = jnp.where(n < MAX_EXACT, n, large)
    return dist, bucket


def _bias_kernel(table_ref, bucket_ref, dist_ref, o_ref):
    h = pl.program_id(0)
    bucket = bucket_ref[...]
    dist = dist_ref[...]
    acc = jnp.zeros(bucket.shape, jnp.float32)
    for b in range(NUM_BUCKETS):
        acc = jnp.where(bucket == b, table_ref[b, h], acc)
    in_window = (dist >= 0) & (dist < WINDOW)
    o_ref[...] = jnp.where(in_window, acc, NEG_INF)


def _rel_bias(table):
    dist, bucket = _band_geometry()
    return pl.pallas_call(
        _bias_kernel,
        grid=(N_Q_HEADS,),
        in_specs=[
            pl.BlockSpec(memory_space=pltpu.SMEM),
            pl.BlockSpec((WINDOW, 2 * WINDOW), lambda h: (0, 0)),
            pl.BlockSpec((WINDOW, 2 * WINDOW), lambda h: (0, 0)),
        ],
        out_specs=pl.BlockSpec((None, WINDOW, 2 * WINDOW), lambda h: (h, 0, 0)),
        out_shape=jax.ShapeDtypeStruct((N_Q_HEADS, WINDOW, 2 * WINDOW), jnp.float32),
        compiler_params=_cparams("arbitrary"),
        name="rel_bias",
    )(table, bucket.astype(jnp.int32), dist.astype(jnp.int32))


def _attn_kernel(sink_ref, q_ref, kv_ref, kvp_ref, bias_ref, o_ref, *, layer):
    i = pl.program_id(0)
    first_block = (i % (SEQ // WINDOW)) == 0
    q = q_ref[...]
    kv = jnp.concatenate([kvp_ref[...], kv_ref[...]], axis=0)
    col = lax.broadcasted_iota(jnp.int32, (WINDOW, 2 * WINDOW), 1)
    prev_ok = jnp.logical_or(col >= WINDOW, jnp.logical_not(first_block))
    outs = []
    for hk in range(N_KV_HEADS):
        k = kv[:, hk * HEAD_DIM:(hk + 1) * HEAD_DIM]
        v = kv[:, KV_DIM + hk * HEAD_DIM:KV_DIM + (hk + 1) * HEAD_DIM]
        for g in range(GQA_GROUP):
            hq = hk * GQA_GROUP + g
            qh = q[:, hq * HEAD_DIM:(hq + 1) * HEAD_DIM] * (HEAD_DIM ** -0.5)
            s = lax.dot_general(qh, k, (((1,), (1,)), ((), ())), preferred_element_type=jnp.float32)
            s = jnp.where(prev_ok, s + bias_ref[hq], NEG_INF)
            sink = sink_ref[layer, hq]
            m = jnp.maximum(jnp.max(s, axis=-1, keepdims=True), sink)
            p = jnp.exp(s - m)
            denom = jnp.sum(p, axis=-1, keepdims=True) + jnp.exp(sink - m)
            o = jnp.dot(p.astype(jnp.bfloat16), v, preferred_element_type=jnp.float32)
            outs.append((o / denom).astype(jnp.bfloat16))
    o_ref[...] = jnp.concatenate(outs, axis=-1)


def _attention(qkv, bias, sinks, layer):
    kv_col = Q_DIM // (2 * KV_DIM)
    return pl.pallas_call(
        functools.partial(_attn_kernel, layer=layer),
        grid=(N_TOK // WINDOW,),
        in_specs=[
            pl.BlockSpec(memory_space=pltpu.SMEM),
            pl.BlockSpec((WINDOW, Q_DIM), lambda i: (i, 0)),
            pl.BlockSpec((WINDOW, 2 * KV_DIM), lambda i: (i, kv_col)),
            pl.BlockSpec((WINDOW, 2 * KV_DIM), lambda i: (jnp.maximum(i - 1, 0), kv_col)),
            pl.BlockSpec((N_Q_HEADS, WINDOW, 2 * WINDOW), lambda i: (0, 0, 0)),
        ],
        out_specs=pl.BlockSpec((WINDOW, Q_DIM), lambda i: (i, 0)),
        out_shape=jax.ShapeDtypeStruct((N_TOK, Q_DIM), jnp.bfloat16),
        compiler_params=_cparams("arbitrary"),
        name="swa_attention",
    )(sinks, qkv, qkv, qkv, bias)


def _split_bf16(a):
    hi = a.astype(jnp.bfloat16)
    lo = (a - hi.astype(jnp.float32)).astype(jnp.bfloat16)
    return hi, lo


def _residual_ffn_router(m, x, mod, g_ffn, wr_hi, wr_lo, b_r, outs):
    xo_ref, xs_ref, pos_ref, gate_ref, units_ref = outs
    x_new = x + mod[2:3, :] * m
    xo_ref[...] = x_new
    h = _rms_modulate(x_new, g_ffn, mod[4:5, :], mod[3:4, :])
    rows = h.shape[0]
    h_hi, h_lo = _split_bf16(h)
    logits = (jnp.dot(h_hi, wr_hi, preferred_element_type=jnp.float32)
              + jnp.dot(h_hi, wr_lo, preferred_element_type=jnp.float32)
              + jnp.dot(h_lo, wr_hi, preferred_element_type=jnp.float32)) + b_r
    lane = lax.broadcasted_iota(jnp.int32, logits.shape, 1)
    vals, ids = [], []
    for _ in range(TOP_K):
        best = jnp.max(logits, axis=-1, keepdims=True)
        arg = jnp.min(jnp.where(logits == best, lane, N_EXPERTS), axis=-1, keepdims=True)
        vals.append(best)
        ids.append(arg)
        logits = jnp.where(lane == arg, -jnp.inf, logits)
    exps = [jnp.exp(v - vals[0]) for v in vals]
    total = exps[0] + exps[1] + exps[2] + exps[3]
    gate_ref[...] = jnp.concatenate([e / total for e in exps], axis=-1)

    member = jnp.zeros(logits.shape, jnp.float32)
    for arg in ids:
        member = member + (lane == arg).astype(jnp.float32)
    units = jnp.floor((jnp.sum(member, axis=0, keepdims=True) + (SEG - 1)) * (1.0 / SEG))
    units_ref[...] = units
    lower_experts = (lax.broadcasted_iota(jnp.int32, (N_EXPERTS, N_EXPERTS), 0)
                     < lax.broadcasted_iota(jnp.int32, (N_EXPERTS, N_EXPERTS), 1)).astype(jnp.bfloat16)
    seg_start = SEG * jnp.dot(jnp.broadcast_to(units, (SUBLANES, N_EXPERTS)).astype(jnp.bfloat16), lower_experts,
                              preferred_element_type=jnp.float32)[0:1, :]
    earlier = (lax.broadcasted_iota(jnp.int32, (rows, rows), 0)
               > lax.broadcasted_iota(jnp.int32, (rows, rows), 1)).astype(jnp.bfloat16)
    place = seg_start + jnp.dot(earlier, member.astype(jnp.bfloat16), preferred_element_type=jnp.float32)
    pos = [jnp.sum(jnp.where(lane == arg, place, 0.0), axis=-1, keepdims=True).astype(jnp.int32) for arg in ids]
    pos_ref[...] = jnp.concatenate(pos, axis=-1)

    q = lax.broadcasted_iota(jnp.int32, (rows, TILE_ROWS), 1)
    hit = (q == pos[0]) | (q == pos[1]) | (q == pos[2]) | (q == pos[3])
    onehot = jnp.where(hit, 1.0, 0.0).astype(jnp.bfloat16)
    h_bf = h.astype(jnp.bfloat16)
    half = D_MODEL // 2
    for c in range(2):
        xs = lax.dot_general(onehot, h_bf[:, c * half:(c + 1) * half], (((0,), (0,)), ((), ())),
                             preferred_element_type=jnp.float32)
        xs_ref[:, c * half:(c + 1) * half] = xs.astype(jnp.bfloat16)


N_EPILOGUE_OUTS = 5


def _attn_out_kernel(a_ref, wo_ref, bo_ref, x_ref, mod_ref, g_ref, wrh_ref, wrl_ref, br_ref, *outs):
    m = jnp.dot(a_ref[...], wo_ref[...], preferred_element_type=jnp.float32) + bo_ref[...]
    _residual_ffn_router(m, x_ref[...], mod_ref[...], g_ref[...], wrh_ref[...], wrl_ref[...], br_ref[...], outs)


def _epilogue_out_specs():
    return [
        pl.BlockSpec((TM, D_MODEL), lambda i: (i, 0)),
        pl.BlockSpec((TILE_ROWS, D_MODEL), lambda i: (i, 0)),
        pl.BlockSpec((TM, TOP_K), lambda i: (i, 0)),
        pl.BlockSpec((TM, TOP_K), lambda i: (i, 0)),
        pl.BlockSpec((None, 1, N_EXPERTS), lambda i: (i, 0, 0)),
    ]


def _epilogue_out_shapes():
    return [
        jax.ShapeDtypeStruct((N_TOK, D_MODEL), jnp.float32),
        jax.ShapeDtypeStruct((LOCAL_ROWS, D_MODEL), jnp.bfloat16),
        jax.ShapeDtypeStruct((N_TOK, TOP_K), jnp.int32),
        jax.ShapeDtypeStruct((N_TOK, TOP_K), jnp.float32),
        jax.ShapeDtypeStruct((N_TILES, 1, N_EXPERTS), jnp.float32),
    ]


def _router_operands(w_r, b_r):
    hi, lo = _split_bf16(w_r)
    return hi, lo, b_r.reshape(1, N_EXPERTS)


def _router_specs():
    return [
        pl.BlockSpec((D_MODEL, N_EXPERTS), lambda i: (0, 0)),
        pl.BlockSpec((D_MODEL, N_EXPERTS), lambda i: (0, 0)),
        pl.BlockSpec((1, N_EXPERTS), lambda i: (0, 0)),
    ]


def _attn_out(attn, w_o, b_o, x, mod, g_ffn, w_r, b_r, layer):
    return pl.pallas_call(
        _attn_out_kernel,
        grid=(N_TOK // TM,),
        in_specs=[
            pl.BlockSpec((TM, Q_DIM), lambda i: (i, 0)),
            pl.BlockSpec((Q_DIM, D_MODEL), lambda i: (0, 0)),
            pl.BlockSpec((1, D_MODEL), lambda i: (0, 0)),
            pl.BlockSpec((TM, D_MODEL), lambda i: (i, 0)),
            _mod_spec(layer),
            pl.BlockSpec((1, D_MODEL), lambda i: (0, 0)),
        ] + _router_specs(),
        out_specs=_epilogue_out_specs(),
        out_shape=_epilogue_out_shapes(),
        compiler_params=_cparams("arbitrary"),
        name="attn_out_router",
    )(attn, w_o, b_o.reshape(1, D_MODEL), x, mod, g_ffn.reshape(1, D_MODEL), *_router_operands(w_r, b_r))


def _conv_out_kernel(u_ref, up_ref, wdw_ref, bdw_ref, lng_ref, lnb_ref, wo_ref, bo_ref,
                     x_ref, mod_ref, g_ref, wrh_ref, wrl_ref, br_ref, *rest):
    outs = rest[:N_EPILOGUE_OUTS]
    ext_ref, y_ref = rest[N_EPILOGUE_OUTS:]
    i = pl.program_id(0)
    first_tile = (i % (SEQ // TM)) == 0
    ext_ref[0:CONV_HALO, :] = jnp.where(first_tile, 0.0, up_ref[...])
    ext_ref[CONV_HALO:, :] = u_ref[...]
    lead = CONV_HALO - (CONV_WIDTH - 1)
    span = CONV_ROWS + CONV_HALO

    def chunk(r, carry):
        r0 = pl.multiple_of(r * CONV_ROWS, CONV_ROWS)
        for cb in range(ROW_TILES):
            cols = slice(cb * LANES, (cb + 1) * LANES)
            e = ext_ref[pl.ds(r0, span), cols]
            acc = jnp.zeros((CONV_ROWS, LANES), jnp.float32)
            for sub in range(SUBLANES):
                shifted = e if sub == 0 else pltpu.roll(e, span - sub, 0)
                for a in range(span // SUBLANES):
                    tap = a * SUBLANES + sub - lead
                    if 0 <= tap < CONV_WIDTH:
                        acc = acc + wdw_ref[tap:tap + 1, cols] * shifted[a * SUBLANES:a * SUBLANES + CONV_ROWS]
            y_ref[pl.ds(r0, CONV_ROWS), cols] = acc
        return carry

    lax.fori_loop(0, TM // CONV_ROWS, chunk, 0)
    y = y_ref[...] + bdw_ref[...]
    mu = jnp.mean(y, axis=-1, keepdims=True)
    var = jnp.mean(jnp.square(y - mu), axis=-1, keepdims=True)
    z = (y - mu) * lax.rsqrt(var + NORM_EPS) * lng_ref[...] + lnb_ref[...]
    z = z * jax.nn.sigmoid(z)
    m = jnp.dot(z.astype(jnp.bfloat16), wo_ref[...], preferred_element_type=jnp.float32) + bo_ref[...]
    _residual_ffn_router(m, x_ref[...], mod_ref[...], g_ref[...], wrh_ref[...], wrl_ref[...], br_ref[...],
                         outs)


def _conv_out(u, w_dw, b_dw, ln_g, ln_b, w_o, b_o, x, mod, g_ffn, w_r, b_r, layer):
    halo_blocks = TM // CONV_HALO
    row = lambda a: a.reshape(1, D_MODEL)
    return pl.pallas_call(
        _conv_out_kernel,
        grid=(N_TOK // TM,),
        in_specs=[
            pl.BlockSpec((TM, D_MODEL), lambda i: (i, 0)),
            pl.BlockSpec((CONV_HALO, D_MODEL), lambda i: (jnp.maximum(i * halo_blocks - 1, 0), 0)),
            pl.BlockSpec((CONV_WIDTH, D_MODEL), lambda i: (0, 0)),
            pl.BlockSpec((1, D_MODEL), lambda i: (0, 0)),
            pl.BlockSpec((1, D_MODEL), lambda i: (0, 0)),
            pl.BlockSpec((1, D_MODEL), lambda i: (0, 0)),
            pl.BlockSpec((D_MODEL, D_MODEL), lambda i: (0, 0)),
            pl.BlockSpec((1, D_MODEL), lambda i: (0, 0)),
            pl.BlockSpec((TM, D_MODEL), lambda i: (i, 0)),
            _mod_spec(layer),
            pl.BlockSpec((1, D_MODEL), lambda i: (0, 0)),
        ] + _router_specs(),
        out_specs=_epilogue_out_specs(),
        out_shape=_epilogue_out_shapes(),
        scratch_shapes=[
            pltpu.VMEM((TM + CONV_HALO, D_MODEL), jnp.float32),
            pltpu.VMEM((TM, D_MODEL), jnp.float32),
        ],
        compiler_params=_cparams("arbitrary"),
        name="conv_out_router",
    )(u, u, w_dw, row(b_dw), row(ln_g), row(ln_b), w_o, row(b_o), x, mod, row(g_ffn),
      *_router_operands(w_r, b_r))


def _routing(units):
    per_tile = units.reshape(N_TILES, N_EXPERTS).astype(jnp.int32)
    seg_off = jnp.cumsum(per_tile, axis=1) - per_tile
    used = jnp.sum(per_tile, axis=1)
    zero = jnp.zeros((N_EXPERTS, 1), jnp.int32)
    cum = jnp.concatenate([zero, jnp.cumsum(per_tile.T, axis=1)], axis=1)
    total = cum[:, -1]
    n_blocks = (total + BLK_UNITS - 1) // BLK_UNITS
    blk_end = jnp.cumsum(n_blocks)
    blk_start = blk_end - n_blocks
    n_live = blk_end[-1]
    blk = jnp.arange(N_BLKP, dtype=jnp.int32)
    blk_exp = jnp.minimum(jnp.sum(blk[:, None] >= blk_end[None, :], axis=1), N_EXPERTS - 1)
    blk_exp = jnp.where(blk < n_live, blk_exp, blk_exp[n_live - 1]).astype(jnp.int32)
    blk_at = (blk - blk_start[blk_exp]) * BLK_UNITS
    at = blk_at[:, None] + jnp.arange(BLK_UNITS, dtype=jnp.int32)[None, :]
    exp = jnp.broadcast_to(blk_exp[:, None], at.shape)
    tile = jnp.minimum(jnp.sum(cum[blk_exp][:, None, 1:] <= at[:, :, None], axis=-1), N_TILES - 1)
    local = tile * TILE_UNITS + seg_off[tile, exp] + at - cum[exp, tile]
    unit_map = jnp.where(at < total[exp], local, -1)
    experts = jnp.arange(N_EXPERTS, dtype=jnp.int32)
    owners = lax.cummin(jnp.where(n_blocks > 0, experts, N_EXPERTS), reverse=True)
    next_exp = jnp.concatenate([owners[1:], jnp.full((1,), N_EXPERTS, jnp.int32)])
    next_exp = jnp.where(next_exp < N_EXPERTS, next_exp, -1)
    i32 = lambda a: a.astype(jnp.int32)
    return dict(blk_exp=blk_exp, unit_map=i32(unit_map.reshape(-1)), n_live=i32(n_live.reshape(1)), used=i32(used),
                next_exp=i32(next_exp))


def _unit_rows(unit, n_units):
    return pl.ds(pl.multiple_of(unit * SEG, SEG), n_units * SEG)


def _moe_kernel(exp_ref, map_ref, live_ref, used_ref, next_ref,
                xs_hbm, win_hbm, bin_ref, wout_hbm, bout_ref, ys_hbm,
                xbuf0, xbuf1, ybuf0, ybuf1, zbuf, win_f32, wout_f32, win_bf, wout_bf, gsem, ssem, zsem, wsem,
                *, layer):
    b = pl.program_id(0)
    last = pl.num_programs(0) - 1
    xbufs = (xbuf0, xbuf1)
    ybufs = (ybuf0, ybuf1)

    def zero_tail(tile, act):
        used = used_ref[tile]
        n = TILE_UNITS - used
        for size in _bits_down(SLACK_UNITS):
            @pl.when((n & size) != 0)
            def _():
                start = n & ~(2 * size - 1)
                act(pltpu.make_async_copy(zbuf.at[_unit_rows(0, size), :],
                                          ys_hbm.at[_unit_rows(tile * TILE_UNITS + used + start, size), :], zsem))

    def gather(block, buf):
        for u in range(BLK_UNITS):
            local = jnp.maximum(map_ref[block * BLK_UNITS + u], 0)
            pltpu.make_async_copy(xs_hbm.at[_unit_rows(local, 1), :], xbufs[buf].at[_unit_rows(u, 1), :],
                                  gsem.at[buf]).start()

    def scatter(block, buf):
        trash = N_TILES * TILE_UNITS + buf * BLK_UNITS
        for u in range(BLK_UNITS):
            local = map_ref[block * BLK_UNITS + u]
            local = jnp.where(local >= 0, local, trash + u)
            pltpu.make_async_copy(ybufs[buf].at[_unit_rows(u, 1), :], ys_hbm.at[_unit_rows(local, 1), :],
                                  ssem.at[buf]).start()

    def wait_rows(buffer, sem):
        pltpu.make_async_copy(buffer, buffer, sem).wait()

    def weight_copies(e):
        return (pltpu.make_async_copy(win_hbm.at[layer, e], win_f32, wsem.at[0]),
                pltpu.make_async_copy(wout_hbm.at[layer, e], wout_f32, wsem.at[1]))

    @pl.when(b == 0)
    def _():
        for copy in weight_copies(exp_ref[0]):
            copy.start()

    new_expert = jnp.logical_or(b == 0, exp_ref[b] != exp_ref[jnp.maximum(b - 1, 0)])

    @pl.when(new_expert)
    def _():
        e = exp_ref[b]
        for copy in weight_copies(e):
            copy.wait()
        win_bf[...] = win_f32[...].astype(jnp.bfloat16)
        wout_bf[...] = wout_f32[...].astype(jnp.bfloat16)
        upcoming = next_ref[e]

        @pl.when(upcoming >= 0)
        def _():
            for copy in weight_copies(upcoming):
                copy.start()

    @pl.when(b == 0)
    def _():
        gather(0, 0)
        ybuf0[...] = jnp.zeros_like(ybuf0)
        ybuf1[...] = jnp.zeros_like(ybuf1)
        zbuf[...] = jnp.zeros_like(zbuf)
        pltpu.make_async_copy(ybuf0, ys_hbm.at[_unit_rows(N_TILES * TILE_UNITS, BLK_UNITS), :], ssem.at[0]).start()
        trash1 = pltpu.make_async_copy(zbuf.at[_unit_rows(0, BLK_UNITS), :],
                                       ys_hbm.at[_unit_rows(N_TILES * TILE_UNITS + BLK_UNITS, BLK_UNITS), :], zsem)
        trash1.start()
        trash1.wait()

    @pl.when(jnp.logical_and(b >= 1, b <= N_TILES))
    def _():
        zero_tail(b - 1, lambda copy: copy.wait())

    @pl.when(b < N_TILES)
    def _():
        zero_tail(b, lambda copy: copy.start())

    def step(cur):
        nxt = 1 - cur
        wait_rows(xbufs[cur], gsem.at[cur])
        wait_rows(ybufs[cur], ssem.at[cur])
        gather(jnp.minimum(b + 1, last), nxt)
        scatter(jnp.maximum(b - 1, 0), nxt)

        @pl.when(b < live_ref[0])
        def _():
            u = jnp.dot(xbufs[cur][...], win_bf[...], preferred_element_type=jnp.float32) + bin_ref[...]
            glu = jnp.minimum(u[:, :D_EXPERT], SWIGLU_LIMIT)
            lin = jnp.clip(u[:, D_EXPERT:], -SWIGLU_LIMIT, SWIGLU_LIMIT)
            act = glu * jax.nn.sigmoid(SWIGLU_ALPHA * glu) * (lin + 1.0)
            y = jnp.dot(act.astype(jnp.bfloat16), wout_bf[...], preferred_element_type=jnp.float32) + bout_ref[...]
            ybufs[cur][...] = y.astype(jnp.bfloat16)

        @pl.when(b == last)
        def _():
            scatter(b, cur)
            wait_rows(ybufs[nxt], ssem.at[nxt])
            wait_rows(ybufs[cur], ssem.at[cur])
            wait_rows(xbufs[nxt], gsem.at[nxt])

    for parity in range(2):
        pl.when(b % 2 == parity)(functools.partial(step, parity))


def _moe_experts(xs_local, route, w_in, b_in, w_out, b_out, layer):
    by_expert = lambda b, ex, *_: (layer, ex[b], 0, 0)
    tables = [route[k] for k in ("blk_exp", "unit_map", "n_live", "used", "next_exp")]
    grid_spec = pltpu.PrefetchScalarGridSpec(
        num_scalar_prefetch=len(tables),
        grid=(N_BLKP,),
        in_specs=[
            pl.BlockSpec(memory_space=pl.ANY),
            pl.BlockSpec(memory_space=pl.ANY),
            pl.BlockSpec((None, None, 1, 2 * D_EXPERT), by_expert),
            pl.BlockSpec(memory_space=pl.ANY),
            pl.BlockSpec((None, None, 1, D_MODEL), by_expert),
        ],
        out_specs=pl.BlockSpec(memory_space=pl.ANY),
        scratch_shapes=[
            pltpu.VMEM((BM, D_MODEL), jnp.bfloat16),
            pltpu.VMEM((BM, D_MODEL), jnp.bfloat16),
            pltpu.VMEM((BM, D_MODEL), jnp.bfloat16),
            pltpu.VMEM((BM, D_MODEL), jnp.bfloat16),
            pltpu.VMEM((SLACK_UNITS * SEG, D_MODEL), jnp.bfloat16),
            pltpu.VMEM((D_MODEL, 2 * D_EXPERT), jnp.float32),
            pltpu.VMEM((D_EXPERT, D_MODEL), jnp.float32),
            pltpu.VMEM((D_MODEL, 2 * D_EXPERT), jnp.bfloat16),
            pltpu.VMEM((D_EXPERT, D_MODEL), jnp.bfloat16),
            pltpu.SemaphoreType.DMA((2,)),
            pltpu.SemaphoreType.DMA((2,)),
            pltpu.SemaphoreType.DMA(()),
            pltpu.SemaphoreType.DMA((2,)),
        ],
    )
    return pl.pallas_call(
        functools.partial(_moe_kernel, layer=layer),
        grid_spec=grid_spec,
        out_shape=jax.ShapeDtypeStruct((LOCAL_ROWS + 2 * BM, D_MODEL), jnp.bfloat16),
        compiler_params=_cparams("arbitrary"),
        name="moe_experts",
    )(*tables, xs_local, w_in, b_in.reshape(DEPTH, N_EXPERTS, 1, 2 * D_EXPERT), w_out,
      b_out.reshape(DEPTH, N_EXPERTS, 1, D_MODEL))


def _combine_kernel(ys_ref, pos_ref, gate_ref, x_ref, mod_ref, gfin_ref, o_ref, *, final_norm):
    pos = pos_ref[...]
    gates = gate_ref[...]
    q = lax.broadcasted_iota(jnp.int32, (TM, TILE_ROWS), 1)
    weights = jnp.zeros((TM, TILE_ROWS), jnp.float32)
    for k in range(TOP_K):
        weights = jnp.where(q == pos[:, k:k + 1], gates[:, k:k + 1], weights)
    mixed = jnp.dot(weights.astype(jnp.bfloat16), ys_ref[...], preferred_element_type=jnp.float32)
    x_new = x_ref[...] + mod_ref[5:6, :] * mixed
    if final_norm:
        x_new = x_new * lax.rsqrt(jnp.mean(x_new * x_new, axis=-1, keepdims=True) + NORM_EPS) * gfin_ref[...]
    o_ref[...] = x_new


def _combine(ys_local, pos, gates, x, mod, g_final, layer, *, final_norm):
    return pl.pallas_call(
        functools.partial(_combine_kernel, final_norm=final_norm),
        grid=(N_TILES,),
        in_specs=[
            pl.BlockSpec((TILE_ROWS, D_MODEL), lambda i: (i, 0)),
            pl.BlockSpec((TM, TOP_K), lambda i: (i, 0)),
            pl.BlockSpec((TM, TOP_K), lambda i: (i, 0)),
            pl.BlockSpec((TM, D_MODEL), lambda i: (i, 0)),
            _mod_spec(layer),
            pl.BlockSpec((1, D_MODEL), lambda i: (0, 0)),
        ],
        out_specs=pl.BlockSpec((TM, D_MODEL), lambda i: (i, 0)),
        out_shape=jax.ShapeDtypeStruct((N_TOK, D_MODEL), jnp.float32),
        compiler_params=_cparams("arbitrary"),
        name="moe_combine",
    )(ys_local, pos, gates, x, mod, g_final.reshape(1, D_MODEL))


def kernel(x, c, w_mod, b_mod, norm_mix_g, norm_ffn_g, attn_w_qkv, attn_b_qkv, attn_w_o, attn_b_o,
           attn_sinks, rel_bias_table, conv_w_in, conv_b_in, conv_w_dw, conv_b_dw, conv_ln_g, conv_ln_b,
           conv_w_out, conv_b_out, router_w, router_b, moe_w_in, moe_b_in, moe_w_out, moe_b_out,
           final_norm_g):
    bf16 = jnp.bfloat16
    mod = _modulation(c, w_mod, b_mod)
    bias = _rel_bias(rel_bias_table)
    xf = x.reshape(N_TOK, D_MODEL)
    for i in range(DEPTH):
        j = i // 2
        if i % 2 == 0:
            qkv = _in_proj(xf, mod, norm_mix_g[i], attn_w_qkv[j].astype(bf16), attn_b_qkv[j], i,
                           glu=False, out_dtype=bf16)
            attn = _attention(qkv, bias, attn_sinks, j)
            xf, xs_local, pos, gates, units = _attn_out(attn, attn_w_o[j].astype(bf16), attn_b_o[j], xf, mod,
                                                        norm_ffn_g[i], router_w[i], router_b[i], i)
        else:
            u = _in_proj(xf, mod, norm_mix_g[i], conv_w_in[j].astype(bf16), conv_b_in[j], i,
                         glu=True, out_dtype=jnp.float32)
            xf, xs_local, pos, gates, units = _conv_out(u, conv_w_dw[j], conv_b_dw[j], conv_ln_g[j], conv_ln_b[j],
                                                        conv_w_out[j].astype(bf16), conv_b_out[j], xf, mod,
                                                        norm_ffn_g[i], router_w[i], router_b[i], i)
        route = _routing(units)
        ys_local = _moe_experts(xs_local, route, moe_w_in, moe_b_in, moe_w_out, moe_b_out, i)
        xf = _combine(ys_local, pos, gates, xf, mod, final_norm_g, i, final_norm=(i == DEPTH - 1))
    return xf.reshape(BATCH, SEQ, D_MODEL)
```

```python
import functools
import math

import jax
import jax.numpy as jnp
from jax import lax
from jax.experimental import pallas as pl
from jax.experimental.pallas import tpu as pltpu

D_MODEL = 1024
BATCH = 8
SEQ = 2048
DEPTH = 4
N_TOK = BATCH * SEQ
HEAD_DIM = 64
N_Q_HEADS = 16
N_KV_HEADS = 4
GQA_GROUP = 4
WINDOW = 128
Q_DIM = 1024
KV_DIM = 256
QKV_DIM = 1536
NUM_BUCKETS = 32
MAX_EXACT = 16
MAX_DISTANCE = 128
CONV_WIDTH = 31
N_EXPERTS = 32
TOP_K = 4
D_EXPERT = 1024
SWIGLU_LIMIT = 7.0
SWIGLU_ALPHA = 1.702
NORM_EPS = 1e-5
NEG_INF = -1e30

LANES = 128
SUBLANES = 8
ROW_TILES = D_MODEL // LANES
N_SLOT = N_TOK * TOP_K

TM = 512
TM_ATTN = WINDOW
N_TILES = N_TOK // TM
SEG = 16
TILE_ROWS = -(-(TM * TOP_K + N_EXPERTS * (SEG - 1)) // 256) * 256
TILE_UNITS = TILE_ROWS // SEG
BM = 512
BLK_UNITS = BM // SEG
SLACK_UNITS = TILE_UNITS - TM * TOP_K // SEG


def _bits_down(n):
    return tuple(1 << i for i in reversed(range(n.bit_length())))
N_BLKP = N_TILES * TILE_ROWS // BM + N_EXPERTS
LOCAL_ROWS = N_TILES * TILE_ROWS
CONV_HALO = 32
CONV_ROWS = 64

VMEM_LIMIT = 56 * 1024 * 1024


def _cparams(*sem):
    return pltpu.CompilerParams(dimension_semantics=sem, vmem_limit_bytes=VMEM_LIMIT)


def _mod_kernel(c_ref, w_ref, b_ref, o_ref):
    c = c_ref[...]
    c_act = (c * jax.nn.sigmoid(c)).astype(jnp.bfloat16)
    o_ref[...] = jnp.dot(c_act, w_ref[...].astype(jnp.bfloat16),
                         preferred_element_type=jnp.float32) + b_ref[...]


def _modulation(c, w_mod, b_mod):
    out = pl.pallas_call(
        _mod_kernel,
        grid=(DEPTH, 6),
        in_specs=[
            pl.BlockSpec((BATCH, D_MODEL), lambda i, j: (0, 0)),
            pl.BlockSpec((None, D_MODEL, D_MODEL), lambda i, j: (i, 0, j)),
            pl.BlockSpec((None, None, 1, D_MODEL), lambda i, j: (i, j, 0, 0)),
        ],
        out_specs=pl.BlockSpec((None, None, BATCH, D_MODEL), lambda i, j: (i, j, 0, 0)),
        out_shape=jax.ShapeDtypeStruct((DEPTH, 6, BATCH, D_MODEL), jnp.float32),
        compiler_params=_cparams("arbitrary", "arbitrary"),
        name="adaln_modulation",
    )(c, w_mod, b_mod.reshape(DEPTH, 6, 1, D_MODEL))
    return jnp.transpose(out, (0, 2, 1, 3))


def _mod_spec(layer):
    return pl.BlockSpec((None, None, 6, D_MODEL), lambda i: (layer, i // (SEQ // TM), 0, 0))


def _rms_modulate(x, g, scale, shift):
    y = x * lax.rsqrt(jnp.mean(x * x, axis=-1, keepdims=True) + NORM_EPS)
    return (y * g) * (1.0 + scale) + shift


def _in_proj_kernel(x_ref, mod_ref, g_ref, w_ref, b_ref, o_ref, *, glu):
    h = _rms_modulate(x_ref[...], g_ref[...], mod_ref[1:2, :], mod_ref[0:1, :])
    u = jnp.dot(h.astype(jnp.bfloat16), w_ref[...], preferred_element_type=jnp.float32) + b_ref[...]
    if glu:
        half = u.shape[-1] // 2
        u = u[:, :half] * jax.nn.sigmoid(u[:, half:])
    o_ref[...] = u.astype(o_ref.dtype)


def _in_proj(x, mod, g, w, b, layer, *, glu, out_dtype):
    n_in = w.shape[-1]
    n_out = n_in // 2 if glu else n_in
    return pl.pallas_call(
        functools.partial(_in_proj_kernel, glu=glu),
        grid=(N_TOK // TM,),
        in_specs=[
            pl.BlockSpec((TM, D_MODEL), lambda i: (i, 0)),
            _mod_spec(layer),
            pl.BlockSpec((1, D_MODEL), lambda i: (0, 0)),
            pl.BlockSpec((D_MODEL, n_in), lambda i: (0, 0)),
            pl.BlockSpec((1, n_in), lambda i: (0, 0)),
        ],
        out_specs=pl.BlockSpec((TM, n_out), lambda i: (i, 0)),
        out_shape=jax.ShapeDtypeStruct((N_TOK, n_out), out_dtype),
        compiler_params=_cparams("arbitrary"),
        name="mixer_in_proj_glu" if glu else "mixer_in_proj",
    )(x, mod, g.reshape(1, D_MODEL), w, b.reshape(1, n_in))


def _band_geometry():
    dist = jnp.arange(WINDOW)[:, None] + WINDOW - jnp.arange(2 * WINDOW)[None, :]
    n = jnp.maximum(dist, 0)
    nf = jnp.maximum(n, 1).astype(jnp.float32)
    large = MAX_EXACT + (jnp.log(nf / MAX_EXACT) / math.log(MAX_DISTANCE / MAX_EXACT)
                         * (NUM_BUCKETS - MAX_EXACT)).astype(jnp.int32)
    large = jnp.minimum(large, NUM_BUCKETS - 1)
    bucket = jnp.where(n < MAX_EXACT, n, large)
    return dist, bucket


def _bias_kernel(table_ref, bucket_ref, dist_ref, o_ref):
    h = pl.program_id(0)
    bucket = bucket_ref[...]
    dist = dist_ref[...]
    acc = jnp.zeros(bucket.shape, jnp.float32)
    for b in range(NUM_BUCKETS):
        acc = jnp.where(bucket == b, table_ref[b, h], acc)
    in_window = (dist >= 0) & (dist < WINDOW)
    o_ref[...] = jnp.where(in_window, acc, NEG_INF)


def _rel_bias(table):
    dist, bucket = _band_geometry()
    return pl.pallas_call(
        _bias_kernel,
        grid=(N_Q_HEADS,),
        in_specs=[
            pl.BlockSpec(memory_space=pltpu.SMEM),
            pl.BlockSpec((WINDOW, 2 * WINDOW), lambda h: (0, 0)),
            pl.BlockSpec((WINDOW, 2 * WINDOW), lambda h: (0, 0)),
        ],
        out_specs=pl.BlockSpec((None, WINDOW, 2 * WINDOW), lambda h: (h, 0, 0)),
        out_shape=jax.ShapeDtypeStruct((N_Q_HEADS, WINDOW, 2 * WINDOW), jnp.float32),
        compiler_params=_cparams("arbitrary"),
        name="rel_bias",
    )(table, bucket.astype(jnp.int32), dist.astype(jnp.int32))


def _attn_kernel(sink_ref, q_ref, kv_ref, kvp_ref, bias_ref, o_ref, *, layer):
    i = pl.program_id(0)
    first_block = (i % (SEQ // WINDOW)) == 0
    q = q_ref[...]
    kv = jnp.concatenate([kvp_ref[...], kv_ref[...]], axis=0)
    col = lax.broadcasted_iota(jnp.int32, (WINDOW, 2 * WINDOW), 1)
    prev_ok = jnp.logical_or(col >= WINDOW, jnp.logical_not(first_block))
    outs = []
    for hk in range(N_KV_HEADS):
        k = kv[:, hk * HEAD_DIM:(hk + 1) * HEAD_DIM]
        v = kv[:, KV_DIM + hk * HEAD_DIM:KV_DIM + (hk + 1) * HEAD_DIM]
        for g in range(GQA_GROUP):
            hq = hk * GQA_GROUP + g
            qh = q[:, hq * HEAD_DIM:(hq + 1) * HEAD_DIM] * (HEAD_DIM ** -0.5)
            s = lax.dot_general(qh, k, (((1,), (1,)), ((), ())), preferred_element_type=jnp.float32)
            s = jnp.where(prev_ok, s + bias_ref[hq], NEG_INF)
            sink = sink_ref[layer, hq]
            m = jnp.maximum(jnp.max(s, axis=-1, keepdims=True), sink)
            p = jnp.exp(s - m)
            denom = jnp.sum(p, axis=-1, keepdims=True) + jnp.exp(sink - m)
            o = jnp.dot(p.astype(jnp.bfloat16), v, preferred_element_type=jnp.float32)
            outs.append((o / denom).astype(jnp.bfloat16))
    o_ref[...] = jnp.concatenate(outs, axis=-1)


def _attention(qkv, bias, sinks, layer):
    kv_col = Q_DIM // (2 * KV_DIM)
    return pl.pallas_call(
        functools.partial(_attn_kernel, layer=layer),
        grid=(N_TOK // WINDOW,),
        in_specs=[
            pl.BlockSpec(memory_space=pltpu.SMEM),
            pl.BlockSpec((WINDOW, Q_DIM), lambda i: (i, 0)),
            pl.BlockSpec((WINDOW, 2 * KV_DIM), lambda i: (i, kv_col)),
            pl.BlockSpec((WINDOW, 2 * KV_DIM), lambda i: (jnp.maximum(i - 1, 0), kv_col)),
            pl.BlockSpec((N_Q_HEADS, WINDOW, 2 * WINDOW), lambda i: (0, 0, 0)),
        ],
        out_specs=pl.BlockSpec((WINDOW, Q_DIM), lambda i: (i, 0)),
        out_shape=jax.ShapeDtypeStruct((N_TOK, Q_DIM), jnp.bfloat16),
        compiler_params=_cparams("arbitrary"),
        name="swa_attention",
    )(sinks, qkv, qkv, qkv, bias)


def _split_bf16(a):
    hi = a.astype(jnp.bfloat16)
    lo = (a - hi.astype(jnp.float32)).astype(jnp.bfloat16)
    return hi, lo


def _residual_ffn_router(m, x, mod, g_ffn, wr_hi, wr_lo, b_r, outs):
    xo_ref, xs_ref, pos_ref, gate_ref, units_ref = outs
    x_new = x + mod[2:3, :] * m
    xo_ref[...] = x_new
    h = _rms_modulate(x_new, g_ffn, mod[4:5, :], mod[3:4, :])
    rows = h.shape[0]
    h_hi, h_lo = _split_bf16(h)
    logits = (jnp.dot(h_hi, wr_hi, preferred_element_type=jnp.float32)
              + jnp.dot(h_hi, wr_lo, preferred_element_type=jnp.float32)
              + jnp.dot(h_lo, wr_hi, preferred_element_type=jnp.float32)) + b_r
    lane = lax.broadcasted_iota(jnp.int32, logits.shape, 1)
    vals, ids = [], []
    for _ in range(TOP_K):
        best = jnp.max(logits, axis=-1, keepdims=True)
        arg = jnp.min(jnp.where(logits == best, lane, N_EXPERTS), axis=-1, keepdims=True)
        vals.append(best)
        ids.append(arg)
        logits = jnp.where(lane == arg, -jnp.inf, logits)
    exps = [jnp.exp(v - vals[0]) for v in vals]
    total = exps[0] + exps[1] + exps[2] + exps[3]
    gate_ref[...] = jnp.concatenate([e / total for e in exps], axis=-1)

    member = jnp.zeros(logits.shape, jnp.float32)
    for arg in ids:
        member = member + (lane == arg).astype(jnp.float32)
    units = jnp.floor((jnp.sum(member, axis=0, keepdims=True) + (SEG - 1)) * (1.0 / SEG))
    units_ref[...] = units
    lower_experts = (lax.broadcasted_iota(jnp.int32, (N_EXPERTS, N_EXPERTS), 0)
                     < lax.broadcasted_iota(jnp.int32, (N_EXPERTS, N_EXPERTS), 1)).astype(jnp.bfloat16)
    seg_start = SEG * jnp.dot(jnp.broadcast_to(units, (SUBLANES, N_EXPERTS)).astype(jnp.bfloat16), lower_experts,
                              preferred_element_type=jnp.float32)[0:1, :]
    earlier = (lax.broadcasted_iota(jnp.int32, (rows, rows), 0)
               > lax.broadcasted_iota(jnp.int32, (rows, rows), 1)).astype(jnp.bfloat16)
    place = seg_start + jnp.dot(earlier, member.astype(jnp.bfloat16), preferred_element_type=jnp.float32)
    pos = [jnp.sum(jnp.where(lane == arg, place, 0.0), axis=-1, keepdims=True).astype(jnp.int32) for arg in ids]
    pos_ref[...] = jnp.concatenate(pos, axis=-1)

    q = lax.broadcasted_iota(jnp.int32, (rows, TILE_ROWS), 1)
    hit = (q == pos[0]) | (q == pos[1]) | (q == pos[2]) | (q == pos[3])
    onehot = jnp.where(hit, 1.0, 0.0).astype(jnp.bfloat16)
    h_bf = h.astype(jnp.bfloat16)
    half = D_MODEL // 2
    for c in range(2):
        xs = lax.dot_general(onehot, h_bf[:, c * half:(c + 1) * half], (((0,), (0,)), ((), ())),
                             preferred_element_type=jnp.float32)
        xs_ref[:, c * half:(c + 1) * half] = xs.astype(jnp.bfloat16)


N_EPILOGUE_OUTS = 5


def _attn_out_kernel(a_ref, wo_ref, bo_ref, x_ref, mod_ref, g_ref, wrh_ref, wrl_ref, br_ref, *outs):
    m = jnp.dot(a_ref[...], wo_ref[...], preferred_element_type=jnp.float32) + bo_ref[...]
    _residual_ffn_router(m, x_ref[...], mod_ref[...], g_ref[...], wrh_ref[...], wrl_ref[...], br_ref[...], outs)


def _epilogue_out_specs():
    return [
        pl.BlockSpec((TM, D_MODEL), lambda i: (i, 0)),
        pl.BlockSpec((TILE_ROWS, D_MODEL), lambda i: (i, 0)),
        pl.BlockSpec((TM, TOP_K), lambda i: (i, 0)),
        pl.BlockSpec((TM, TOP_K), lambda i: (i, 0)),
        pl.BlockSpec((None, 1, N_EXPERTS), lambda i: (i, 0, 0)),
    ]


def _epilogue_out_shapes():
    return [
        jax.ShapeDtypeStruct((N_TOK, D_MODEL), jnp.float32),
        jax.ShapeDtypeStruct((LOCAL_ROWS, D_MODEL), jnp.bfloat16),
        jax.ShapeDtypeStruct((N_TOK, TOP_K), jnp.int32),
        jax.ShapeDtypeStruct((N_TOK, TOP_K), jnp.float32),
        jax.ShapeDtypeStruct((N_TILES, 1, N_EXPERTS), jnp.float32),
    ]


def _router_operands(w_r, b_r):
    hi, lo = _split_bf16(w_r)
    return hi, lo, b_r.reshape(1, N_EXPERTS)


def _router_specs():
    return [
        pl.BlockSpec((D_MODEL, N_EXPERTS), lambda i: (0, 0)),
        pl.BlockSpec((D_MODEL, N_EXPERTS), lambda i: (0, 0)),
        pl.BlockSpec((1, N_EXPERTS), lambda i: (0, 0)),
    ]


def _attn_out(attn, w_o, b_o, x, mod, g_ffn, w_r, b_r, layer):
    return pl.pallas_call(
        _attn_out_kernel,
        grid=(N_TOK // TM,),
        in_specs=[
            pl.BlockSpec((TM, Q_DIM), lambda i: (i, 0)),
            pl.BlockSpec((Q_DIM, D_MODEL), lambda i: (0, 0)),
            pl.BlockSpec((1, D_MODEL), lambda i: (0, 0)),
            pl.BlockSpec((TM, D_MODEL), lambda i: (i, 0)),
            _mod_spec(layer),
            pl.BlockSpec((1, D_MODEL), lambda i: (0, 0)),
        ] + _router_specs(),
        out_specs=_epilogue_out_specs(),
        out_shape=_epilogue_out_shapes(),
        compiler_params=_cparams("arbitrary"),
        name="attn_out_router",
    )(attn, w_o, b_o.reshape(1, D_MODEL), x, mod, g_ffn.reshape(1, D_MODEL), *_router_operands(w_r, b_r))


def _conv_out_kernel(u_ref, up_ref, wdw_ref, bdw_ref, lng_ref, lnb_ref, wo_ref, bo_ref,
                     x_ref, mod_ref, g_ref, wrh_ref, wrl_ref, br_ref, *rest):
    outs = rest[:N_EPILOGUE_OUTS]
    ext_ref, y_ref = rest[N_EPILOGUE_OUTS:]
    i = pl.program_id(0)
    first_tile = (i % (SEQ // TM)) == 0
    ext_ref[0:CONV_HALO, :] = jnp.where(first_tile, 0.0, up_ref[...])
    ext_ref[CONV_HALO:, :] = u_ref[...]
    lead = CONV_HALO - (CONV_WIDTH - 1)
    span = CONV_ROWS + CONV_HALO

    def chunk(r, carry):
        r0 = pl.multiple_of(r * CONV_ROWS, CONV_ROWS)
        for cb in range(ROW_TILES):
            cols = slice(cb * LANES, (cb + 1) * LANES)
            e = ext_ref[pl.ds(r0, span), cols]
            acc = jnp.zeros((CONV_ROWS, LANES), jnp.float32)
            for sub in range(SUBLANES):
                shifted = e if sub == 0 else pltpu.roll(e, span - sub, 0)
                for a in range(span // SUBLANES):
                    tap = a * SUBLANES + sub - lead
                    if 0 <= tap < CONV_WIDTH:
                        acc = acc + wdw_ref[tap:tap + 1, cols] * shifted[a * SUBLANES:a * SUBLANES + CONV_ROWS]
            y_ref[pl.ds(r0, CONV_ROWS), cols] = acc
        return carry

    lax.fori_loop(0, TM // CONV_ROWS, chunk, 0)
    y = y_ref[...] + bdw_ref[...]
    mu = jnp.mean(y, axis=-1, keepdims=True)
    var = jnp.mean(jnp.square(y - mu), axis=-1, keepdims=True)
    z = (y - mu) * lax.rsqrt(var + NORM_EPS) * lng_ref[...] + lnb_ref[...]
    z = z * jax.nn.sigmoid(z)
    m = jnp.dot(z.astype(jnp.bfloat16), wo_ref[...], preferred_element_type=jnp.float32) + bo_ref[...]
    _residual_ffn_router(m, x_ref[...], mod_ref[...], g_ref[...], wrh_ref[...], wrl_ref[...], br_ref[...],
                         outs)


def _conv_out(u, w_dw, b_dw, ln_g, ln_b, w_o, b_o, x, mod, g_ffn, w_r, b_r, layer):
    halo_blocks = TM // CONV_HALO
    row = lambda a: a.reshape(1, D_MODEL)
    return pl.pallas_call(
        _conv_out_kernel,
        grid=(N_TOK // TM,),
        in_specs=[
            pl.BlockSpec((TM, D_MODEL), lambda i: (i, 0)),
            pl.BlockSpec((CONV_HALO, D_MODEL), lambda i: (jnp.maximum(i * halo_blocks - 1, 0), 0)),
            pl.BlockSpec((CONV_WIDTH, D_MODEL), lambda i: (0, 0)),
            pl.BlockSpec((1, D_MODEL), lambda i: (0, 0)),
            pl.BlockSpec((1, D_MODEL), lambda i: (0, 0)),
            pl.BlockSpec((1, D_MODEL), lambda i: (0, 0)),
            pl.BlockSpec((D_MODEL, D_MODEL), lambda i: (0, 0)),
            pl.BlockSpec((1, D_MODEL), lambda i: (0, 0)),
            pl.BlockSpec((TM, D_MODEL), lambda i: (i, 0)),
            _mod_spec(layer),
            pl.BlockSpec((1, D_MODEL), lambda i: (0, 0)),
        ] + _router_specs(),
        out_specs=_epilogue_out_specs(),
        out_shape=_epilogue_out_shapes(),
        scratch_shapes=[
            pltpu.VMEM((TM + CONV_HALO, D_MODEL), jnp.float32),
            pltpu.VMEM((TM, D_MODEL), jnp.float32),
        ],
        compiler_params=_cparams("arbitrary"),
        name="conv_out_router",
    )(u, u, w_dw, row(b_dw), row(ln_g), row(ln_b), w_o, row(b_o), x, mod, row(g_ffn),
      *_router_operands(w_r, b_r))


def _routing(units):
    per_tile = units.reshape(N_TILES, N_EXPERTS).astype(jnp.int32)
    seg_off = jnp.cumsum(per_tile, axis=1) - per_tile
    used = jnp.sum(per_tile, axis=1)
    zero = jnp.zeros((N_EXPERTS, 1), jnp.int32)
    cum = jnp.concatenate([zero, jnp.cumsum(per_tile.T, axis=1)], axis=1)
    total = cum[:, -1]
    n_blocks = (total + BLK_UNITS - 1) // BLK_UNITS
    blk_end = jnp.cumsum(n_blocks)
    blk_start = blk_end - n_blocks
    n_live = blk_end[-1]
    blk = jnp.arange(N_BLKP, dtype=jnp.int32)
    blk_exp = jnp.minimum(jnp.sum(blk[:, None] >= blk_end[None, :], axis=1), N_EXPERTS - 1)
    blk_exp = jnp.where(blk < n_live, blk_exp, blk_exp[n_live - 1]).astype(jnp.int32)
    blk_at = (blk - blk_start[blk_exp]) * BLK_UNITS
    at = blk_at[:, None] + jnp.arange(BLK_UNITS, dtype=jnp.int32)[None, :]
    starts = cum[blk_exp]
    shift = (seg_off.T - cum[:, :N_TILES])[blk_exp]
    tile = jnp.minimum(jnp.sum(starts[:, None, 1:] <= at[:, :, None], axis=-1), N_TILES - 1)
    in_tile = tile[:, :, None] == jnp.arange(N_TILES, dtype=jnp.int32)[None, None, :]
    local = tile * TILE_UNITS + at + jnp.sum(jnp.where(in_tile, shift[:, None, :], 0), axis=-1)
    unit_map = jnp.where(at < total[blk_exp][:, None], local, -1)
    experts = jnp.arange(N_EXPERTS, dtype=jnp.int32)
    owners = lax.cummin(jnp.where(n_blocks > 0, experts, N_EXPERTS), reverse=True)
    next_exp = jnp.concatenate([owners[1:], jnp.full((1,), N_EXPERTS, jnp.int32)])
    next_exp = jnp.where(next_exp < N_EXPERTS, next_exp, -1)
    i32 = lambda a: a.astype(jnp.int32)
    return dict(blk_exp=blk_exp, unit_map=i32(unit_map.reshape(-1)), n_live=i32(n_live.reshape(1)), used=i32(used),
                next_exp=i32(next_exp))


def _unit_rows(unit, n_units):
    return pl.ds(pl.multiple_of(unit * SEG, SEG), n_units * SEG)


def _moe_kernel(exp_ref, map_ref, live_ref, used_ref, next_ref,
                xs_hbm, win_hbm, bin_ref, wout_hbm, bout_ref, ys_hbm,
                xbuf0, xbuf1, ybuf0, ybuf1, zbuf, win_f32, wout_f32, win_bf, wout_bf, gsem, ssem, zsem, wsem,
                *, layer):
    b = pl.program_id(0)
    last = pl.num_programs(0) - 1
    xbufs = (xbuf0, xbuf1)
    ybufs = (ybuf0, ybuf1)

    def zero_tail(tile, act):
        used = used_ref[tile]
        n = TILE_UNITS - used
        for size in _bits_down(SLACK_UNITS):
            @pl.when((n & size) != 0)
            def _():
                start = n & ~(2 * size - 1)
                act(pltpu.make_async_copy(zbuf.at[_unit_rows(0, size), :],
                                          ys_hbm.at[_unit_rows(tile * TILE_UNITS + used + start, size), :], zsem))

    def gather(block, buf):
        for u in range(BLK_UNITS):
            local = jnp.maximum(map_ref[block * BLK_UNITS + u], 0)
            pltpu.make_async_copy(xs_hbm.at[_unit_rows(local, 1), :], xbufs[buf].at[_unit_rows(u, 1), :],
                                  gsem.at[buf]).start()

    def scatter(block, buf):
        trash = N_TILES * TILE_UNITS + buf * BLK_UNITS
        for u in range(BLK_UNITS):
            local = map_ref[block * BLK_UNITS + u]
            local = jnp.where(local >= 0, local, trash + u)
            pltpu.make_async_copy(ybufs[buf].at[_unit_rows(u, 1), :], ys_hbm.at[_unit_rows(local, 1), :],
                                  ssem.at[buf]).start()

    def wait_rows(buffer, sem):
        pltpu.make_async_copy(buffer, buffer, sem).wait()

    def weight_copies(e):
        return (pltpu.make_async_copy(win_hbm.at[layer, e], win_f32, wsem.at[0]),
                pltpu.make_async_copy(wout_hbm.at[layer, e], wout_f32, wsem.at[1]))

    @pl.when(b == 0)
    def _():
        for copy in weight_copies(exp_ref[0]):
            copy.start(priority=1)

    new_expert = jnp.logical_or(b == 0, exp_ref[b] != exp_ref[jnp.maximum(b - 1, 0)])

    @pl.when(new_expert)
    def _():
        e = exp_ref[b]
        for copy in weight_copies(e):
            copy.wait()
        win_bf[...] = win_f32[...].astype(jnp.bfloat16)
        wout_bf[...] = wout_f32[...].astype(jnp.bfloat16)
        upcoming = next_ref[e]

        @pl.when(upcoming >= 0)
        def _():
            for copy in weight_copies(upcoming):
                copy.start(priority=1)

    @pl.when(b == 0)
    def _():
        gather(0, 0)
        ybuf0[...] = jnp.zeros_like(ybuf0)
        ybuf1[...] = jnp.zeros_like(ybuf1)
        zbuf[...] = jnp.zeros_like(zbuf)
        pltpu.make_async_copy(ybuf0, ys_hbm.at[_unit_rows(N_TILES * TILE_UNITS, BLK_UNITS), :], ssem.at[0]).start()
        trash1 = pltpu.make_async_copy(zbuf.at[_unit_rows(0, BLK_UNITS), :],
                                       ys_hbm.at[_unit_rows(N_TILES * TILE_UNITS + BLK_UNITS, BLK_UNITS), :], zsem)
        trash1.start()
        trash1.wait()

    @pl.when(jnp.logical_and(b >= 1, b <= N_TILES))
    def _():
        zero_tail(b - 1, lambda copy: copy.wait())

    @pl.when(b < N_TILES)
    def _():
        zero_tail(b, lambda copy: copy.start())

    def step(cur):
        nxt = 1 - cur
        wait_rows(xbufs[cur], gsem.at[cur])
        wait_rows(ybufs[cur], ssem.at[cur])
        gather(jnp.minimum(b + 1, last), nxt)
        scatter(jnp.maximum(b - 1, 0), nxt)

        @pl.when(b < live_ref[0])
        def _():
            u = jnp.dot(xbufs[cur][...], win_bf[...], preferred_element_type=jnp.float32) + bin_ref[...]
            glu = jnp.minimum(u[:, :D_EXPERT], SWIGLU_LIMIT)
            lin = jnp.clip(u[:, D_EXPERT:], -SWIGLU_LIMIT, SWIGLU_LIMIT)
            act = glu * jax.nn.sigmoid(SWIGLU_ALPHA * glu) * (lin + 1.0)
            y = jnp.dot(act.astype(jnp.bfloat16), wout_bf[...], preferred_element_type=jnp.float32) + bout_ref[...]
            ybufs[cur][...] = y.astype(jnp.bfloat16)

        @pl.when(b == last)
        def _():
            scatter(b, cur)
            wait_rows(ybufs[nxt], ssem.at[nxt])
            wait_rows(ybufs[cur], ssem.at[cur])
            wait_rows(xbufs[nxt], gsem.at[nxt])

    for parity in range(2):
        pl.when(b % 2 == parity)(functools.partial(step, parity))


def _moe_experts(xs_local, route, w_in, b_in, w_out, b_out, layer):
    by_expert = lambda b, ex, *_: (layer, ex[b], 0, 0)
    tables = [route[k] for k in ("blk_exp", "unit_map", "n_live", "used", "next_exp")]
    grid_spec = pltpu.PrefetchScalarGridSpec(
        num_scalar_prefetch=len(tables),
        grid=(N_BLKP,),
        in_specs=[
            pl.BlockSpec(memory_space=pl.ANY),
            pl.BlockSpec(memory_space=pl.ANY),
            pl.BlockSpec((None, None, 1, 2 * D_EXPERT), by_expert),
            pl.BlockSpec(memory_space=pl.ANY),
            pl.BlockSpec((None, None, 1, D_MODEL), by_expert),
        ],
        out_specs=pl.BlockSpec(memory_space=pl.ANY),
        scratch_shapes=[
            pltpu.VMEM((BM, D_MODEL), jnp.bfloat16),
            pltpu.VMEM((BM, D_MODEL), jnp.bfloat16),
            pltpu.VMEM((BM, D_MODEL), jnp.bfloat16),
            pltpu.VMEM((BM, D_MODEL), jnp.bfloat16),
            pltpu.VMEM((SLACK_UNITS * SEG, D_MODEL), jnp.bfloat16),
            pltpu.VMEM((D_MODEL, 2 * D_EXPERT), jnp.float32),
            pltpu.VMEM((D_EXPERT, D_MODEL), jnp.float32),
            pltpu.VMEM((D_MODEL, 2 * D_EXPERT), jnp.bfloat16),
            pltpu.VMEM((D_EXPERT, D_MODEL), jnp.bfloat16),
            pltpu.SemaphoreType.DMA((2,)),
            pltpu.SemaphoreType.DMA((2,)),
            pltpu.SemaphoreType.DMA(()),
            pltpu.SemaphoreType.DMA((2,)),
        ],
    )
    return pl.pallas_call(
        functools.partial(_moe_kernel, layer=layer),
        grid_spec=grid_spec,
        out_shape=jax.ShapeDtypeStruct((LOCAL_ROWS + 2 * BM, D_MODEL), jnp.bfloat16),
        compiler_params=_cparams("arbitrary"),
        name="moe_experts",
    )(*tables, xs_local, w_in, b_in.reshape(DEPTH, N_EXPERTS, 1, 2 * D_EXPERT), w_out,
      b_out.reshape(DEPTH, N_EXPERTS, 1, D_MODEL))


def _combine_kernel(ys_ref, pos_ref, gate_ref, x_ref, mod_ref, gfin_ref, o_ref, *, final_norm):
    pos = pos_ref[...]
    gates = gate_ref[...]
    q = lax.broadcasted_iota(jnp.int32, (TM, TILE_ROWS), 1)
    weights = jnp.zeros((TM, TILE_ROWS), jnp.float32)
    for k in range(TOP_K):
        weights = jnp.where(q == pos[:, k:k + 1], gates[:, k:k + 1], weights)
    mixed = jnp.dot(weights.astype(jnp.bfloat16), ys_ref[...], preferred_element_type=jnp.float32)
    x_new = x_ref[...] + mod_ref[5:6, :] * mixed
    if final_norm:
        x_new = x_new * lax.rsqrt(jnp.mean(x_new * x_new, axis=-1, keepdims=True) + NORM_EPS) * gfin_ref[...]
    o_ref[...] = x_new


def _combine(ys_local, pos, gates, x, mod, g_final, layer, *, final_norm):
    return pl.pallas_call(
        functools.partial(_combine_kernel, final_norm=final_norm),
        grid=(N_TILES,),
        in_specs=[
            pl.BlockSpec((TILE_ROWS, D_MODEL), lambda i: (i, 0)),
            pl.BlockSpec((TM, TOP_K), lambda i: (i, 0)),
            pl.BlockSpec((TM, TOP_K), lambda i: (i, 0)),
            pl.BlockSpec((TM, D_MODEL), lambda i: (i, 0)),
            _mod_spec(layer),
            pl.BlockSpec((1, D_MODEL), lambda i: (0, 0)),
        ],
        out_specs=pl.BlockSpec((TM, D_MODEL), lambda i: (i, 0)),
        out_shape=jax.ShapeDtypeStruct((N_TOK, D_MODEL), jnp.float32),
        compiler_params=_cparams("arbitrary"),
        name="moe_combine",
    )(ys_local, pos, gates, x, mod, g_final.reshape(1, D_MODEL))


def kernel(x, c, w_mod, b_mod, norm_mix_g, norm_ffn_g, attn_w_qkv, attn_b_qkv, attn_w_o, attn_b_o,
           attn_sinks, rel_bias_table, conv_w_in, conv_b_in, conv_w_dw, conv_b_dw, conv_ln_g, conv_ln_b,
           conv_w_out, conv_b_out, router_w, router_b, moe_w_in, moe_b_in, moe_w_out, moe_b_out,
           final_norm_g):
    bf16 = jnp.bfloat16
    mod = _modulation(c, w_mod, b_mod)
    bias = _rel_bias(rel_bias_table)
    xf = x.reshape(N_TOK, D_MODEL)
    for i in range(DEPTH):
        j = i // 2
        if i % 2 == 0:
            qkv = _in_proj(xf, mod, norm_mix_g[i], attn_w_qkv[j].astype(bf16), attn_b_qkv[j], i,
                           glu=False, out_dtype=bf16)
            attn = _attention(qkv, bias, attn_sinks, j)
            xf, xs_local, pos, gates, units = _attn_out(attn, attn_w_o[j].astype(bf16), attn_b_o[j], xf, mod,
                                                        norm_ffn_g[i], router_w[i], router_b[i], i)
        else:
            u = _in_proj(xf, mod, norm_mix_g[i], conv_w_in[j].astype(bf16), conv_b_in[j], i,
                         glu=True, out_dtype=jnp.float32)
            xf, xs_local, pos, gates, units = _conv_out(u, conv_w_dw[j], conv_b_dw[j], conv_ln_g[j], conv_ln_b[j],
                                                        conv_w_out[j].astype(bf16), conv_b_out[j], xf, mod,
                                                        norm_ffn_g[i], router_w[i], router_b[i], i)
        route = _routing(units)
        ys_local = _moe_experts(xs_local, route, moe_w_in, moe_b_in, moe_w_out, moe_b_out, i)
        xf = _combine(ys_local, pos, gates, xf, mod, final_norm_g, i, final_norm=(i == DEPTH - 1))
    return xf.reshape(BATCH, SEQ, D_MODEL)
```

```python
import functools
import math

import jax
import jax.numpy as jnp
from jax import lax
from jax.experimental import pallas as pl
from jax.experimental.pallas import tpu as pltpu

D_MODEL = 1024
BATCH = 8
SEQ = 2048
DEPTH = 4
N_TOK = BATCH * SEQ
HEAD_DIM = 64
N_Q_HEADS = 16
N_KV_HEADS = 4
GQA_GROUP = 4
WINDOW = 128
Q_DIM = 1024
KV_DIM = 256
QKV_DIM = 1536
NUM_BUCKETS = 32
MAX_EXACT = 16
MAX_DISTANCE = 128
CONV_WIDTH = 31
N_EXPERTS = 32
TOP_K = 4
D_EXPERT = 1024
SWIGLU_LIMIT = 7.0
SWIGLU_ALPHA = 1.702
NORM_EPS = 1e-5
NEG_INF = -1e30

LANES = 128
SUBLANES = 8
ROW_TILES = D_MODEL // LANES
N_SLOT = N_TOK * TOP_K

TM = 512
TM_ATTN = WINDOW
N_TILES = N_TOK // TM
SEG = 16
TILE_ROWS = -(-(TM * TOP_K + N_EXPERTS * (SEG - 1)) // 256) * 256
TILE_UNITS = TILE_ROWS // SEG
BM = 512
BLK_UNITS = BM // SEG
SLACK_UNITS = TILE_UNITS - TM * TOP_K // SEG


def _bits_down(n):
    return tuple(1 << i for i in reversed(range(n.bit_length())))
N_BLKP = N_TILES * TILE_ROWS // BM + N_EXPERTS
LOCAL_ROWS = N_TILES * TILE_ROWS
CONV_HALO = 32
CONV_ROWS = 64

VMEM_LIMIT = 56 * 1024 * 1024


def _cparams(*sem):
    return pltpu.CompilerParams(dimension_semantics=sem, vmem_limit_bytes=VMEM_LIMIT)


def _mod_kernel(c_ref, w_ref, b_ref, o_ref):
    c = c_ref[...]
    c_act = (c * jax.nn.sigmoid(c)).astype(jnp.bfloat16)
    o_ref[...] = jnp.dot(c_act, w_ref[...].astype(jnp.bfloat16),
                         preferred_element_type=jnp.float32) + b_ref[...]


def _modulation(c, w_mod, b_mod):
    out = pl.pallas_call(
        _mod_kernel,
        grid=(DEPTH, 6),
        in_specs=[
            pl.BlockSpec((BATCH, D_MODEL), lambda i, j: (0, 0)),
            pl.BlockSpec((None, D_MODEL, D_MODEL), lambda i, j: (i, 0, j)),
            pl.BlockSpec((None, None, 1, D_MODEL), lambda i, j: (i, j, 0, 0)),
        ],
        out_specs=pl.BlockSpec((None, None, BATCH, D_MODEL), lambda i, j: (i, j, 0, 0)),
        out_shape=jax.ShapeDtypeStruct((DEPTH, 6, BATCH, D_MODEL), jnp.float32),
        compiler_params=_cparams("arbitrary", "arbitrary"),
        name="adaln_modulation",
    )(c, w_mod, b_mod.reshape(DEPTH, 6, 1, D_MODEL))
    return jnp.transpose(out, (0, 2, 1, 3))


def _mod_spec(layer):
    return pl.BlockSpec((None, None, 6, D_MODEL), lambda i: (layer, i // (SEQ // TM), 0, 0))


def _rms_modulate(x, g, scale, shift):
    y = x * lax.rsqrt(jnp.mean(x * x, axis=-1, keepdims=True) + NORM_EPS)
    return (y * g) * (1.0 + scale) + shift


def _in_proj_kernel(x_ref, mod_ref, g_ref, w_ref, b_ref, o_ref, *, glu):
    h = _rms_modulate(x_ref[...], g_ref[...], mod_ref[1:2, :], mod_ref[0:1, :])
    u = jnp.dot(h.astype(jnp.bfloat16), w_ref[...], preferred_element_type=jnp.float32) + b_ref[...]
    if glu:
        half = u.shape[-1] // 2
        u = u[:, :half] * jax.nn.sigmoid(u[:, half:])
    o_ref[...] = u.astype(o_ref.dtype)


def _in_proj(x, mod, g, w, b, layer, *, glu, out_dtype):
    n_in = w.shape[-1]
    n_out = n_in // 2 if glu else n_in
    return pl.pallas_call(
        functools.partial(_in_proj_kernel, glu=glu),
        grid=(N_TOK // TM,),
        in_specs=[
            pl.BlockSpec((TM, D_MODEL), lambda i: (i, 0)),
            _mod_spec(layer),
            pl.BlockSpec((1, D_MODEL), lambda i: (0, 0)),
            pl.BlockSpec((D_MODEL, n_in), lambda i: (0, 0)),
            pl.BlockSpec((1, n_in), lambda i: (0, 0)),
        ],
        out_specs=pl.BlockSpec((TM, n_out), lambda i: (i, 0)),
        out_shape=jax.ShapeDtypeStruct((N_TOK, n_out), out_dtype),
        compiler_params=_cparams("arbitrary"),
        name="mixer_in_proj_glu" if glu else "mixer_in_proj",
    )(x, mod, g.reshape(1, D_MODEL), w, b.reshape(1, n_in))


def _band_geometry():
    dist = jnp.arange(WINDOW)[:, None] + WINDOW - jnp.arange(2 * WINDOW)[None, :]
    n = jnp.maximum(dist, 0)
    nf = jnp.maximum(n, 1).astype(jnp.float32)
    large = MAX_EXACT + (jnp.log(nf / MAX_EXACT) / math.log(MAX_DISTANCE / MAX_EXACT)
                         * (NUM_BUCKETS - MAX_EXACT)).astype(jnp.int32)
    large = jnp.minimum(large, NUM_BUCKETS - 1)
    bucket = jnp.where(n < MAX_EXACT, n, large)
    return dist, bucket


def _bias_kernel(table_ref, bucket_ref, dist_ref, o_ref):
    h = pl.program_id(0)
    bucket = bucket_ref[...]
    dist = dist_ref[...]
    acc = jnp.zeros(bucket.shape, jnp.float32)
    for b in range(NUM_BUCKETS):
        acc = jnp.where(bucket == b, table_ref[b, h], acc)
    in_window = (dist >= 0) & (dist < WINDOW)
    o_ref[...] = jnp.where(in_window, acc, NEG_INF)


def _rel_bias(table):
    dist, bucket = _band_geometry()
    return pl.pallas_call(
        _bias_kernel,
        grid=(N_Q_HEADS,),
        in_specs=[
            pl.BlockSpec(memory_space=pltpu.SMEM),
            pl.BlockSpec((WINDOW, 2 * WINDOW), lambda h: (0, 0)),
            pl.BlockSpec((WINDOW, 2 * WINDOW), lambda h: (0, 0)),
        ],
        out_specs=pl.BlockSpec((None, WINDOW, 2 * WINDOW), lambda h: (h, 0, 0)),
        out_shape=jax.ShapeDtypeStruct((N_Q_HEADS, WINDOW, 2 * WINDOW), jnp.float32),
        compiler_params=_cparams("arbitrary"),
        name="rel_bias",
    )(table, bucket.astype(jnp.int32), dist.astype(jnp.int32))


def _attn_kernel(sink_ref, q_ref, kv_ref, kvp_ref, bias_ref, o_ref, *, layer):
    i = pl.program_id(0)
    first_block = (i % (SEQ // WINDOW)) == 0
    q = q_ref[...]
    kv = jnp.concatenate([kvp_ref[...], kv_ref[...]], axis=0)
    col = lax.broadcasted_iota(jnp.int32, (WINDOW, 2 * WINDOW), 1)
    prev_ok = jnp.logical_or(col >= WINDOW, jnp.logical_not(first_block))
    outs = []
    for hk in range(N_KV_HEADS):
        k = kv[:, hk * HEAD_DIM:(hk + 1) * HEAD_DIM]
        v = kv[:, KV_DIM + hk * HEAD_DIM:KV_DIM + (hk + 1) * HEAD_DIM]
        for g in range(GQA_GROUP):
            hq = hk * GQA_GROUP + g
            qh = q[:, hq * HEAD_DIM:(hq + 1) * HEAD_DIM] * (HEAD_DIM ** -0.5)
            s = lax.dot_general(qh, k, (((1,), (1,)), ((), ())), preferred_element_type=jnp.float32)
            s = jnp.where(prev_ok, s + bias_ref[hq], NEG_INF)
            sink = sink_ref[layer, hq]
            m = jnp.maximum(jnp.max(s, axis=-1, keepdims=True), sink)
            p = jnp.exp(s - m)
            denom = jnp.sum(p, axis=-1, keepdims=True) + jnp.exp(sink - m)
            o = jnp.dot(p.astype(jnp.bfloat16), v, preferred_element_type=jnp.float32)
            outs.append((o / denom).astype(jnp.bfloat16))
    o_ref[...] = jnp.concatenate(outs, axis=-1)


def _attention(qkv, bias, sinks, layer):
    kv_col = Q_DIM // (2 * KV_DIM)
    return pl.pallas_call(
        functools.partial(_attn_kernel, layer=layer),
        grid=(N_TOK // WINDOW,),
        in_specs=[
            pl.BlockSpec(memory_space=pltpu.SMEM),
            pl.BlockSpec((WINDOW, Q_DIM), lambda i: (i, 0)),
            pl.BlockSpec((WINDOW, 2 * KV_DIM), lambda i: (i, kv_col)),
            pl.BlockSpec((WINDOW, 2 * KV_DIM), lambda i: (jnp.maximum(i - 1, 0), kv_col)),
            pl.BlockSpec((N_Q_HEADS, WINDOW, 2 * WINDOW), lambda i: (0, 0, 0)),
        ],
        out_specs=pl.BlockSpec((WINDOW, Q_DIM), lambda i: (i, 0)),
        out_shape=jax.ShapeDtypeStruct((N_TOK, Q_DIM), jnp.bfloat16),
        compiler_params=_cparams("arbitrary"),
        name="swa_attention",
    )(sinks, qkv, qkv, qkv, bias)


def _split_bf16(a):
    hi = a.astype(jnp.bfloat16)
    lo = (a - hi.astype(jnp.float32)).astype(jnp.bfloat16)
    return hi, lo


def _residual_ffn_router(m, x, mod, g_ffn, wr_hi, wr_lo, b_r, outs):
    xo_ref, xs_ref, pos_ref, gate_ref, units_ref = outs
    x_new = x + mod[2:3, :] * m
    xo_ref[...] = x_new
    h = _rms_modulate(x_new, g_ffn, mod[4:5, :], mod[3:4, :])
    rows = h.shape[0]
    h_hi, h_lo = _split_bf16(h)
    logits = (jnp.dot(h_hi, wr_hi, preferred_element_type=jnp.float32)
              + jnp.dot(h_hi, wr_lo, preferred_element_type=jnp.float32)
              + jnp.dot(h_lo, wr_hi, preferred_element_type=jnp.float32)) + b_r
    lane = lax.broadcasted_iota(jnp.int32, logits.shape, 1)
    vals, ids = [], []
    for _ in range(TOP_K):
        best = jnp.max(logits, axis=-1, keepdims=True)
        arg = jnp.min(jnp.where(logits == best, lane, N_EXPERTS), axis=-1, keepdims=True)
        vals.append(best)
        ids.append(arg)
        logits = jnp.where(lane == arg, -jnp.inf, logits)
    exps = [jnp.exp(v - vals[0]) for v in vals]
    total = exps[0] + exps[1] + exps[2] + exps[3]
    gate_ref[...] = jnp.concatenate([e / total for e in exps], axis=-1)

    member = jnp.zeros(logits.shape, jnp.float32)
    for arg in ids:
        member = member + (lane == arg).astype(jnp.float32)
    units = jnp.floor((jnp.sum(member, axis=0, keepdims=True) + (SEG - 1)) * (1.0 / SEG))
    units_ref[...] = units
    lower_experts = (lax.broadcasted_iota(jnp.int32, (N_EXPERTS, N_EXPERTS), 0)
                     < lax.broadcasted_iota(jnp.int32, (N_EXPERTS, N_EXPERTS), 1)).astype(jnp.bfloat16)
    seg_start = SEG * jnp.dot(jnp.broadcast_to(units, (SUBLANES, N_EXPERTS)).astype(jnp.bfloat16), lower_experts,
                              preferred_element_type=jnp.float32)[0:1, :]
    earlier = (lax.broadcasted_iota(jnp.int32, (rows, rows), 0)
               > lax.broadcasted_iota(jnp.int32, (rows, rows), 1)).astype(jnp.bfloat16)
    place = seg_start + jnp.dot(earlier, member.astype(jnp.bfloat16), preferred_element_type=jnp.float32)
    pos = [jnp.sum(jnp.where(lane == arg, place, 0.0), axis=-1, keepdims=True).astype(jnp.int32) for arg in ids]
    pos_ref[...] = jnp.concatenate(pos, axis=-1)

    q = lax.broadcasted_iota(jnp.int32, (rows, TILE_ROWS), 1)
    hit = (q == pos[0]) | (q == pos[1]) | (q == pos[2]) | (q == pos[3])
    onehot = jnp.where(hit, 1.0, 0.0).astype(jnp.bfloat16)
    h_bf = h.astype(jnp.bfloat16)
    half = D_MODEL // 2
    for c in range(2):
        xs = lax.dot_general(onehot, h_bf[:, c * half:(c + 1) * half], (((0,), (0,)), ((), ())),
                             preferred_element_type=jnp.float32)
        xs_ref[:, c * half:(c + 1) * half] = xs.astype(jnp.bfloat16)


N_EPILOGUE_OUTS = 5


def _attn_out_kernel(a_ref, wo_ref, bo_ref, x_ref, mod_ref, g_ref, wrh_ref, wrl_ref, br_ref, *outs):
    m = jnp.dot(a_ref[...], wo_ref[...], preferred_element_type=jnp.float32) + bo_ref[...]
    _residual_ffn_router(m, x_ref[...], mod_ref[...], g_ref[...], wrh_ref[...], wrl_ref[...], br_ref[...], outs)


def _epilogue_out_specs():
    return [
        pl.BlockSpec((TM, D_MODEL), lambda i: (i, 0)),
        pl.BlockSpec((TILE_ROWS, D_MODEL), lambda i: (i, 0)),
        pl.BlockSpec((TM, TOP_K), lambda i: (i, 0)),
        pl.BlockSpec((TM, TOP_K), lambda i: (i, 0)),
        pl.BlockSpec((None, 1, N_EXPERTS), lambda i: (i, 0, 0)),
    ]


def _epilogue_out_shapes():
    return [
        jax.ShapeDtypeStruct((N_TOK, D_MODEL), jnp.float32),
        jax.ShapeDtypeStruct((LOCAL_ROWS, D_MODEL), jnp.bfloat16),
        jax.ShapeDtypeStruct((N_TOK, TOP_K), jnp.int32),
        jax.ShapeDtypeStruct((N_TOK, TOP_K), jnp.float32),
        jax.ShapeDtypeStruct((N_TILES, 1, N_EXPERTS), jnp.float32),
    ]


def _router_operands(w_r, b_r):
    hi, lo = _split_bf16(w_r)
    return hi, lo, b_r.reshape(1, N_EXPERTS)


def _router_specs():
    return [
        pl.BlockSpec((D_MODEL, N_EXPERTS), lambda i: (0, 0)),
        pl.BlockSpec((D_MODEL, N_EXPERTS), lambda i: (0, 0)),
        pl.BlockSpec((1, N_EXPERTS), lambda i: (0, 0)),
    ]


def _attn_out(attn, w_o, b_o, x, mod, g_ffn, w_r, b_r, layer):
    return pl.pallas_call(
        _attn_out_kernel,
        grid=(N_TOK // TM,),
        in_specs=[
            pl.BlockSpec((TM, Q_DIM), lambda i: (i, 0)),
            pl.BlockSpec((Q_DIM, D_MODEL), lambda i: (0, 0)),
            pl.BlockSpec((1, D_MODEL), lambda i: (0, 0)),
            pl.BlockSpec((TM, D_MODEL), lambda i: (i, 0)),
            _mod_spec(layer),
            pl.BlockSpec((1, D_MODEL), lambda i: (0, 0)),
        ] + _router_specs(),
        out_specs=_epilogue_out_specs(),
        out_shape=_epilogue_out_shapes(),
        compiler_params=_cparams("arbitrary"),
        name="attn_out_router",
    )(attn, w_o, b_o.reshape(1, D_MODEL), x, mod, g_ffn.reshape(1, D_MODEL), *_router_operands(w_r, b_r))


def _conv_out_kernel(u_ref, up_ref, wdw_ref, bdw_ref, lng_ref, lnb_ref, wo_ref, bo_ref,
                     x_ref, mod_ref, g_ref, wrh_ref, wrl_ref, br_ref, *rest):
    outs = rest[:N_EPILOGUE_OUTS]
    ext_ref, y_ref = rest[N_EPILOGUE_OUTS:]
    i = pl.program_id(0)
    first_tile = (i % (SEQ // TM)) == 0
    ext_ref[0:CONV_HALO, :] = jnp.where(first_tile, 0.0, up_ref[...])
    ext_ref[CONV_HALO:, :] = u_ref[...]
    lead = CONV_HALO - (CONV_WIDTH - 1)
    span = CONV_ROWS + CONV_HALO

    def chunk(r, carry):
        r0 = pl.multiple_of(r * CONV_ROWS, CONV_ROWS)
        for cb in range(ROW_TILES):
            cols = slice(cb * LANES, (cb + 1) * LANES)
            e = ext_ref[pl.ds(r0, span), cols]
            acc = jnp.zeros((CONV_ROWS, LANES), jnp.float32)
            for sub in range(SUBLANES):
                shifted = e if sub == 0 else pltpu.roll(e, span - sub, 0)
                for a in range(span // SUBLANES):
                    tap = a * SUBLANES + sub - lead
                    if 0 <= tap < CONV_WIDTH:
                        acc = acc + wdw_ref[tap:tap + 1, cols] * shifted[a * SUBLANES:a * SUBLANES + CONV_ROWS]
            y_ref[pl.ds(r0, CONV_ROWS), cols] = acc
        return carry

    lax.fori_loop(0, TM // CONV_ROWS, chunk, 0)
    y = y_ref[...] + bdw_ref[...]
    mu = jnp.mean(y, axis=-1, keepdims=True)
    var = jnp.mean(jnp.square(y - mu), axis=-1, keepdims=True)
    z = (y - mu) * lax.rsqrt(var + NORM_EPS) * lng_ref[...] + lnb_ref[...]
    z = z * jax.nn.sigmoid(z)
    m = jnp.dot(z.astype(jnp.bfloat16), wo_ref[...], preferred_element_type=jnp.float32) + bo_ref[...]
    _residual_ffn_router(m, x_ref[...], mod_ref[...], g_ref[...], wrh_ref[...], wrl_ref[...], br_ref[...],
                         outs)


def _conv_out(u, w_dw, b_dw, ln_g, ln_b, w_o, b_o, x, mod, g_ffn, w_r, b_r, layer):
    halo_blocks = TM // CONV_HALO
    row = lambda a: a.reshape(1, D_MODEL)
    return pl.pallas_call(
        _conv_out_kernel,
        grid=(N_TOK // TM,),
        in_specs=[
            pl.BlockSpec((TM, D_MODEL), lambda i: (i, 0)),
            pl.BlockSpec((CONV_HALO, D_MODEL), lambda i: (jnp.maximum(i * halo_blocks - 1, 0), 0)),
            pl.BlockSpec((CONV_WIDTH, D_MODEL), lambda i: (0, 0)),
            pl.BlockSpec((1, D_MODEL), lambda i: (0, 0)),
            pl.BlockSpec((1, D_MODEL), lambda i: (0, 0)),
            pl.BlockSpec((1, D_MODEL), lambda i: (0, 0)),
            pl.BlockSpec((D_MODEL, D_MODEL), lambda i: (0, 0)),
            pl.BlockSpec((1, D_MODEL), lambda i: (0, 0)),
            pl.BlockSpec((TM, D_MODEL), lambda i: (i, 0)),
            _mod_spec(layer),
            pl.BlockSpec((1, D_MODEL), lambda i: (0, 0)),
        ] + _router_specs(),
        out_specs=_epilogue_out_specs(),
        out_shape=_epilogue_out_shapes(),
        scratch_shapes=[
            pltpu.VMEM((TM + CONV_HALO, D_MODEL), jnp.float32),
            pltpu.VMEM((TM, D_MODEL), jnp.float32),
        ],
        compiler_params=_cparams("arbitrary"),
        name="conv_out_router",
    )(u, u, w_dw, row(b_dw), row(ln_g), row(ln_b), w_o, row(b_o), x, mod, row(g_ffn),
      *_router_operands(w_r, b_r))


def _routing(units):
    per_tile = units.reshape(N_TILES, N_EXPERTS).astype(jnp.int32)
    seg_off = jnp.cumsum(per_tile, axis=1) - per_tile
    used = jnp.sum(per_tile, axis=1)
    zero = jnp.zeros((N_EXPERTS, 1), jnp.int32)
    cum = jnp.concatenate([zero, jnp.cumsum(per_tile.T, axis=1)], axis=1)
    total = cum[:, -1]
    n_blocks = (total + BLK_UNITS - 1) // BLK_UNITS
    blk_end = jnp.cumsum(n_blocks)
    blk_start = blk_end - n_blocks
    n_live = blk_end[-1]
    blk = jnp.arange(N_BLKP, dtype=jnp.int32)
    blk_exp = jnp.minimum(jnp.sum(blk[:, None] >= blk_end[None, :], axis=1), N_EXPERTS - 1)
    blk_exp = jnp.where(blk < n_live, blk_exp, blk_exp[n_live - 1]).astype(jnp.int32)
    per_expert = jnp.concatenate([cum, seg_off.T - cum[:, :N_TILES], blk_start[:, None], total[:, None]], axis=1)
    picked = jnp.dot(jax.nn.one_hot(blk_exp, N_EXPERTS, dtype=jnp.float32), per_expert.astype(jnp.float32),
                     precision=lax.Precision.HIGHEST).astype(jnp.int32)
    starts = picked[:, :N_TILES + 1]
    shift = picked[:, N_TILES + 1:2 * N_TILES + 1]
    blk_at = (blk - picked[:, -2]) * BLK_UNITS
    at = blk_at[:, None] + jnp.arange(BLK_UNITS, dtype=jnp.int32)[None, :]
    tile = jnp.minimum(jnp.sum(starts[:, None, 1:] <= at[:, :, None], axis=-1), N_TILES - 1)
    in_tile = tile[:, :, None] == jnp.arange(N_TILES, dtype=jnp.int32)[None, None, :]
    local = tile * TILE_UNITS + at + jnp.sum(jnp.where(in_tile, shift[:, None, :], 0), axis=-1)
    unit_map = jnp.where(at < picked[:, -1:], local, -1)
    experts = jnp.arange(N_EXPERTS, dtype=jnp.int32)
    owners = lax.cummin(jnp.where(n_blocks > 0, experts, N_EXPERTS), reverse=True)
    next_exp = jnp.concatenate([owners[1:], jnp.full((1,), N_EXPERTS, jnp.int32)])
    next_exp = jnp.where(next_exp < N_EXPERTS, next_exp, -1)
    i32 = lambda a: a.astype(jnp.int32)
    held = jnp.clip(picked[:, -1] - blk_at, 0, BLK_UNITS)
    return dict(blk_exp=blk_exp, unit_map=i32(unit_map.reshape(-1)), held=i32(held), used=i32(used),
                next_exp=i32(next_exp))


def _unit_rows(unit, n_units):
    return pl.ds(pl.multiple_of(unit * SEG, SEG), n_units * SEG)


def _moe_kernel(exp_ref, map_ref, held_ref, used_ref, next_ref,
                xs_hbm, win_hbm, bin_ref, wout_hbm, bout_ref, ys_hbm,
                xbuf0, xbuf1, ybuf0, ybuf1, zbuf, win_f32, wout_f32, win_bf, wout_bf, gsem, ssem, zsem, wsem,
                *, layer):
    b = pl.program_id(0)
    last = pl.num_programs(0) - 1
    xbufs = (xbuf0, xbuf1)
    ybufs = (ybuf0, ybuf1)

    def zero_tail(tile, act):
        used = used_ref[tile]
        n = TILE_UNITS - used
        for size in _bits_down(SLACK_UNITS):
            @pl.when((n & size) != 0)
            def _():
                start = n & ~(2 * size - 1)
                act(pltpu.make_async_copy(zbuf.at[_unit_rows(0, size), :],
                                          ys_hbm.at[_unit_rows(tile * TILE_UNITS + used + start, size), :], zsem))

    def gather(block, buf):
        for u in range(BLK_UNITS):
            local = jnp.maximum(map_ref[block * BLK_UNITS + u], 0)
            pltpu.make_async_copy(xs_hbm.at[_unit_rows(local, 1), :], xbufs[buf].at[_unit_rows(u, 1), :],
                                  gsem.at[buf]).start()

    def scatter(block, buf):
        trash = N_TILES * TILE_UNITS + buf * BLK_UNITS
        for u in range(BLK_UNITS):
            local = map_ref[block * BLK_UNITS + u]
            local = jnp.where(local >= 0, local, trash + u)
            pltpu.make_async_copy(ybufs[buf].at[_unit_rows(u, 1), :], ys_hbm.at[_unit_rows(local, 1), :],
                                  ssem.at[buf]).start()

    def wait_rows(buffer, sem):
        pltpu.make_async_copy(buffer, buffer, sem).wait()

    def weight_copies(e):
        return (pltpu.make_async_copy(win_hbm.at[layer, e], win_f32, wsem.at[0]),
                pltpu.make_async_copy(wout_hbm.at[layer, e], wout_f32, wsem.at[1]))

    @pl.when(b == 0)
    def _():
        for copy in weight_copies(exp_ref[0]):
            copy.start(priority=1)

    new_expert = jnp.logical_or(b == 0, exp_ref[b] != exp_ref[jnp.maximum(b - 1, 0)])

    @pl.when(new_expert)
    def _():
        e = exp_ref[b]
        for copy in weight_copies(e):
            copy.wait()
        win_bf[...] = win_f32[...].astype(jnp.bfloat16)
        wout_bf[...] = wout_f32[...].astype(jnp.bfloat16)
        upcoming = next_ref[e]

        @pl.when(upcoming >= 0)
        def _():
            for copy in weight_copies(upcoming):
                copy.start(priority=1)

    @pl.when(b == 0)
    def _():
        gather(0, 0)
        ybuf0[...] = jnp.zeros_like(ybuf0)
        ybuf1[...] = jnp.zeros_like(ybuf1)
        zbuf[...] = jnp.zeros_like(zbuf)
        pltpu.make_async_copy(ybuf0, ys_hbm.at[_unit_rows(N_TILES * TILE_UNITS, BLK_UNITS), :], ssem.at[0]).start()
        trash1 = pltpu.make_async_copy(zbuf.at[_unit_rows(0, BLK_UNITS), :],
                                       ys_hbm.at[_unit_rows(N_TILES * TILE_UNITS + BLK_UNITS, BLK_UNITS), :], zsem)
        trash1.start()
        trash1.wait()

    @pl.when(jnp.logical_and(b >= 1, b <= N_TILES))
    def _():
        zero_tail(b - 1, lambda copy: copy.wait())

    @pl.when(b < N_TILES)
    def _():
        zero_tail(b, lambda copy: copy.start())

    def step(cur):
        nxt = 1 - cur
        wait_rows(xbufs[cur], gsem.at[cur])
        wait_rows(ybufs[cur], ssem.at[cur])
        gather(jnp.minimum(b + 1, last), nxt)
        scatter(jnp.maximum(b - 1, 0), nxt)

        def mlp(rows):
            x = xbufs[cur][0:rows, :]
            u = jnp.dot(x, win_bf[...], preferred_element_type=jnp.float32) + bin_ref[...]
            glu = jnp.minimum(u[:, :D_EXPERT], SWIGLU_LIMIT)
            lin = jnp.clip(u[:, D_EXPERT:], -SWIGLU_LIMIT, SWIGLU_LIMIT)
            act = glu * jax.nn.sigmoid(SWIGLU_ALPHA * glu) * (lin + 1.0)
            y = jnp.dot(act.astype(jnp.bfloat16), wout_bf[...], preferred_element_type=jnp.float32) + bout_ref[...]
            ybufs[cur][0:rows, :] = y.astype(jnp.bfloat16)

        held = held_ref[b]
        pl.when(held > BLK_UNITS // 2)(functools.partial(mlp, BM))
        pl.when(jnp.logical_and(held > 0, held <= BLK_UNITS // 2))(functools.partial(mlp, BM // 2))

        @pl.when(b == last)
        def _():
            scatter(b, cur)
            wait_rows(ybufs[nxt], ssem.at[nxt])
            wait_rows(ybufs[cur], ssem.at[cur])
            wait_rows(xbufs[nxt], gsem.at[nxt])

    for parity in range(2):
        pl.when(b % 2 == parity)(functools.partial(step, parity))


def _moe_experts(xs_local, route, w_in, b_in, w_out, b_out, layer):
    by_expert = lambda b, ex, *_: (layer, ex[b], 0, 0)
    tables = [route[k] for k in ("blk_exp", "unit_map", "held", "used", "next_exp")]
    grid_spec = pltpu.PrefetchScalarGridSpec(
        num_scalar_prefetch=len(tables),
        grid=(N_BLKP,),
        in_specs=[
            pl.BlockSpec(memory_space=pl.ANY),
            pl.BlockSpec(memory_space=pl.ANY),
            pl.BlockSpec((None, None, 1, 2 * D_EXPERT), by_expert),
            pl.BlockSpec(memory_space=pl.ANY),
            pl.BlockSpec((None, None, 1, D_MODEL), by_expert),
        ],
        out_specs=pl.BlockSpec(memory_space=pl.ANY),
        scratch_shapes=[
            pltpu.VMEM((BM, D_MODEL), jnp.bfloat16),
            pltpu.VMEM((BM, D_MODEL), jnp.bfloat16),
            pltpu.VMEM((BM, D_MODEL), jnp.bfloat16),
            pltpu.VMEM((BM, D_MODEL), jnp.bfloat16),
            pltpu.VMEM((SLACK_UNITS * SEG, D_MODEL), jnp.bfloat16),
            pltpu.VMEM((D_MODEL, 2 * D_EXPERT), jnp.float32),
            pltpu.VMEM((D_EXPERT, D_MODEL), jnp.float32),
            pltpu.VMEM((D_MODEL, 2 * D_EXPERT), jnp.bfloat16),
            pltpu.VMEM((D_EXPERT, D_MODEL), jnp.bfloat16),
            pltpu.SemaphoreType.DMA((2,)),
            pltpu.SemaphoreType.DMA((2,)),
            pltpu.SemaphoreType.DMA(()),
            pltpu.SemaphoreType.DMA((2,)),
        ],
    )
    return pl.pallas_call(
        functools.partial(_moe_kernel, layer=layer),
        grid_spec=grid_spec,
        out_shape=jax.ShapeDtypeStruct((LOCAL_ROWS + 2 * BM, D_MODEL), jnp.bfloat16),
        compiler_params=_cparams("arbitrary"),
        name="moe_experts",
    )(*tables, xs_local, w_in, b_in.reshape(DEPTH, N_EXPERTS, 1, 2 * D_EXPERT), w_out,
      b_out.reshape(DEPTH, N_EXPERTS, 1, D_MODEL))


def _combine_kernel(ys_ref, pos_ref, gate_ref, x_ref, mod_ref, gfin_ref, o_ref, *, final_norm):
    pos = pos_ref[...]
    gates = gate_ref[...]
    q = lax.broadcasted_iota(jnp.int32, (TM, TILE_ROWS), 1)
    weights = jnp.zeros((TM, TILE_ROWS), jnp.float32)
    for k in range(TOP_K):
        weights = jnp.where(q == pos[:, k:k + 1], gates[:, k:k + 1], weights)
    mixed = jnp.dot(weights.astype(jnp.bfloat16), ys_ref[...], preferred_element_type=jnp.float32)
    x_new = x_ref[...] + mod_ref[5:6, :] * mixed
    if final_norm:
        x_new = x_new * lax.rsqrt(jnp.mean(x_new * x_new, axis=-1, keepdims=True) + NORM_EPS) * gfin_ref[...]
    o_ref[...] = x_new


def _combine(ys_local, pos, gates, x, mod, g_final, layer, *, final_norm):
    return pl.pallas_call(
        functools.partial(_combine_kernel, final_norm=final_norm),
        grid=(N_TILES,),
        in_specs=[
            pl.BlockSpec((TILE_ROWS, D_MODEL), lambda i: (i, 0)),
            pl.BlockSpec((TM, TOP_K), lambda i: (i, 0)),
            pl.BlockSpec((TM, TOP_K), lambda i: (i, 0)),
            pl.BlockSpec((TM, D_MODEL), lambda i: (i, 0)),
            _mod_spec(layer),
            pl.BlockSpec((1, D_MODEL), lambda i: (0, 0)),
        ],
        out_specs=pl.BlockSpec((TM, D_MODEL), lambda i: (i, 0)),
        out_shape=jax.ShapeDtypeStruct((N_TOK, D_MODEL), jnp.float32),
        compiler_params=_cparams("arbitrary"),
        name="moe_combine",
    )(ys_local, pos, gates, x, mod, g_final.reshape(1, D_MODEL))


def kernel(x, c, w_mod, b_mod, norm_mix_g, norm_ffn_g, attn_w_qkv, attn_b_qkv, attn_w_o, attn_b_o,
           attn_sinks, rel_bias_table, conv_w_in, conv_b_in, conv_w_dw, conv_b_dw, conv_ln_g, conv_ln_b,
           conv_w_out, conv_b_out, router_w, router_b, moe_w_in, moe_b_in, moe_w_out, moe_b_out,
           final_norm_g):
    bf16 = jnp.bfloat16
    mod = _modulation(c, w_mod, b_mod)
    bias = _rel_bias(rel_bias_table)
    xf = x.reshape(N_TOK, D_MODEL)
    for i in range(DEPTH):
        j = i // 2
        if i % 2 == 0:
            qkv = _in_proj(xf, mod, norm_mix_g[i], attn_w_qkv[j].astype(bf16), attn_b_qkv[j], i,
                           glu=False, out_dtype=bf16)
            attn = _attention(qkv, bias, attn_sinks, j)
            xf, xs_local, pos, gates, units = _attn_out(attn, attn_w_o[j].astype(bf16), attn_b_o[j], xf, mod,
                                                        norm_ffn_g[i], router_w[i], router_b[i], i)
        else:
            u = _in_proj(xf, mod, norm_mix_g[i], conv_w_in[j].astype(bf16), conv_b_in[j], i,
                         glu=True, out_dtype=jnp.float32)
            xf, xs_local, pos, gates, units = _conv_out(u, conv_w_dw[j], conv_b_dw[j], conv_ln_g[j], conv_ln_b[j],
                                                        conv_w_out[j].astype(bf16), conv_b_out[j], xf, mod,
                                                        norm_ffn_g[i], router_w[i], router_b[i], i)
        route = _routing(units)
        ys_local = _moe_experts(xs_local, route, moe_w_in, moe_b_in, moe_w_out, moe_b_out, i)
        xf = _combine(ys_local, pos, gates, xf, mod, final_norm_g, i, final_norm=(i == DEPTH - 1))
    return xf.reshape(BATCH, SEQ, D_MODEL)
```

```python
import functools
import math

import jax
import jax.numpy as jnp
from jax import lax
from jax.experimental import pallas as pl
from jax.experimental.pallas import tpu as pltpu

D_MODEL = 1024
BATCH = 8
SEQ = 2048
DEPTH = 4
N_TOK = BATCH * SEQ
HEAD_DIM = 64
N_Q_HEADS = 16
N_KV_HEADS = 4
GQA_GROUP = 4
WINDOW = 128
Q_DIM = 1024
KV_DIM = 256
QKV_DIM = 1536
NUM_BUCKETS = 32
MAX_EXACT = 16
MAX_DISTANCE = 128
CONV_WIDTH = 31
N_EXPERTS = 32
TOP_K = 4
D_EXPERT = 1024
SWIGLU_LIMIT = 7.0
SWIGLU_ALPHA = 1.702
NORM_EPS = 1e-5
NEG_INF = -1e30

LANES = 128
SUBLANES = 8
ROW_TILES = D_MODEL // LANES
N_SLOT = N_TOK * TOP_K

TM = 512
TM_ATTN = WINDOW
N_TILES = N_TOK // TM
SEG = 16
TILE_ROWS = -(-(TM * TOP_K + N_EXPERTS * (SEG - 1)) // 256) * 256
TILE_UNITS = TILE_ROWS // SEG
BM = 512
BLK_UNITS = BM // SEG
N_BUF = 3
AHEAD = 2
SLACK_UNITS = TILE_UNITS - TM * TOP_K // SEG


def _bits_down(n):
    return tuple(1 << i for i in reversed(range(n.bit_length())))
N_BLKP = N_TILES * TILE_ROWS // BM + N_EXPERTS
LOCAL_ROWS = N_TILES * TILE_ROWS
CONV_HALO = 32
CONV_ROWS = 64

VMEM_LIMIT = 56 * 1024 * 1024


def _cparams(*sem):
    return pltpu.CompilerParams(dimension_semantics=sem, vmem_limit_bytes=VMEM_LIMIT)


def _mod_kernel(c_ref, w_ref, b_ref, o_ref):
    c = c_ref[...]
    c_act = (c * jax.nn.sigmoid(c)).astype(jnp.bfloat16)
    o_ref[...] = jnp.dot(c_act, w_ref[...].astype(jnp.bfloat16),
                         preferred_element_type=jnp.float32) + b_ref[...]


def _modulation(c, w_mod, b_mod):
    out = pl.pallas_call(
        _mod_kernel,
        grid=(DEPTH, 6),
        in_specs=[
            pl.BlockSpec((BATCH, D_MODEL), lambda i, j: (0, 0)),
            pl.BlockSpec((None, D_MODEL, D_MODEL), lambda i, j: (i, 0, j)),
            pl.BlockSpec((None, None, 1, D_MODEL), lambda i, j: (i, j, 0, 0)),
        ],
        out_specs=pl.BlockSpec((None, None, BATCH, D_MODEL), lambda i, j: (i, j, 0, 0)),
        out_shape=jax.ShapeDtypeStruct((DEPTH, 6, BATCH, D_MODEL), jnp.float32),
        compiler_params=_cparams("arbitrary", "arbitrary"),
        name="adaln_modulation",
    )(c, w_mod, b_mod.reshape(DEPTH, 6, 1, D_MODEL))
    return jnp.transpose(out, (0, 2, 1, 3))


def _mod_spec(layer):
    return pl.BlockSpec((None, None, 6, D_MODEL), lambda i: (layer, i // (SEQ // TM), 0, 0))


def _rms_modulate(x, g, scale, shift):
    y = x * lax.rsqrt(jnp.mean(x * x, axis=-1, keepdims=True) + NORM_EPS)
    return (y * g) * (1.0 + scale) + shift


def _in_proj_kernel(x_ref, mod_ref, g_ref, w_ref, b_ref, o_ref, *, glu):
    h = _rms_modulate(x_ref[...], g_ref[...], mod_ref[1:2, :], mod_ref[0:1, :])
    u = jnp.dot(h.astype(jnp.bfloat16), w_ref[...], preferred_element_type=jnp.float32) + b_ref[...]
    if glu:
        half = u.shape[-1] // 2
        u = u[:, :half] * jax.nn.sigmoid(u[:, half:])
    o_ref[...] = u.astype(o_ref.dtype)


def _in_proj(x, mod, g, w, b, layer, *, glu, out_dtype):
    n_in = w.shape[-1]
    n_out = n_in // 2 if glu else n_in
    return pl.pallas_call(
        functools.partial(_in_proj_kernel, glu=glu),
        grid=(N_TOK // TM,),
        in_specs=[
            pl.BlockSpec((TM, D_MODEL), lambda i: (i, 0)),
            _mod_spec(layer),
            pl.BlockSpec((1, D_MODEL), lambda i: (0, 0)),
            pl.BlockSpec((D_MODEL, n_in), lambda i: (0, 0)),
            pl.BlockSpec((1, n_in), lambda i: (0, 0)),
        ],
        out_specs=pl.BlockSpec((TM, n_out), lambda i: (i, 0)),
        out_shape=jax.ShapeDtypeStruct((N_TOK, n_out), out_dtype),
        compiler_params=_cparams("arbitrary"),
        name="mixer_in_proj_glu" if glu else "mixer_in_proj",
    )(x, mod, g.reshape(1, D_MODEL), w, b.reshape(1, n_in))


def _band_geometry():
    dist = jnp.arange(WINDOW)[:, None] + WINDOW - jnp.arange(2 * WINDOW)[None, :]
    n = jnp.maximum(dist, 0)
    nf = jnp.maximum(n, 1).astype(jnp.float32)
    large = MAX_EXACT + (jnp.log(nf / MAX_EXACT) / math.log(MAX_DISTANCE / MAX_EXACT)
                         * (NUM_BUCKETS - MAX_EXACT)).astype(jnp.int32)
    large = jnp.minimum(large, NUM_BUCKETS - 1)
    bucket = jnp.where(n < MAX_EXACT, n, large)
    return dist, bucket


def _bias_kernel(table_ref, bucket_ref, dist_ref, o_ref):
    h = pl.program_id(0)
    bucket = bucket_ref[...]
    dist = dist_ref[...]
    acc = jnp.zeros(bucket.shape, jnp.float32)
    for b in range(NUM_BUCKETS):
        acc = jnp.where(bucket == b, table_ref[b, h], acc)
    in_window = (dist >= 0) & (dist < WINDOW)
    o_ref[...] = jnp.where(in_window, acc, NEG_INF)


def _rel_bias(table):
    dist, bucket = _band_geometry()
    return pl.pallas_call(
        _bias_kernel,
        grid=(N_Q_HEADS,),
        in_specs=[
            pl.BlockSpec(memory_space=pltpu.SMEM),
            pl.BlockSpec((WINDOW, 2 * WINDOW), lambda h: (0, 0)),
            pl.BlockSpec((WINDOW, 2 * WINDOW), lambda h: (0, 0)),
        ],
        out_specs=pl.BlockSpec((None, WINDOW, 2 * WINDOW), lambda h: (h, 0, 0)),
        out_shape=jax.ShapeDtypeStruct((N_Q_HEADS, WINDOW, 2 * WINDOW), jnp.float32),
        compiler_params=_cparams("arbitrary"),
        name="rel_bias",
    )(table, bucket.astype(jnp.int32), dist.astype(jnp.int32))


def _attn_kernel(sink_ref, q_ref, kv_ref, kvp_ref, bias_ref, o_ref, *, layer):
    i = pl.program_id(0)
    first_block = (i % (SEQ // WINDOW)) == 0
    q = q_ref[...]
    kv = jnp.concatenate([kvp_ref[...], kv_ref[...]], axis=0)
    col = lax.broadcasted_iota(jnp.int32, (WINDOW, 2 * WINDOW), 1)
    prev_ok = jnp.logical_or(col >= WINDOW, jnp.logical_not(first_block))
    outs = []
    for hk in range(N_KV_HEADS):
        k = kv[:, hk * HEAD_DIM:(hk + 1) * HEAD_DIM]
        v = kv[:, KV_DIM + hk * HEAD_DIM:KV_DIM + (hk + 1) * HEAD_DIM]
        for g in range(GQA_GROUP):
            hq = hk * GQA_GROUP + g
            qh = q[:, hq * HEAD_DIM:(hq + 1) * HEAD_DIM] * (HEAD_DIM ** -0.5)
            s = lax.dot_general(qh, k, (((1,), (1,)), ((), ())), preferred_element_type=jnp.float32)
            s = jnp.where(prev_ok, s + bias_ref[hq], NEG_INF)
            sink = sink_ref[layer, hq]
            m = jnp.maximum(jnp.max(s, axis=-1, keepdims=True), sink)
            p = jnp.exp(s - m)
            denom = jnp.sum(p, axis=-1, keepdims=True) + jnp.exp(sink - m)
            o = jnp.dot(p.astype(jnp.bfloat16), v, preferred_element_type=jnp.float32)
            outs.append((o / denom).astype(jnp.bfloat16))
    o_ref[...] = jnp.concatenate(outs, axis=-1)


def _attention(qkv, bias, sinks, layer):
    kv_col = Q_DIM // (2 * KV_DIM)
    return pl.pallas_call(
        functools.partial(_attn_kernel, layer=layer),
        grid=(N_TOK // WINDOW,),
        in_specs=[
            pl.BlockSpec(memory_space=pltpu.SMEM),
            pl.BlockSpec((WINDOW, Q_DIM), lambda i: (i, 0)),
            pl.BlockSpec((WINDOW, 2 * KV_DIM), lambda i: (i, kv_col)),
            pl.BlockSpec((WINDOW, 2 * KV_DIM), lambda i: (jnp.maximum(i - 1, 0), kv_col)),
            pl.BlockSpec((N_Q_HEADS, WINDOW, 2 * WINDOW), lambda i: (0, 0, 0)),
        ],
        out_specs=pl.BlockSpec((WINDOW, Q_DIM), lambda i: (i, 0)),
        out_shape=jax.ShapeDtypeStruct((N_TOK, Q_DIM), jnp.bfloat16),
        compiler_params=_cparams("arbitrary"),
        name="swa_attention",
    )(sinks, qkv, qkv, qkv, bias)


def _split_bf16(a):
    hi = a.astype(jnp.bfloat16)
    lo = (a - hi.astype(jnp.float32)).astype(jnp.bfloat16)
    return hi, lo


def _residual_ffn_router(m, x, mod, g_ffn, wr_hi, wr_lo, b_r, outs):
    xo_ref, xs_ref, pos_ref, gate_ref, units_ref = outs
    x_new = x + mod[2:3, :] * m
    xo_ref[...] = x_new
    h = _rms_modulate(x_new, g_ffn, mod[4:5, :], mod[3:4, :])
    rows = h.shape[0]
    h_hi, h_lo = _split_bf16(h)
    logits = (jnp.dot(h_hi, wr_hi, preferred_element_type=jnp.float32)
              + jnp.dot(h_hi, wr_lo, preferred_element_type=jnp.float32)
              + jnp.dot(h_lo, wr_hi, preferred_element_type=jnp.float32)) + b_r
    lane = lax.broadcasted_iota(jnp.int32, logits.shape, 1)
    vals, ids = [], []
    for _ in range(TOP_K):
        best = jnp.max(logits, axis=-1, keepdims=True)
        arg = jnp.min(jnp.where(logits == best, lane, N_EXPERTS), axis=-1, keepdims=True)
        vals.append(best)
        ids.append(arg)
        logits = jnp.where(lane == arg, -jnp.inf, logits)
    exps = [jnp.exp(v - vals[0]) for v in vals]
    total = exps[0] + exps[1] + exps[2] + exps[3]
    gate_ref[...] = jnp.concatenate([e / total for e in exps], axis=-1)

    member = jnp.zeros(logits.shape, jnp.float32)
    for arg in ids:
        member = member + (lane == arg).astype(jnp.float32)
    units = jnp.floor((jnp.sum(member, axis=0, keepdims=True) + (SEG - 1)) * (1.0 / SEG))
    units_ref[...] = units
    lower_experts = (lax.broadcasted_iota(jnp.int32, (N_EXPERTS, N_EXPERTS), 0)
                     < lax.broadcasted_iota(jnp.int32, (N_EXPERTS, N_EXPERTS), 1)).astype(jnp.bfloat16)
    seg_start = SEG * jnp.dot(jnp.broadcast_to(units, (SUBLANES, N_EXPERTS)).astype(jnp.bfloat16), lower_experts,
                              preferred_element_type=jnp.float32)[0:1, :]
    earlier = (lax.broadcasted_iota(jnp.int32, (rows, rows), 0)
               > lax.broadcasted_iota(jnp.int32, (rows, rows), 1)).astype(jnp.bfloat16)
    place = seg_start + jnp.dot(earlier, member.astype(jnp.bfloat16), preferred_element_type=jnp.float32)
    pos = [jnp.sum(jnp.where(lane == arg, place, 0.0), axis=-1, keepdims=True).astype(jnp.int32) for arg in ids]
    pos_ref[...] = jnp.concatenate(pos, axis=-1)

    q = lax.broadcasted_iota(jnp.int32, (rows, TILE_ROWS), 1)
    hit = (q == pos[0]) | (q == pos[1]) | (q == pos[2]) | (q == pos[3])
    onehot = jnp.where(hit, 1.0, 0.0).astype(jnp.bfloat16)
    h_bf = h.astype(jnp.bfloat16)
    half = D_MODEL // 2
    for c in range(2):
        xs = lax.dot_general(onehot, h_bf[:, c * half:(c + 1) * half], (((0,), (0,)), ((), ())),
                             preferred_element_type=jnp.float32)
        xs_ref[:, c * half:(c + 1) * half] = pltpu.bitcast(xs.astype(jnp.bfloat16), jnp.uint32)


N_EPILOGUE_OUTS = 5


def _attn_out_kernel(a_ref, wo_ref, bo_ref, x_ref, mod_ref, g_ref, wrh_ref, wrl_ref, br_ref, *outs):
    m = jnp.dot(a_ref[...], wo_ref[...], preferred_element_type=jnp.float32) + bo_ref[...]
    _residual_ffn_router(m, x_ref[...], mod_ref[...], g_ref[...], wrh_ref[...], wrl_ref[...], br_ref[...], outs)


def _epilogue_out_specs():
    return [
        pl.BlockSpec((TM, D_MODEL), lambda i: (i, 0)),
        pl.BlockSpec((TILE_ROWS // 2, D_MODEL), lambda i: (i, 0)),
        pl.BlockSpec((TM, TOP_K), lambda i: (i, 0)),
        pl.BlockSpec((TM, TOP_K), lambda i: (i, 0)),
        pl.BlockSpec((None, 1, N_EXPERTS), lambda i: (i, 0, 0)),
    ]


def _epilogue_out_shapes():
    return [
        jax.ShapeDtypeStruct((N_TOK, D_MODEL), jnp.float32),
        jax.ShapeDtypeStruct((LOCAL_ROWS // 2, D_MODEL), jnp.uint32),
        jax.ShapeDtypeStruct((N_TOK, TOP_K), jnp.int32),
        jax.ShapeDtypeStruct((N_TOK, TOP_K), jnp.float32),
        jax.ShapeDtypeStruct((N_TILES, 1, N_EXPERTS), jnp.float32),
    ]


def _router_operands(w_r, b_r):
    hi, lo = _split_bf16(w_r)
    return hi, lo, b_r.reshape(1, N_EXPERTS)


def _router_specs():
    return [
        pl.BlockSpec((D_MODEL, N_EXPERTS), lambda i: (0, 0)),
        pl.BlockSpec((D_MODEL, N_EXPERTS), lambda i: (0, 0)),
        pl.BlockSpec((1, N_EXPERTS), lambda i: (0, 0)),
    ]


def _attn_out(attn, w_o, b_o, x, mod, g_ffn, w_r, b_r, layer):
    return pl.pallas_call(
        _attn_out_kernel,
        grid=(N_TOK // TM,),
        in_specs=[
            pl.BlockSpec((TM, Q_DIM), lambda i: (i, 0)),
            pl.BlockSpec((Q_DIM, D_MODEL), lambda i: (0, 0)),
            pl.BlockSpec((1, D_MODEL), lambda i: (0, 0)),
            pl.BlockSpec((TM, D_MODEL), lambda i: (i, 0)),
            _mod_spec(layer),
            pl.BlockSpec((1, D_MODEL), lambda i: (0, 0)),
        ] + _router_specs(),
        out_specs=_epilogue_out_specs(),
        out_shape=_epilogue_out_shapes(),
        compiler_params=_cparams("arbitrary"),
        name="attn_out_router",
    )(attn, w_o, b_o.reshape(1, D_MODEL), x, mod, g_ffn.reshape(1, D_MODEL), *_router_operands(w_r, b_r))


def _conv_out_kernel(u_ref, up_ref, wdw_ref, bdw_ref, lng_ref, lnb_ref, wo_ref, bo_ref,
                     x_ref, mod_ref, g_ref, wrh_ref, wrl_ref, br_ref, *rest):
    outs = rest[:N_EPILOGUE_OUTS]
    ext_ref, y_ref = rest[N_EPILOGUE_OUTS:]
    i = pl.program_id(0)
    first_tile = (i % (SEQ // TM)) == 0
    ext_ref[0:CONV_HALO, :] = jnp.where(first_tile, 0.0, up_ref[...])
    ext_ref[CONV_HALO:, :] = u_ref[...]
    lead = CONV_HALO - (CONV_WIDTH - 1)
    span = CONV_ROWS + CONV_HALO

    def chunk(r, carry):
        r0 = pl.multiple_of(r * CONV_ROWS, CONV_ROWS)
        for cb in range(ROW_TILES):
            cols = slice(cb * LANES, (cb + 1) * LANES)
            e = ext_ref[pl.ds(r0, span), cols]
            acc = jnp.zeros((CONV_ROWS, LANES), jnp.float32)
            for sub in range(SUBLANES):
                shifted = e if sub == 0 else pltpu.roll(e, span - sub, 0)
                for a in range(span // SUBLANES):
                    tap = a * SUBLANES + sub - lead
                    if 0 <= tap < CONV_WIDTH:
                        acc = acc + wdw_ref[tap:tap + 1, cols] * shifted[a * SUBLANES:a * SUBLANES + CONV_ROWS]
            y_ref[pl.ds(r0, CONV_ROWS), cols] = acc
        return carry

    lax.fori_loop(0, TM // CONV_ROWS, chunk, 0)
    y = y_ref[...] + bdw_ref[...]
    mu = jnp.mean(y, axis=-1, keepdims=True)
    var = jnp.mean(jnp.square(y - mu), axis=-1, keepdims=True)
    z = (y - mu) * lax.rsqrt(var + NORM_EPS) * lng_ref[...] + lnb_ref[...]
    z = z * jax.nn.sigmoid(z)
    m = jnp.dot(z.astype(jnp.bfloat16), wo_ref[...], preferred_element_type=jnp.float32) + bo_ref[...]
    _residual_ffn_router(m, x_ref[...], mod_ref[...], g_ref[...], wrh_ref[...], wrl_ref[...], br_ref[...],
                         outs)


def _conv_out(u, w_dw, b_dw, ln_g, ln_b, w_o, b_o, x, mod, g_ffn, w_r, b_r, layer):
    halo_blocks = TM // CONV_HALO
    row = lambda a: a.reshape(1, D_MODEL)
    return pl.pallas_call(
        _conv_out_kernel,
        grid=(N_TOK // TM,),
        in_specs=[
            pl.BlockSpec((TM, D_MODEL), lambda i: (i, 0)),
            pl.BlockSpec((CONV_HALO, D_MODEL), lambda i: (jnp.maximum(i * halo_blocks - 1, 0), 0)),
            pl.BlockSpec((CONV_WIDTH, D_MODEL), lambda i: (0, 0)),
            pl.BlockSpec((1, D_MODEL), lambda i: (0, 0)),
            pl.BlockSpec((1, D_MODEL), lambda i: (0, 0)),
            pl.BlockSpec((1, D_MODEL), lambda i: (0, 0)),
            pl.BlockSpec((D_MODEL, D_MODEL), lambda i: (0, 0)),
            pl.BlockSpec((1, D_MODEL), lambda i: (0, 0)),
            pl.BlockSpec((TM, D_MODEL), lambda i: (i, 0)),
            _mod_spec(layer),
            pl.BlockSpec((1, D_MODEL), lambda i: (0, 0)),
        ] + _router_specs(),
        out_specs=_epilogue_out_specs(),
        out_shape=_epilogue_out_shapes(),
        scratch_shapes=[
            pltpu.VMEM((TM + CONV_HALO, D_MODEL), jnp.float32),
            pltpu.VMEM((TM, D_MODEL), jnp.float32),
        ],
        compiler_params=_cparams("arbitrary"),
        name="conv_out_router",
    )(u, u, w_dw, row(b_dw), row(ln_g), row(ln_b), w_o, row(b_o), x, mod, row(g_ffn),
      *_router_operands(w_r, b_r))


def _routing(units):
    per_tile = units.reshape(N_TILES, N_EXPERTS).astype(jnp.int32)
    seg_off = jnp.cumsum(per_tile, axis=1) - per_tile
    used = jnp.sum(per_tile, axis=1)
    zero = jnp.zeros((N_EXPERTS, 1), jnp.int32)
    cum = jnp.concatenate([zero, jnp.cumsum(per_tile.T, axis=1)], axis=1)
    total = cum[:, -1]
    n_blocks = (total + BLK_UNITS - 1) // BLK_UNITS
    blk_end = jnp.cumsum(n_blocks)
    blk_start = blk_end - n_blocks
    n_live = blk_end[-1]
    blk = jnp.arange(N_BLKP, dtype=jnp.int32)
    blk_exp = jnp.minimum(jnp.sum(blk[:, None] >= blk_end[None, :], axis=1), N_EXPERTS - 1)
    blk_exp = jnp.where(blk < n_live, blk_exp, blk_exp[n_live - 1]).astype(jnp.int32)
    per_expert = jnp.concatenate([cum, seg_off.T - cum[:, :N_TILES], blk_start[:, None], total[:, None]], axis=1)
    picked = jnp.dot(jax.nn.one_hot(blk_exp, N_EXPERTS, dtype=jnp.float32), per_expert.astype(jnp.float32),
                     precision=lax.Precision.HIGHEST).astype(jnp.int32)
    starts = picked[:, :N_TILES + 1]
    shift = picked[:, N_TILES + 1:2 * N_TILES + 1]
    blk_at = (blk - picked[:, -2]) * BLK_UNITS
    at = blk_at[:, None] + jnp.arange(BLK_UNITS, dtype=jnp.int32)[None, :]
    tile = jnp.minimum(jnp.sum(starts[:, None, 1:] <= at[:, :, None], axis=-1), N_TILES - 1)
    in_tile = tile[:, :, None] == jnp.arange(N_TILES, dtype=jnp.int32)[None, None, :]
    local = tile * TILE_UNITS + at + jnp.sum(jnp.where(in_tile, shift[:, None, :], 0), axis=-1)
    unit_map = jnp.where(at < picked[:, -1:], local, -1)
    experts = jnp.arange(N_EXPERTS, dtype=jnp.int32)
    owners = lax.cummin(jnp.where(n_blocks > 0, experts, N_EXPERTS), reverse=True)
    next_exp = jnp.concatenate([owners[1:], jnp.full((1,), N_EXPERTS, jnp.int32)])
    next_exp = jnp.where(next_exp < N_EXPERTS, next_exp, -1)
    i32 = lambda a: a.astype(jnp.int32)
    held = jnp.clip(picked[:, -1] - blk_at, 0, BLK_UNITS)
    return dict(blk_exp=blk_exp, unit_map=i32(unit_map.reshape(-1)), held=i32(held), used=i32(used),
                next_exp=i32(next_exp))


UNIT_WORDS = SEG // 2


def _as_units(words_2d):
    return words_2d.reshape(words_2d.shape[0] // UNIT_WORDS, UNIT_WORDS, words_2d.shape[1])


def _moe_kernel(exp_ref, map_ref, held_ref, used_ref, next_ref,
                xs_hbm, win_hbm, bin_ref, wout_hbm, bout_ref, ys_hbm,
                xbuf, ybuf, zbuf, win_f32, wout_f32, win_bf, wout_bf, gsem, ssem, zsem, wsem,
                *, layer):
    b = pl.program_id(0)
    last = pl.num_programs(0) - 1

    def zero_tail(tile, act):
        used = used_ref[tile]
        n = TILE_UNITS - used
        for size in _bits_down(SLACK_UNITS):
            @pl.when((n & size) != 0)
            def _():
                start = n & ~(2 * size - 1)
                act(pltpu.make_async_copy(zbuf.at[pl.ds(0, size)],
                                          ys_hbm.at[pl.ds(tile * TILE_UNITS + used + start, size)], zsem))

    def gather(block, buf):
        for u in range(BLK_UNITS):
            local = jnp.maximum(map_ref[block * BLK_UNITS + u], 0)
            pltpu.make_async_copy(xs_hbm.at[local], xbuf.at[buf, u], gsem.at[buf]).start()

    def scatter(block, buf):
        trash = N_TILES * TILE_UNITS + buf * BLK_UNITS
        for u in range(BLK_UNITS):
            local = map_ref[block * BLK_UNITS + u]
            local = jnp.where(local >= 0, local, trash + u)
            pltpu.make_async_copy(ybuf.at[buf, u], ys_hbm.at[local], ssem.at[buf]).start()

    def wait_block(buffers, sem, buf):
        pltpu.make_async_copy(buffers.at[buf], buffers.at[buf], sem.at[buf]).wait()

    def weight_copies(e):
        return (pltpu.make_async_copy(win_hbm.at[layer, e], win_f32, wsem.at[0]),
                pltpu.make_async_copy(wout_hbm.at[layer, e], wout_f32, wsem.at[1]))

    @pl.when(b == 0)
    def _():
        for copy in weight_copies(exp_ref[0]):
            copy.start(priority=1)

    new_expert = jnp.logical_or(b == 0, exp_ref[b] != exp_ref[jnp.maximum(b - 1, 0)])

    @pl.when(new_expert)
    def _():
        e = exp_ref[b]
        for copy in weight_copies(e):
            copy.wait()
        win_bf[...] = win_f32[...].astype(jnp.bfloat16)
        wout_bf[...] = wout_f32[...].astype(jnp.bfloat16)
        upcoming = next_ref[e]

        @pl.when(upcoming >= 0)
        def _():
            for copy in weight_copies(upcoming):
                copy.start(priority=1)

    @pl.when(b == 0)
    def _():
        for ahead in range(AHEAD):
            gather(ahead, ahead)
        ybuf[...] = jnp.zeros_like(ybuf)
        zbuf[...] = jnp.zeros_like(zbuf)
        for piece in range(N_BUF * BLK_UNITS // SLACK_UNITS):
            trash = pltpu.make_async_copy(zbuf.at[pl.ds(0, SLACK_UNITS)],
                                          ys_hbm.at[pl.ds(N_TILES * TILE_UNITS + piece * SLACK_UNITS, SLACK_UNITS)], zsem)
            trash.start()
            trash.wait()

    @pl.when(jnp.logical_and(b >= 1, b <= N_TILES))
    def _():
        zero_tail(b - 1, lambda copy: copy.wait())

    @pl.when(b < N_TILES)
    def _():
        zero_tail(b, lambda copy: copy.start())

    slot = b % N_BUF
    wait_block(xbuf, gsem, slot)

    @pl.when(b >= N_BUF)
    def _():
        wait_block(ybuf, ssem, slot)

    gather(jnp.minimum(b + AHEAD, last), (b + AHEAD) % N_BUF)

    @pl.when(b >= 1)
    def _():
        scatter(b - 1, (b - 1) % N_BUF)

    def mlp(rows):
        x = pltpu.bitcast(xbuf[slot, 0:rows // SEG].reshape(rows // 2, D_MODEL), jnp.bfloat16)
        u = jnp.dot(x, win_bf[...], preferred_element_type=jnp.float32) + bin_ref[...]
        glu = jnp.minimum(u[:, :D_EXPERT], SWIGLU_LIMIT)
        lin = jnp.clip(u[:, D_EXPERT:], -SWIGLU_LIMIT, SWIGLU_LIMIT)
        act = glu * jax.nn.sigmoid(SWIGLU_ALPHA * glu) * (lin + 1.0)
        y = jnp.dot(act.astype(jnp.bfloat16), wout_bf[...], preferred_element_type=jnp.float32) + bout_ref[...]
        ybuf[slot, 0:rows // SEG] = _as_units(pltpu.bitcast(y.astype(jnp.bfloat16), jnp.uint32))

    held = held_ref[b]
    pl.when(held > BLK_UNITS // 2)(functools.partial(mlp, BM))
    pl.when(jnp.logical_and(held > 0, held <= BLK_UNITS // 2))(functools.partial(mlp, BM // 2))

    @pl.when(b == last)
    def _():
        scatter(b, slot)
        for k in range(N_BUF):
            wait_block(ybuf, ssem, (b + k) % N_BUF)
        for k in range(1, AHEAD + 1):
            wait_block(xbuf, gsem, (b + k) % N_BUF)


def _moe_experts(xs_local, route, w_in, b_in, w_out, b_out, layer):
    by_expert = lambda b, ex, *_: (layer, ex[b], 0, 0)
    tables = [route[k] for k in ("blk_exp", "unit_map", "held", "used", "next_exp")]
    grid_spec = pltpu.PrefetchScalarGridSpec(
        num_scalar_prefetch=len(tables),
        grid=(N_BLKP,),
        in_specs=[
            pl.BlockSpec(memory_space=pl.ANY),
            pl.BlockSpec(memory_space=pl.ANY),
            pl.BlockSpec((None, None, 1, 2 * D_EXPERT), by_expert),
            pl.BlockSpec(memory_space=pl.ANY),
            pl.BlockSpec((None, None, 1, D_MODEL), by_expert),
        ],
        out_specs=pl.BlockSpec(memory_space=pl.ANY),
        scratch_shapes=[
            pltpu.VMEM((N_BUF, BLK_UNITS, UNIT_WORDS, D_MODEL), jnp.uint32),
            pltpu.VMEM((N_BUF, BLK_UNITS, UNIT_WORDS, D_MODEL), jnp.uint32),
            pltpu.VMEM((SLACK_UNITS, UNIT_WORDS, D_MODEL), jnp.uint32),
            pltpu.VMEM((D_MODEL, 2 * D_EXPERT), jnp.float32),
            pltpu.VMEM((D_EXPERT, D_MODEL), jnp.float32),
            pltpu.VMEM((D_MODEL, 2 * D_EXPERT), jnp.bfloat16),
            pltpu.VMEM((D_EXPERT, D_MODEL), jnp.bfloat16),
            pltpu.SemaphoreType.DMA((N_BUF,)),
            pltpu.SemaphoreType.DMA((N_BUF,)),
            pltpu.SemaphoreType.DMA(()),
            pltpu.SemaphoreType.DMA((2,)),
        ],
    )
    ys_units = pl.pallas_call(
        functools.partial(_moe_kernel, layer=layer),
        grid_spec=grid_spec,
        out_shape=jax.ShapeDtypeStruct((N_TILES * TILE_UNITS + N_BUF * BLK_UNITS, UNIT_WORDS, D_MODEL), jnp.uint32),
        compiler_params=_cparams("arbitrary"),
        name="moe_experts",
    )(*tables, _as_units(xs_local), w_in, b_in.reshape(DEPTH, N_EXPERTS, 1, 2 * D_EXPERT), w_out,
      b_out.reshape(DEPTH, N_EXPERTS, 1, D_MODEL))
    return ys_units.reshape((LOCAL_ROWS + N_BUF * BM) // 2, D_MODEL)


def _combine_kernel(ys_ref, pos_ref, gate_ref, x_ref, mod_ref, gfin_ref, o_ref, *, final_norm):
    pos = pos_ref[...]
    gates = gate_ref[...]
    q = lax.broadcasted_iota(jnp.int32, (TM, TILE_ROWS), 1)
    weights = jnp.zeros((TM, TILE_ROWS), jnp.float32)
    for k in range(TOP_K):
        weights = jnp.where(q == pos[:, k:k + 1], gates[:, k:k + 1], weights)
    ys = pltpu.bitcast(ys_ref[...], jnp.bfloat16)
    mixed = jnp.dot(weights.astype(jnp.bfloat16), ys, preferred_element_type=jnp.float32)
    x_new = x_ref[...] + mod_ref[5:6, :] * mixed
    if final_norm:
        x_new = x_new * lax.rsqrt(jnp.mean(x_new * x_new, axis=-1, keepdims=True) + NORM_EPS) * gfin_ref[...]
    o_ref[...] = x_new


def _combine(ys_local, pos, gates, x, mod, g_final, layer, *, final_norm):
    return pl.pallas_call(
        functools.partial(_combine_kernel, final_norm=final_norm),
        grid=(N_TILES,),
        in_specs=[
            pl.BlockSpec((TILE_ROWS // 2, D_MODEL), lambda i: (i, 0)),
            pl.BlockSpec((TM, TOP_K), lambda i: (i, 0)),
            pl.BlockSpec((TM, TOP_K), lambda i: (i, 0)),
            pl.BlockSpec((TM, D_MODEL), lambda i: (i, 0)),
            _mod_spec(layer),
            pl.BlockSpec((1, D_MODEL), lambda i: (0, 0)),
        ],
        out_specs=pl.BlockSpec((TM, D_MODEL), lambda i: (i, 0)),
        out_shape=jax.ShapeDtypeStruct((N_TOK, D_MODEL), jnp.float32),
        compiler_params=_cparams("arbitrary"),
        name="moe_combine",
    )(ys_local, pos, gates, x, mod, g_final.reshape(1, D_MODEL))


def kernel(x, c, w_mod, b_mod, norm_mix_g, norm_ffn_g, attn_w_qkv, attn_b_qkv, attn_w_o, attn_b_o,
           attn_sinks, rel_bias_table, conv_w_in, conv_b_in, conv_w_dw, conv_b_dw, conv_ln_g, conv_ln_b,
           conv_w_out, conv_b_out, router_w, router_b, moe_w_in, moe_b_in, moe_w_out, moe_b_out,
           final_norm_g):
    bf16 = jnp.bfloat16
    mod = _modulation(c, w_mod, b_mod)
    bias = _rel_bias(rel_bias_table)
    xf = x.reshape(N_TOK, D_MODEL)
    for i in range(DEPTH):
        j = i // 2
        if i % 2 == 0:
            qkv = _in_proj(xf, mod, norm_mix_g[i], attn_w_qkv[j].astype(bf16), attn_b_qkv[j], i,
                           glu=False, out_dtype=bf16)
            attn = _attention(qkv, bias, attn_sinks, j)
            xf, xs_local, pos, gates, units = _attn_out(attn, attn_w_o[j].astype(bf16), attn_b_o[j], xf, mod,
                                                        norm_ffn_g[i], router_w[i], router_b[i], i)
        else:
            u = _in_proj(xf, mod, norm_mix_g[i], conv_w_in[j].astype(bf16), conv_b_in[j], i,
                         glu=True, out_dtype=jnp.float32)
            xf, xs_local, pos, gates, units = _conv_out(u, conv_w_dw[j], conv_b_dw[j], conv_ln_g[j], conv_ln_b[j],
                                                        conv_w_out[j].astype(bf16), conv_b_out[j], xf, mod,
                                                        norm_ffn_g[i], router_w[i], router_b[i], i)
        route = _routing(units)
        ys_local = _moe_experts(xs_local, route, moe_w_in, moe_b_in, moe_w_out, moe_b_out, i)
        xf = _combine(ys_local, pos, gates, xf, mod, final_norm_g, i, final_norm=(i == DEPTH - 1))
    return xf.reshape(BATCH, SEQ, D_MODEL)
```

```python
import functools
import math

import jax
import jax.numpy as jnp
from jax import lax
from jax.experimental import pallas as pl
from jax.experimental.pallas import tpu as pltpu

D_MODEL = 1024
BATCH = 8
SEQ = 2048
DEPTH = 4
N_TOK = BATCH * SEQ
HEAD_DIM = 64
N_Q_HEADS = 16
N_KV_HEADS = 4
GQA_GROUP = 4
WINDOW = 128
Q_DIM = 1024
KV_DIM = 256
QKV_DIM = 1536
NUM_BUCKETS = 32
MAX_EXACT = 16
MAX_DISTANCE = 128
CONV_WIDTH = 31
N_EXPERTS = 32
TOP_K = 4
D_EXPERT = 1024
SWIGLU_LIMIT = 7.0
SWIGLU_ALPHA = 1.702
NORM_EPS = 1e-5
NEG_INF = -1e30

LANES = 128
SUBLANES = 8
ROW_TILES = D_MODEL // LANES
N_SLOT = N_TOK * TOP_K

TM = 512
TM_ATTN = WINDOW
N_TILES = N_TOK // TM
SEG = 16
TILE_ROWS = -(-(TM * TOP_K + N_EXPERTS * (SEG - 1)) // 256) * 256
TILE_UNITS = TILE_ROWS // SEG
BM = 512
BLK_UNITS = BM // SEG
N_BUF = 3
AHEAD = 2
SLACK_UNITS = TILE_UNITS - TM * TOP_K // SEG


def _bits_down(n):
    return tuple(1 << i for i in reversed(range(n.bit_length())))
N_BLKP = N_TILES * TILE_ROWS // BM + N_EXPERTS
LOCAL_ROWS = N_TILES * TILE_ROWS
CONV_HALO = 32
CONV_ROWS = 64

VMEM_LIMIT = 56 * 1024 * 1024


def _cparams(*sem):
    return pltpu.CompilerParams(dimension_semantics=sem, vmem_limit_bytes=VMEM_LIMIT)


def _mod_kernel(c_ref, w_ref, b_ref, o_ref):
    c = c_ref[...]
    c_act = (c * jax.nn.sigmoid(c)).astype(jnp.bfloat16)
    o_ref[...] = jnp.dot(c_act, w_ref[...].astype(jnp.bfloat16),
                         preferred_element_type=jnp.float32) + b_ref[...]


def _modulation(c, w_mod, b_mod):
    out = pl.pallas_call(
        _mod_kernel,
        grid=(DEPTH, 6),
        in_specs=[
            pl.BlockSpec((BATCH, D_MODEL), lambda i, j: (0, 0)),
            pl.BlockSpec((None, D_MODEL, D_MODEL), lambda i, j: (i, 0, j)),
            pl.BlockSpec((None, None, 1, D_MODEL), lambda i, j: (i, j, 0, 0)),
        ],
        out_specs=pl.BlockSpec((None, None, BATCH, D_MODEL), lambda i, j: (i, j, 0, 0)),
        out_shape=jax.ShapeDtypeStruct((DEPTH, 6, BATCH, D_MODEL), jnp.float32),
        compiler_params=_cparams("arbitrary", "arbitrary"),
        name="adaln_modulation",
    )(c, w_mod, b_mod.reshape(DEPTH, 6, 1, D_MODEL))
    return jnp.transpose(out, (0, 2, 1, 3))


def _mod_spec(layer):
    return pl.BlockSpec((None, None, 6, D_MODEL), lambda i: (layer, i // (SEQ // TM), 0, 0))


def _rms_modulate(x, g, scale, shift):
    y = x * lax.rsqrt(jnp.mean(x * x, axis=-1, keepdims=True) + NORM_EPS)
    return (y * g) * (1.0 + scale) + shift


def _in_proj_kernel(x_ref, mod_ref, g_ref, w_ref, b_ref, o_ref, *, glu):
    h = _rms_modulate(x_ref[...], g_ref[...], mod_ref[1:2, :], mod_ref[0:1, :])
    u = jnp.dot(h.astype(jnp.bfloat16), w_ref[...], preferred_element_type=jnp.float32) + b_ref[...]
    if glu:
        half = u.shape[-1] // 2
        u = u[:, :half] * jax.nn.sigmoid(u[:, half:])
    o_ref[...] = u.astype(o_ref.dtype)


def _in_proj(x, mod, g, w, b, layer, *, glu, out_dtype):
    n_in = w.shape[-1]
    n_out = n_in // 2 if glu else n_in
    return pl.pallas_call(
        functools.partial(_in_proj_kernel, glu=glu),
        grid=(N_TOK // TM,),
        in_specs=[
            pl.BlockSpec((TM, D_MODEL), lambda i: (i, 0)),
            _mod_spec(layer),
            pl.BlockSpec((1, D_MODEL), lambda i: (0, 0)),
            pl.BlockSpec((D_MODEL, n_in), lambda i: (0, 0)),
            pl.BlockSpec((1, n_in), lambda i: (0, 0)),
        ],
        out_specs=pl.BlockSpec((TM, n_out), lambda i: (i, 0)),
        out_shape=jax.ShapeDtypeStruct((N_TOK, n_out), out_dtype),
        compiler_params=_cparams("arbitrary"),
        name="mixer_in_proj_glu" if glu else "mixer_in_proj",
    )(x, mod, g.reshape(1, D_MODEL), w, b.reshape(1, n_in))


def _band_geometry():
    dist = jnp.arange(WINDOW)[:, None] + WINDOW - jnp.arange(2 * WINDOW)[None, :]
    n = jnp.maximum(dist, 0)
    nf = jnp.maximum(n, 1).astype(jnp.float32)
    large = MAX_EXACT + (jnp.log(nf / MAX_EXACT) / math.log(MAX_DISTANCE / MAX_EXACT)
                         * (NUM_BUCKETS - MAX_EXACT)).astype(jnp.int32)
    large = jnp.minimum(large, NUM_BUCKETS - 1)
    bucket = jnp.where(n < MAX_EXACT, n, large)
    return dist, bucket


def _bias_kernel(table_ref, bucket_ref, dist_ref, o_ref):
    h = pl.program_id(0)
    bucket = bucket_ref[...]
    dist = dist_ref[...]
    acc = jnp.zeros(bucket.shape, jnp.float32)
    for b in range(NUM_BUCKETS):
        acc = jnp.where(bucket == b, table_ref[b, h], acc)
    in_window = (dist >= 0) & (dist < WINDOW)
    o_ref[...] = jnp.where(in_window, acc, NEG_INF)


def _rel_bias(table):
    dist, bucket = _band_geometry()
    return pl.pallas_call(
        _bias_kernel,
        grid=(N_Q_HEADS,),
        in_specs=[
            pl.BlockSpec(memory_space=pltpu.SMEM),
            pl.BlockSpec((WINDOW, 2 * WINDOW), lambda h: (0, 0)),
            pl.BlockSpec((WINDOW, 2 * WINDOW), lambda h: (0, 0)),
        ],
        out_specs=pl.BlockSpec((None, WINDOW, 2 * WINDOW), lambda h: (h, 0, 0)),
        out_shape=jax.ShapeDtypeStruct((N_Q_HEADS, WINDOW, 2 * WINDOW), jnp.float32),
        compiler_params=_cparams("arbitrary"),
        name="rel_bias",
    )(table, bucket.astype(jnp.int32), dist.astype(jnp.int32))


def _attn_kernel(sink_ref, q_ref, kv_ref, kvp_ref, bias_ref, o_ref, *, layer):
    i = pl.program_id(0)
    first_block = (i % (SEQ // WINDOW)) == 0
    q = q_ref[...]
    kv = jnp.concatenate([kvp_ref[...], kv_ref[...]], axis=0)
    col = lax.broadcasted_iota(jnp.int32, (WINDOW, 2 * WINDOW), 1)
    prev_ok = jnp.logical_or(col >= WINDOW, jnp.logical_not(first_block))
    outs = []
    for hk in range(N_KV_HEADS):
        k = kv[:, hk * HEAD_DIM:(hk + 1) * HEAD_DIM]
        v = kv[:, KV_DIM + hk * HEAD_DIM:KV_DIM + (hk + 1) * HEAD_DIM]
        for g in range(GQA_GROUP):
            hq = hk * GQA_GROUP + g
            qh = q[:, hq * HEAD_DIM:(hq + 1) * HEAD_DIM] * (HEAD_DIM ** -0.5)
            s = lax.dot_general(qh, k, (((1,), (1,)), ((), ())), preferred_element_type=jnp.float32)
            s = jnp.where(prev_ok, s + bias_ref[hq], NEG_INF)
            sink = sink_ref[layer, hq]
            m = jnp.maximum(jnp.max(s, axis=-1, keepdims=True), sink)
            p = jnp.exp(s - m)
            denom = jnp.sum(p, axis=-1, keepdims=True) + jnp.exp(sink - m)
            o = jnp.dot(p.astype(jnp.bfloat16), v, preferred_element_type=jnp.float32)
            outs.append((o / denom).astype(jnp.bfloat16))
    o_ref[...] = jnp.concatenate(outs, axis=-1)


def _attention(qkv, bias, sinks, layer):
    kv_col = Q_DIM // (2 * KV_DIM)
    return pl.pallas_call(
        functools.partial(_attn_kernel, layer=layer),
        grid=(N_TOK // WINDOW,),
        in_specs=[
            pl.BlockSpec(memory_space=pltpu.SMEM),
            pl.BlockSpec((WINDOW, Q_DIM), lambda i: (i, 0)),
            pl.BlockSpec((WINDOW, 2 * KV_DIM), lambda i: (i, kv_col)),
            pl.BlockSpec((WINDOW, 2 * KV_DIM), lambda i: (jnp.maximum(i - 1, 0), kv_col)),
            pl.BlockSpec((N_Q_HEADS, WINDOW, 2 * WINDOW), lambda i: (0, 0, 0)),
        ],
        out_specs=pl.BlockSpec((WINDOW, Q_DIM), lambda i: (i, 0)),
        out_shape=jax.ShapeDtypeStruct((N_TOK, Q_DIM), jnp.bfloat16),
        compiler_params=_cparams("arbitrary"),
        name="swa_attention",
    )(sinks, qkv, qkv, qkv, bias)


def _split_bf16(a):
    hi = a.astype(jnp.bfloat16)
    lo = (a - hi.astype(jnp.float32)).astype(jnp.bfloat16)
    return hi, lo


def _residual_ffn_router(m, x, mod, g_ffn, wr_hi, wr_lo, b_r, outs):
    xo_ref, xs_ref, pos_ref, gate_ref, units_ref = outs
    x_new = x + mod[2:3, :] * m
    xo_ref[...] = x_new
    h = _rms_modulate(x_new, g_ffn, mod[4:5, :], mod[3:4, :])
    rows = h.shape[0]
    h_hi, h_lo = _split_bf16(h)
    logits = (jnp.dot(h_hi, wr_hi, preferred_element_type=jnp.float32)
              + jnp.dot(h_hi, wr_lo, preferred_element_type=jnp.float32)
              + jnp.dot(h_lo, wr_hi, preferred_element_type=jnp.float32)) + b_r
    lane = lax.broadcasted_iota(jnp.int32, logits.shape, 1)
    vals, ids = [], []
    for _ in range(TOP_K):
        best = jnp.max(logits, axis=-1, keepdims=True)
        arg = jnp.min(jnp.where(logits == best, lane, N_EXPERTS), axis=-1, keepdims=True)
        vals.append(best)
        ids.append(arg)
        logits = jnp.where(lane == arg, -jnp.inf, logits)
    exps = [jnp.exp(v - vals[0]) for v in vals]
    total = exps[0] + exps[1] + exps[2] + exps[3]
    gate_ref[...] = jnp.concatenate([e / total for e in exps], axis=-1)

    member = jnp.zeros(logits.shape, jnp.float32)
    for arg in ids:
        member = member + (lane == arg).astype(jnp.float32)
    units = jnp.floor((jnp.sum(member, axis=0, keepdims=True) + (SEG - 1)) * (1.0 / SEG))
    units_ref[...] = units
    lower_experts = (lax.broadcasted_iota(jnp.int32, (N_EXPERTS, N_EXPERTS), 0)
                     < lax.broadcasted_iota(jnp.int32, (N_EXPERTS, N_EXPERTS), 1)).astype(jnp.bfloat16)
    seg_start = SEG * jnp.dot(jnp.broadcast_to(units, (SUBLANES, N_EXPERTS)).astype(jnp.bfloat16), lower_experts,
                              preferred_element_type=jnp.float32)[0:1, :]
    earlier = (lax.broadcasted_iota(jnp.int32, (rows, rows), 0)
               > lax.broadcasted_iota(jnp.int32, (rows, rows), 1)).astype(jnp.bfloat16)
    place = seg_start + jnp.dot(earlier, member.astype(jnp.bfloat16), preferred_element_type=jnp.float32)
    pos = [jnp.sum(jnp.where(lane == arg, place, 0.0), axis=-1, keepdims=True).astype(jnp.int32) for arg in ids]
    pos_ref[...] = jnp.concatenate(pos, axis=-1)

    q = lax.broadcasted_iota(jnp.int32, (rows, TILE_ROWS), 1).astype(jnp.int16)
    p16 = [p.astype(jnp.int16) for p in pos]
    hit = (q == p16[0]) | (q == p16[1]) | (q == p16[2]) | (q == p16[3])
    onehot = jnp.where(hit, jnp.ones((), jnp.bfloat16), jnp.zeros((), jnp.bfloat16))
    h_bf = h.astype(jnp.bfloat16)
    half = D_MODEL // 2
    for c in range(2):
        xs = lax.dot_general(onehot, h_bf[:, c * half:(c + 1) * half], (((0,), (0,)), ((), ())),
                             preferred_element_type=jnp.float32)
        xs_ref[:, c * half:(c + 1) * half] = pltpu.bitcast(xs.astype(jnp.bfloat16), jnp.uint32)


N_EPILOGUE_OUTS = 5


def _attn_out_kernel(a_ref, wo_ref, bo_ref, x_ref, mod_ref, g_ref, wrh_ref, wrl_ref, br_ref, *outs):
    m = jnp.dot(a_ref[...], wo_ref[...], preferred_element_type=jnp.float32) + bo_ref[...]
    _residual_ffn_router(m, x_ref[...], mod_ref[...], g_ref[...], wrh_ref[...], wrl_ref[...], br_ref[...], outs)


def _epilogue_out_specs():
    return [
        pl.BlockSpec((TM, D_MODEL), lambda i: (i, 0)),
        pl.BlockSpec((TILE_ROWS // 2, D_MODEL), lambda i: (i, 0)),
        pl.BlockSpec((TM, TOP_K), lambda i: (i, 0)),
        pl.BlockSpec((TM, TOP_K), lambda i: (i, 0)),
        pl.BlockSpec((None, 1, N_EXPERTS), lambda i: (i, 0, 0)),
    ]


def _epilogue_out_shapes():
    return [
        jax.ShapeDtypeStruct((N_TOK, D_MODEL), jnp.float32),
        jax.ShapeDtypeStruct((LOCAL_ROWS // 2, D_MODEL), jnp.uint32),
        jax.ShapeDtypeStruct((N_TOK, TOP_K), jnp.int32),
        jax.ShapeDtypeStruct((N_TOK, TOP_K), jnp.float32),
        jax.ShapeDtypeStruct((N_TILES, 1, N_EXPERTS), jnp.float32),
    ]


def _router_operands(w_r, b_r):
    hi, lo = _split_bf16(w_r)
    return hi, lo, b_r.reshape(1, N_EXPERTS)


def _router_specs():
    return [
        pl.BlockSpec((D_MODEL, N_EXPERTS), lambda i: (0, 0)),
        pl.BlockSpec((D_MODEL, N_EXPERTS), lambda i: (0, 0)),
        pl.BlockSpec((1, N_EXPERTS), lambda i: (0, 0)),
    ]


def _attn_out(attn, w_o, b_o, x, mod, g_ffn, w_r, b_r, layer):
    return pl.pallas_call(
        _attn_out_kernel,
        grid=(N_TOK // TM,),
        in_specs=[
            pl.BlockSpec((TM, Q_DIM), lambda i: (i, 0)),
            pl.BlockSpec((Q_DIM, D_MODEL), lambda i: (0, 0)),
            pl.BlockSpec((1, D_MODEL), lambda i: (0, 0)),
            pl.BlockSpec((TM, D_MODEL), lambda i: (i, 0)),
            _mod_spec(layer),
            pl.BlockSpec((1, D_MODEL), lambda i: (0, 0)),
        ] + _router_specs(),
        out_specs=_epilogue_out_specs(),
        out_shape=_epilogue_out_shapes(),
        compiler_params=_cparams("arbitrary"),
        name="attn_out_router",
    )(attn, w_o, b_o.reshape(1, D_MODEL), x, mod, g_ffn.reshape(1, D_MODEL), *_router_operands(w_r, b_r))


def _conv_out_kernel(u_ref, up_ref, wdw_ref, bdw_ref, lng_ref, lnb_ref, wo_ref, bo_ref,
                     x_ref, mod_ref, g_ref, wrh_ref, wrl_ref, br_ref, *rest):
    outs = rest[:N_EPILOGUE_OUTS]
    ext_ref, y_ref = rest[N_EPILOGUE_OUTS:]
    i = pl.program_id(0)
    first_tile = (i % (SEQ // TM)) == 0
    ext_ref[0:CONV_HALO, :] = jnp.where(first_tile, 0.0, up_ref[...])
    ext_ref[CONV_HALO:, :] = u_ref[...]
    lead = CONV_HALO - (CONV_WIDTH - 1)
    span = CONV_ROWS + CONV_HALO

    def chunk(r, carry):
        r0 = pl.multiple_of(r * CONV_ROWS, CONV_ROWS)
        for cb in range(ROW_TILES):
            cols = slice(cb * LANES, (cb + 1) * LANES)
            e = ext_ref[pl.ds(r0, span), cols]
            acc = jnp.zeros((CONV_ROWS, LANES), jnp.float32)
            for sub in range(SUBLANES):
                shifted = e if sub == 0 else pltpu.roll(e, span - sub, 0)
                for a in range(span // SUBLANES):
                    tap = a * SUBLANES + sub - lead
                    if 0 <= tap < CONV_WIDTH:
                        acc = acc + wdw_ref[tap:tap + 1, cols] * shifted[a * SUBLANES:a * SUBLANES + CONV_ROWS]
            y_ref[pl.ds(r0, CONV_ROWS), cols] = acc
        return carry

    lax.fori_loop(0, TM // CONV_ROWS, chunk, 0)
    y = y_ref[...] + bdw_ref[...]
    mu = jnp.mean(y, axis=-1, keepdims=True)
    var = jnp.mean(jnp.square(y - mu), axis=-1, keepdims=True)
    z = (y - mu) * lax.rsqrt(var + NORM_EPS) * lng_ref[...] + lnb_ref[...]
    z = z * jax.nn.sigmoid(z)
    m = jnp.dot(z.astype(jnp.bfloat16), wo_ref[...], preferred_element_type=jnp.float32) + bo_ref[...]
    _residual_ffn_router(m, x_ref[...], mod_ref[...], g_ref[...], wrh_ref[...], wrl_ref[...], br_ref[...],
                         outs)


def _conv_out(u, w_dw, b_dw, ln_g, ln_b, w_o, b_o, x, mod, g_ffn, w_r, b_r, layer):
    halo_blocks = TM // CONV_HALO
    row = lambda a: a.reshape(1, D_MODEL)
    return pl.pallas_call(
        _conv_out_kernel,
        grid=(N_TOK // TM,),
        in_specs=[
            pl.BlockSpec((TM, D_MODEL), lambda i: (i, 0)),
            pl.BlockSpec((CONV_HALO, D_MODEL), lambda i: (jnp.maximum(i * halo_blocks - 1, 0), 0)),
            pl.BlockSpec((CONV_WIDTH, D_MODEL), lambda i: (0, 0)),
            pl.BlockSpec((1, D_MODEL), lambda i: (0, 0)),
            pl.BlockSpec((1, D_MODEL), lambda i: (0, 0)),
            pl.BlockSpec((1, D_MODEL), lambda i: (0, 0)),
            pl.BlockSpec((D_MODEL, D_MODEL), lambda i: (0, 0)),
            pl.BlockSpec((1, D_MODEL), lambda i: (0, 0)),
            pl.BlockSpec((TM, D_MODEL), lambda i: (i, 0)),
            _mod_spec(layer),
            pl.BlockSpec((1, D_MODEL), lambda i: (0, 0)),
        ] + _router_specs(),
        out_specs=_epilogue_out_specs(),
        out_shape=_epilogue_out_shapes(),
        scratch_shapes=[
            pltpu.VMEM((TM + CONV_HALO, D_MODEL), jnp.float32),
            pltpu.VMEM((TM, D_MODEL), jnp.float32),
        ],
        compiler_params=_cparams("arbitrary"),
        name="conv_out_router",
    )(u, u, w_dw, row(b_dw), row(ln_g), row(ln_b), w_o, row(b_o), x, mod, row(g_ffn),
      *_router_operands(w_r, b_r))


def _routing(units):
    per_tile = units.reshape(N_TILES, N_EXPERTS).astype(jnp.int32)
    seg_off = jnp.cumsum(per_tile, axis=1) - per_tile
    used = jnp.sum(per_tile, axis=1)
    zero = jnp.zeros((N_EXPERTS, 1), jnp.int32)
    cum = jnp.concatenate([zero, jnp.cumsum(per_tile.T, axis=1)], axis=1)
    total = cum[:, -1]
    n_blocks = (total + BLK_UNITS - 1) // BLK_UNITS
    blk_end = jnp.cumsum(n_blocks)
    blk_start = blk_end - n_blocks
    n_live = blk_end[-1]
    blk = jnp.arange(N_BLKP, dtype=jnp.int32)
    blk_exp = jnp.minimum(jnp.sum(blk[:, None] >= blk_end[None, :], axis=1), N_EXPERTS - 1)
    blk_exp = jnp.where(blk < n_live, blk_exp, blk_exp[n_live - 1]).astype(jnp.int32)
    per_expert = jnp.concatenate([cum, seg_off.T - cum[:, :N_TILES], blk_start[:, None], total[:, None]], axis=1)
    picked = jnp.dot(jax.nn.one_hot(blk_exp, N_EXPERTS, dtype=jnp.float32), per_expert.astype(jnp.float32),
                     precision=lax.Precision.HIGHEST).astype(jnp.int32)
    starts = picked[:, :N_TILES + 1]
    shift = picked[:, N_TILES + 1:2 * N_TILES + 1]
    blk_at = (blk - picked[:, -2]) * BLK_UNITS
    at = blk_at[:, None] + jnp.arange(BLK_UNITS, dtype=jnp.int32)[None, :]
    tile = jnp.minimum(jnp.sum(starts[:, None, 1:] <= at[:, :, None], axis=-1), N_TILES - 1)
    in_tile = tile[:, :, None] == jnp.arange(N_TILES, dtype=jnp.int32)[None, None, :]
    local = tile * TILE_UNITS + at + jnp.sum(jnp.where(in_tile, shift[:, None, :], 0), axis=-1)
    unit_map = jnp.where(at < picked[:, -1:], local, -1)
    experts = jnp.arange(N_EXPERTS, dtype=jnp.int32)
    owners = lax.cummin(jnp.where(n_blocks > 0, experts, N_EXPERTS), reverse=True)
    next_exp = jnp.concatenate([owners[1:], jnp.full((1,), N_EXPERTS, jnp.int32)])
    next_exp = jnp.where(next_exp < N_EXPERTS, next_exp, -1)
    i32 = lambda a: a.astype(jnp.int32)
    held = jnp.clip(picked[:, -1] - blk_at, 0, BLK_UNITS)
    return dict(blk_exp=blk_exp, unit_map=i32(unit_map.reshape(-1)), held=i32(held), used=i32(used),
                next_exp=i32(next_exp))


UNIT_WORDS = SEG // 2


def _as_units(words_2d):
    return words_2d.reshape(words_2d.shape[0] // UNIT_WORDS, UNIT_WORDS, words_2d.shape[1])


def _moe_kernel(exp_ref, map_ref, held_ref, used_ref, next_ref,
                xs_hbm, win_hbm, bin_ref, wout_hbm, bout_ref, ys_hbm,
                xbuf, ybuf, zbuf, win_f32, wout_f32, win_bf, wout_bf, gsem, ssem, zsem, wsem,
                *, layer):
    b = pl.program_id(0)
    last = pl.num_programs(0) - 1

    def zero_tail(tile, act):
        used = used_ref[tile]
        n = TILE_UNITS - used
        for size in _bits_down(SLACK_UNITS):
            @pl.when((n & size) != 0)
            def _():
                start = n & ~(2 * size - 1)
                act(pltpu.make_async_copy(zbuf.at[pl.ds(0, size)],
                                          ys_hbm.at[pl.ds(tile * TILE_UNITS + used + start, size)], zsem))

    def gather(block, buf):
        for u in range(BLK_UNITS):
            local = jnp.maximum(map_ref[block * BLK_UNITS + u], 0)
            pltpu.make_async_copy(xs_hbm.at[local], xbuf.at[buf, u], gsem.at[buf]).start()

    def scatter(block, buf):
        trash = N_TILES * TILE_UNITS + buf * BLK_UNITS
        for u in range(BLK_UNITS):
            local = map_ref[block * BLK_UNITS + u]
            local = jnp.where(local >= 0, local, trash + u)
            pltpu.make_async_copy(ybuf.at[buf, u], ys_hbm.at[local], ssem.at[buf]).start()

    def wait_block(buffers, sem, buf):
        pltpu.make_async_copy(buffers.at[buf], buffers.at[buf], sem.at[buf]).wait()

    def weight_copies(e):
        return (pltpu.make_async_copy(win_hbm.at[layer, e], win_f32, wsem.at[0]),
                pltpu.make_async_copy(wout_hbm.at[layer, e], wout_f32, wsem.at[1]))

    @pl.when(b == 0)
    def _():
        for copy in weight_copies(exp_ref[0]):
            copy.start(priority=1)

    new_expert = jnp.logical_or(b == 0, exp_ref[b] != exp_ref[jnp.maximum(b - 1, 0)])

    @pl.when(new_expert)
    def _():
        e = exp_ref[b]
        for copy in weight_copies(e):
            copy.wait()
        win_bf[...] = win_f32[...].astype(jnp.bfloat16)
        wout_bf[...] = wout_f32[...].astype(jnp.bfloat16)
        upcoming = next_ref[e]

        @pl.when(upcoming >= 0)
        def _():
            for copy in weight_copies(upcoming):
                copy.start(priority=1)

    @pl.when(b == 0)
    def _():
        for ahead in range(AHEAD):
            gather(ahead, ahead)
        ybuf[...] = jnp.zeros_like(ybuf)
        zbuf[...] = jnp.zeros_like(zbuf)
        for piece in range(N_BUF * BLK_UNITS // SLACK_UNITS):
            trash = pltpu.make_async_copy(zbuf.at[pl.ds(0, SLACK_UNITS)],
                                          ys_hbm.at[pl.ds(N_TILES * TILE_UNITS + piece * SLACK_UNITS, SLACK_UNITS)], zsem)
            trash.start()
            trash.wait()

    @pl.when(jnp.logical_and(b >= 1, b <= N_TILES))
    def _():
        zero_tail(b - 1, lambda copy: copy.wait())

    @pl.when(b < N_TILES)
    def _():
        zero_tail(b, lambda copy: copy.start())

    slot = b % N_BUF
    wait_block(xbuf, gsem, slot)

    @pl.when(b >= N_BUF)
    def _():
        wait_block(ybuf, ssem, slot)

    gather(jnp.minimum(b + AHEAD, last), (b + AHEAD) % N_BUF)

    @pl.when(b >= 1)
    def _():
        scatter(b - 1, (b - 1) % N_BUF)

    def mlp(rows):
        x = pltpu.bitcast(xbuf[slot, 0:rows // SEG].reshape(rows // 2, D_MODEL), jnp.bfloat16)
        u = jnp.dot(x, win_bf[...], preferred_element_type=jnp.float32) + bin_ref[...]
        glu = jnp.minimum(u[:, :D_EXPERT], SWIGLU_LIMIT)
        lin = jnp.clip(u[:, D_EXPERT:], -SWIGLU_LIMIT, SWIGLU_LIMIT)
        act = glu * jax.nn.sigmoid(SWIGLU_ALPHA * glu) * (lin + 1.0)
        y = jnp.dot(act.astype(jnp.bfloat16), wout_bf[...], preferred_element_type=jnp.float32) + bout_ref[...]
        ybuf[slot, 0:rows // SEG] = _as_units(pltpu.bitcast(y.astype(jnp.bfloat16), jnp.uint32))

    held = held_ref[b]
    pl.when(held > BLK_UNITS // 2)(functools.partial(mlp, BM))
    pl.when(jnp.logical_and(held > 0, held <= BLK_UNITS // 2))(functools.partial(mlp, BM // 2))

    @pl.when(b == last)
    def _():
        scatter(b, slot)
        for k in range(N_BUF):
            wait_block(ybuf, ssem, (b + k) % N_BUF)
        for k in range(1, AHEAD + 1):
            wait_block(xbuf, gsem, (b + k) % N_BUF)


def _moe_experts(xs_local, route, w_in, b_in, w_out, b_out, layer):
    by_expert = lambda b, ex, *_: (layer, ex[b], 0, 0)
    tables = [route[k] for k in ("blk_exp", "unit_map", "held", "used", "next_exp")]
    grid_spec = pltpu.PrefetchScalarGridSpec(
        num_scalar_prefetch=len(tables),
        grid=(N_BLKP,),
        in_specs=[
            pl.BlockSpec(memory_space=pl.ANY),
            pl.BlockSpec(memory_space=pl.ANY),
            pl.BlockSpec((None, None, 1, 2 * D_EXPERT), by_expert),
            pl.BlockSpec(memory_space=pl.ANY),
            pl.BlockSpec((None, None, 1, D_MODEL), by_expert),
        ],
        out_specs=pl.BlockSpec(memory_space=pl.ANY),
        scratch_shapes=[
            pltpu.VMEM((N_BUF, BLK_UNITS, UNIT_WORDS, D_MODEL), jnp.uint32),
            pltpu.VMEM((N_BUF, BLK_UNITS, UNIT_WORDS, D_MODEL), jnp.uint32),
            pltpu.VMEM((SLACK_UNITS, UNIT_WORDS, D_MODEL), jnp.uint32),
            pltpu.VMEM((D_MODEL, 2 * D_EXPERT), jnp.float32),
            pltpu.VMEM((D_EXPERT, D_MODEL), jnp.float32),
            pltpu.VMEM((D_MODEL, 2 * D_EXPERT), jnp.bfloat16),
            pltpu.VMEM((D_EXPERT, D_MODEL), jnp.bfloat16),
            pltpu.SemaphoreType.DMA((N_BUF,)),
            pltpu.SemaphoreType.DMA((N_BUF,)),
            pltpu.SemaphoreType.DMA(()),
            pltpu.SemaphoreType.DMA((2,)),
        ],
    )
    ys_units = pl.pallas_call(
        functools.partial(_moe_kernel, layer=layer),
        grid_spec=grid_spec,
        out_shape=jax.ShapeDtypeStruct((N_TILES * TILE_UNITS + N_BUF * BLK_UNITS, UNIT_WORDS, D_MODEL), jnp.uint32),
        compiler_params=_cparams("arbitrary"),
        name="moe_experts",
    )(*tables, _as_units(xs_local), w_in, b_in.reshape(DEPTH, N_EXPERTS, 1, 2 * D_EXPERT), w_out,
      b_out.reshape(DEPTH, N_EXPERTS, 1, D_MODEL))
    return ys_units.reshape((LOCAL_ROWS + N_BUF * BM) // 2, D_MODEL)


def _combine_kernel(ys_ref, pos_ref, gate_ref, x_ref, mod_ref, gfin_ref, o_ref, *, final_norm):
    pos = pos_ref[...].astype(jnp.int16)
    gates = gate_ref[...].astype(jnp.bfloat16)
    q = lax.broadcasted_iota(jnp.int32, (TM, TILE_ROWS), 1).astype(jnp.int16)
    weights = jnp.zeros((TM, TILE_ROWS), jnp.bfloat16)
    for k in range(TOP_K):
        weights = jnp.where(q == pos[:, k:k + 1], gates[:, k:k + 1], weights)
    ys = pltpu.bitcast(ys_ref[...], jnp.bfloat16)
    mixed = jnp.dot(weights, ys, preferred_element_type=jnp.float32)
    x_new = x_ref[...] + mod_ref[5:6, :] * mixed
    if final_norm:
        x_new = x_new * lax.rsqrt(jnp.mean(x_new * x_new, axis=-1, keepdims=True) + NORM_EPS) * gfin_ref[...]
    o_ref[...] = x_new


def _combine(ys_local, pos, gates, x, mod, g_final, layer, *, final_norm):
    return pl.pallas_call(
        functools.partial(_combine_kernel, final_norm=final_norm),
        grid=(N_TILES,),
        in_specs=[
            pl.BlockSpec((TILE_ROWS // 2, D_MODEL), lambda i: (i, 0)),
            pl.BlockSpec((TM, TOP_K), lambda i: (i, 0)),
            pl.BlockSpec((TM, TOP_K), lambda i: (i, 0)),
            pl.BlockSpec((TM, D_MODEL), lambda i: (i, 0)),
            _mod_spec(layer),
            pl.BlockSpec((1, D_MODEL), lambda i: (0, 0)),
        ],
        out_specs=pl.BlockSpec((TM, D_MODEL), lambda i: (i, 0)),
        out_shape=jax.ShapeDtypeStruct((N_TOK, D_MODEL), jnp.float32),
        compiler_params=_cparams("arbitrary"),
        name="moe_combine",
    )(ys_local, pos, gates, x, mod, g_final.reshape(1, D_MODEL))


def kernel(x, c, w_mod, b_mod, norm_mix_g, norm_ffn_g, attn_w_qkv, attn_b_qkv, attn_w_o, attn_b_o,
           attn_sinks, rel_bias_table, conv_w_in, conv_b_in, conv_w_dw, conv_b_dw, conv_ln_g, conv_ln_b,
           conv_w_out, conv_b_out, router_w, router_b, moe_w_in, moe_b_in, moe_w_out, moe_b_out,
           final_norm_g):
    bf16 = jnp.bfloat16
    mod = _modulation(c, w_mod, b_mod)
    bias = _rel_bias(rel_bias_table)
    xf = x.reshape(N_TOK, D_MODEL)
    for i in range(DEPTH):
        j = i // 2
        if i % 2 == 0:
            qkv = _in_proj(xf, mod, norm_mix_g[i], attn_w_qkv[j].astype(bf16), attn_b_qkv[j], i,
                           glu=False, out_dtype=bf16)
            attn = _attention(qkv, bias, attn_sinks, j)
            xf, xs_local, pos, gates, units = _attn_out(attn, attn_w_o[j].astype(bf16), attn_b_o[j], xf, mod,
                                                        norm_ffn_g[i], router_w[i], router_b[i], i)
        else:
            u = _in_proj(xf, mod, norm_mix_g[i], conv_w_in[j].astype(bf16), conv_b_in[j], i,
                         glu=True, out_dtype=jnp.float32)
            xf, xs_local, pos, gates, units = _conv_out(u, conv_w_dw[j], conv_b_dw[j], conv_ln_g[j], conv_ln_b[j],
                                                        conv_w_out[j].astype(bf16), conv_b_out[j], xf, mod,
                                                        norm_ffn_g[i], router_w[i], router_b[i], i)
        route = _routing(units)
        ys_local = _moe_experts(xs_local, route, moe_w_in, moe_b_in, moe_w_out, moe_b_out, i)
        xf = _combine(ys_local, pos, gates, xf, mod, final_norm_g, i, final_norm=(i == DEPTH - 1))
    return xf.reshape(BATCH, SEQ, D_MODEL)
```

```python
import functools
import math

import jax
import jax.numpy as jnp
from jax import lax
from jax.experimental import pallas as pl
from jax.experimental.pallas import tpu as pltpu

D_MODEL = 1024
BATCH = 8
SEQ = 2048
DEPTH = 4
N_TOK = BATCH * SEQ
HEAD_DIM = 64
N_Q_HEADS = 16
N_KV_HEADS = 4
GQA_GROUP = 4
WINDOW = 128
Q_DIM = 1024
KV_DIM = 256
QKV_DIM = 1536
NUM_BUCKETS = 32
MAX_EXACT = 16
MAX_DISTANCE = 128
CONV_WIDTH = 31
N_EXPERTS = 32
TOP_K = 4
D_EXPERT = 1024
SWIGLU_LIMIT = 7.0
SWIGLU_ALPHA = 1.702
NORM_EPS = 1e-5
NEG_INF = -1e30

LANES = 128
SUBLANES = 8
ROW_TILES = D_MODEL // LANES
N_SLOT = N_TOK * TOP_K

TM = 512
TM_ATTN = WINDOW
N_TILES = N_TOK // TM
SEG = 16
TILE_ROWS = -(-(TM * TOP_K + N_EXPERTS * (SEG - 1)) // 256) * 256
TILE_UNITS = TILE_ROWS // SEG
BM = 512
BLK_UNITS = BM // SEG
N_BUF = 3
AHEAD = 2
SLACK_UNITS = TILE_UNITS - TM * TOP_K // SEG


def _bits_down(n):
    return tuple(1 << i for i in reversed(range(n.bit_length())))
N_BLKP = N_TILES * TILE_ROWS // BM + N_EXPERTS
LOCAL_ROWS = N_TILES * TILE_ROWS
CONV_HALO = 32
CONV_ROWS = 64

VMEM_LIMIT = 56 * 1024 * 1024


def _cparams(*sem):
    return pltpu.CompilerParams(dimension_semantics=sem, vmem_limit_bytes=VMEM_LIMIT)


def _mod_kernel(c_ref, w_ref, b_ref, o_ref):
    c = c_ref[...]
    c_act = (c * jax.nn.sigmoid(c)).astype(jnp.bfloat16)
    o_ref[...] = jnp.dot(c_act, w_ref[...].astype(jnp.bfloat16),
                         preferred_element_type=jnp.float32) + b_ref[...]


def _modulation(c, w_mod, b_mod):
    out = pl.pallas_call(
        _mod_kernel,
        grid=(DEPTH, 6),
        in_specs=[
            pl.BlockSpec((BATCH, D_MODEL), lambda i, j: (0, 0)),
            pl.BlockSpec((None, D_MODEL, D_MODEL), lambda i, j: (i, 0, j)),
            pl.BlockSpec((None, None, 1, D_MODEL), lambda i, j: (i, j, 0, 0)),
        ],
        out_specs=pl.BlockSpec((None, None, BATCH, D_MODEL), lambda i, j: (i, j, 0, 0)),
        out_shape=jax.ShapeDtypeStruct((DEPTH, 6, BATCH, D_MODEL), jnp.float32),
        compiler_params=_cparams("arbitrary", "arbitrary"),
        name="adaln_modulation",
    )(c, w_mod, b_mod.reshape(DEPTH, 6, 1, D_MODEL))
    return jnp.transpose(out, (0, 2, 1, 3))


def _mod_spec(layer):
    return pl.BlockSpec((None, None, 6, D_MODEL), lambda i: (layer, i // (SEQ // TM), 0, 0))


def _rms_modulate(x, g, scale, shift):
    y = x * lax.rsqrt(jnp.mean(x * x, axis=-1, keepdims=True) + NORM_EPS)
    return (y * g) * (1.0 + scale) + shift


def _in_proj_kernel(x_ref, mod_ref, g_ref, w_ref, b_ref, o_ref, *, glu):
    h = _rms_modulate(x_ref[...], g_ref[...], mod_ref[1:2, :], mod_ref[0:1, :])
    u = jnp.dot(h.astype(jnp.bfloat16), w_ref[...], preferred_element_type=jnp.float32) + b_ref[...]
    if glu:
        half = u.shape[-1] // 2
        u = u[:, :half] * jax.nn.sigmoid(u[:, half:])
    o_ref[...] = u.astype(o_ref.dtype)


def _in_proj(x, mod, g, w, b, layer, *, glu, out_dtype):
    n_in = w.shape[-1]
    n_out = n_in // 2 if glu else n_in
    return pl.pallas_call(
        functools.partial(_in_proj_kernel, glu=glu),
        grid=(N_TOK // TM,),
        in_specs=[
            pl.BlockSpec((TM, D_MODEL), lambda i: (i, 0)),
            _mod_spec(layer),
            pl.BlockSpec((1, D_MODEL), lambda i: (0, 0)),
            pl.BlockSpec((D_MODEL, n_in), lambda i: (0, 0)),
            pl.BlockSpec((1, n_in), lambda i: (0, 0)),
        ],
        out_specs=pl.BlockSpec((TM, n_out), lambda i: (i, 0)),
        out_shape=jax.ShapeDtypeStruct((N_TOK, n_out), out_dtype),
        compiler_params=_cparams("arbitrary"),
        name="mixer_in_proj_glu" if glu else "mixer_in_proj",
    )(x, mod, g.reshape(1, D_MODEL), w, b.reshape(1, n_in))


def _band_geometry():
    dist = jnp.arange(WINDOW)[:, None] + WINDOW - jnp.arange(2 * WINDOW)[None, :]
    n = jnp.maximum(dist, 0)
    nf = jnp.maximum(n, 1).astype(jnp.float32)
    large = MAX_EXACT + (jnp.log(nf / MAX_EXACT) / math.log(MAX_DISTANCE / MAX_EXACT)
                         * (NUM_BUCKETS - MAX_EXACT)).astype(jnp.int32)
    large = jnp.minimum(large, NUM_BUCKETS - 1)
    bucket = jnp.where(n < MAX_EXACT, n, large)
    return dist, bucket


def _bias_kernel(table_ref, bucket_ref, dist_ref, o_ref):
    first = pl.program_id(0)
    h = pl.program_id(1)
    bucket = bucket_ref[...]
    dist = dist_ref[...]
    acc = jnp.zeros(bucket.shape, jnp.float32)
    for b in range(NUM_BUCKETS):
        acc = jnp.where(bucket == b, table_ref[b, h], acc)
    col = lax.broadcasted_iota(jnp.int32, bucket.shape, 1)
    visible = (dist >= 0) & (dist < WINDOW) & jnp.logical_or(col >= WINDOW, first == 0)
    o_ref[...] = jnp.where(visible, acc, NEG_INF)


def _rel_bias(table):
    dist, bucket = _band_geometry()
    return pl.pallas_call(
        _bias_kernel,
        grid=(2, N_Q_HEADS),
        in_specs=[
            pl.BlockSpec(memory_space=pltpu.SMEM),
            pl.BlockSpec((WINDOW, 2 * WINDOW), lambda f, h: (0, 0)),
            pl.BlockSpec((WINDOW, 2 * WINDOW), lambda f, h: (0, 0)),
        ],
        out_specs=pl.BlockSpec((None, None, WINDOW, 2 * WINDOW), lambda f, h: (f, h, 0, 0)),
        out_shape=jax.ShapeDtypeStruct((2, N_Q_HEADS, WINDOW, 2 * WINDOW), jnp.float32),
        compiler_params=_cparams("arbitrary", "arbitrary"),
        name="rel_bias",
    )(table, bucket.astype(jnp.int32), dist.astype(jnp.int32))


def _attn_kernel(sink_ref, q_ref, kv_ref, kvp_ref, bias_ref, o_ref, *, layer):
    i = pl.program_id(0)
    first_block = ((i % (SEQ // WINDOW)) == 0).astype(jnp.int32)
    q = q_ref[...]
    kv = jnp.concatenate([kvp_ref[...], kv_ref[...]], axis=0)
    outs = []
    for hk in range(N_KV_HEADS):
        k = kv[:, hk * HEAD_DIM:(hk + 1) * HEAD_DIM]
        v = kv[:, KV_DIM + hk * HEAD_DIM:KV_DIM + (hk + 1) * HEAD_DIM]
        for g in range(GQA_GROUP):
            hq = hk * GQA_GROUP + g
            qh = q[:, hq * HEAD_DIM:(hq + 1) * HEAD_DIM] * (HEAD_DIM ** -0.5)
            s = lax.dot_general(qh, k, (((1,), (1,)), ((), ())), preferred_element_type=jnp.float32)
            s = s + bias_ref[first_block, hq]
            sink = sink_ref[layer, hq]
            m = jnp.maximum(jnp.max(s, axis=-1, keepdims=True), sink)
            p = jnp.exp(s - m)
            denom = jnp.sum(p, axis=-1, keepdims=True) + jnp.exp(sink - m)
            o = jnp.dot(p.astype(jnp.bfloat16), v, preferred_element_type=jnp.float32)
            outs.append((o / denom).astype(jnp.bfloat16))
    o_ref[...] = jnp.concatenate(outs, axis=-1)


def _attention(qkv, bias, sinks, layer):
    kv_col = Q_DIM // (2 * KV_DIM)
    return pl.pallas_call(
        functools.partial(_attn_kernel, layer=layer),
        grid=(N_TOK // WINDOW,),
        in_specs=[
            pl.BlockSpec(memory_space=pltpu.SMEM),
            pl.BlockSpec((WINDOW, Q_DIM), lambda i: (i, 0)),
            pl.BlockSpec((WINDOW, 2 * KV_DIM), lambda i: (i, kv_col)),
            pl.BlockSpec((WINDOW, 2 * KV_DIM), lambda i: (jnp.maximum(i - 1, 0), kv_col)),
            pl.BlockSpec((2, N_Q_HEADS, WINDOW, 2 * WINDOW), lambda i: (0, 0, 0, 0)),
        ],
        out_specs=pl.BlockSpec((WINDOW, Q_DIM), lambda i: (i, 0)),
        out_shape=jax.ShapeDtypeStruct((N_TOK, Q_DIM), jnp.bfloat16),
        compiler_params=_cparams("arbitrary"),
        name="swa_attention",
    )(sinks, qkv, qkv, qkv, bias)


def _split_bf16(a):
    hi = a.astype(jnp.bfloat16)
    lo = (a - hi.astype(jnp.float32)).astype(jnp.bfloat16)
    return hi, lo


def _residual_ffn_router(m, x, mod, g_ffn, wr_hi, wr_lo, b_r, outs):
    xo_ref, xs_ref, pos_ref, gate_ref, units_ref = outs
    x_new = x + mod[2:3, :] * m
    xo_ref[...] = x_new
    h = _rms_modulate(x_new, g_ffn, mod[4:5, :], mod[3:4, :])
    rows = h.shape[0]
    h_hi, h_lo = _split_bf16(h)
    logits = (jnp.dot(h_hi, wr_hi, preferred_element_type=jnp.float32)
              + jnp.dot(h_hi, wr_lo, preferred_element_type=jnp.float32)
              + jnp.dot(h_lo, wr_hi, preferred_element_type=jnp.float32)) + b_r
    lane = lax.broadcasted_iota(jnp.int32, logits.shape, 1)
    vals, ids = [], []
    for _ in range(TOP_K):
        best = jnp.max(logits, axis=-1, keepdims=True)
        arg = jnp.min(jnp.where(logits == best, lane, N_EXPERTS), axis=-1, keepdims=True)
        vals.append(best)
        ids.append(arg)
        logits = jnp.where(lane == arg, -jnp.inf, logits)
    exps = [jnp.exp(v - vals[0]) for v in vals]
    total = exps[0] + exps[1] + exps[2] + exps[3]
    gate_ref[...] = jnp.concatenate([e / total for e in exps], axis=-1)

    member = jnp.zeros(logits.shape, jnp.float32)
    for arg in ids:
        member = member + (lane == arg).astype(jnp.float32)
    units = jnp.floor((jnp.sum(member, axis=0, keepdims=True) + (SEG - 1)) * (1.0 / SEG))
    units_ref[...] = units
    lower_experts = (lax.broadcasted_iota(jnp.int32, (N_EXPERTS, N_EXPERTS), 0)
                     < lax.broadcasted_iota(jnp.int32, (N_EXPERTS, N_EXPERTS), 1)).astype(jnp.bfloat16)
    seg_start = SEG * jnp.dot(jnp.broadcast_to(units, (SUBLANES, N_EXPERTS)).astype(jnp.bfloat16), lower_experts,
                              preferred_element_type=jnp.float32)[0:1, :]
    earlier = jnp.where(lax.broadcasted_iota(jnp.int32, (rows, rows), 0).astype(jnp.int16)
                        > lax.broadcasted_iota(jnp.int32, (rows, rows), 1).astype(jnp.int16),
                        jnp.ones((), jnp.bfloat16), jnp.zeros((), jnp.bfloat16))
    place = seg_start + jnp.dot(earlier, member.astype(jnp.bfloat16), preferred_element_type=jnp.float32)
    pos = [jnp.sum(jnp.where(lane == arg, place, 0.0), axis=-1, keepdims=True).astype(jnp.int32) for arg in ids]
    pos_ref[...] = jnp.concatenate(pos, axis=-1)

    column = lax.broadcasted_iota(jnp.int32, (rows, LANES), 1)
    pos_cols = jnp.zeros((rows, LANES), jnp.float32)
    for k in range(TOP_K):
        pos_cols = jnp.where(column == k, pos[k].astype(jnp.float32), pos_cols)
    pos_lanes = pos_cols.T
    p16 = [pos_lanes[k:k + 1, :].astype(jnp.int32).astype(jnp.int16) for k in range(TOP_K)]
    q = lax.broadcasted_iota(jnp.int32, (TILE_ROWS, rows), 0).astype(jnp.int16)
    hit = (q == p16[0]) | (q == p16[1]) | (q == p16[2]) | (q == p16[3])
    onehot = jnp.where(hit, jnp.ones((), jnp.bfloat16), jnp.zeros((), jnp.bfloat16))
    h_bf = h.astype(jnp.bfloat16)
    half = D_MODEL // 2
    for c in range(2):
        xs = jnp.dot(onehot, h_bf[:, c * half:(c + 1) * half], preferred_element_type=jnp.float32)
        xs_ref[:, c * half:(c + 1) * half] = pltpu.bitcast(xs.astype(jnp.bfloat16), jnp.uint32)


N_EPILOGUE_OUTS = 5


def _attn_out_kernel(a_ref, wo_ref, bo_ref, x_ref, mod_ref, g_ref, wrh_ref, wrl_ref, br_ref, *outs):
    m = jnp.dot(a_ref[...], wo_ref[...], preferred_element_type=jnp.float32) + bo_ref[...]
    _residual_ffn_router(m, x_ref[...], mod_ref[...], g_ref[...], wrh_ref[...], wrl_ref[...], br_ref[...], outs)


def _epilogue_out_specs():
    return [
        pl.BlockSpec((TM, D_MODEL), lambda i: (i, 0)),
        pl.BlockSpec((TILE_ROWS // 2, D_MODEL), lambda i: (i, 0)),
        pl.BlockSpec((TM, TOP_K), lambda i: (i, 0)),
        pl.BlockSpec((TM, TOP_K), lambda i: (i, 0)),
        pl.BlockSpec((None, 1, N_EXPERTS), lambda i: (i, 0, 0)),
    ]


def _epilogue_out_shapes():
    return [
        jax.ShapeDtypeStruct((N_TOK, D_MODEL), jnp.float32),
        jax.ShapeDtypeStruct((LOCAL_ROWS // 2, D_MODEL), jnp.uint32),
        jax.ShapeDtypeStruct((N_TOK, TOP_K), jnp.int32),
        jax.ShapeDtypeStruct((N_TOK, TOP_K), jnp.float32),
        jax.ShapeDtypeStruct((N_TILES, 1, N_EXPERTS), jnp.float32),
    ]


def _router_operands(w_r, b_r):
    hi, lo = _split_bf16(w_r)
    return hi, lo, b_r.reshape(1, N_EXPERTS)


def _router_specs():
    return [
        pl.BlockSpec((D_MODEL, N_EXPERTS), lambda i: (0, 0)),
        pl.BlockSpec((D_MODEL, N_EXPERTS), lambda i: (0, 0)),
        pl.BlockSpec((1, N_EXPERTS), lambda i: (0, 0)),
    ]


def _attn_out(attn, w_o, b_o, x, mod, g_ffn, w_r, b_r, layer):
    return pl.pallas_call(
        _attn_out_kernel,
        grid=(N_TOK // TM,),
        in_specs=[
            pl.BlockSpec((TM, Q_DIM), lambda i: (i, 0)),
            pl.BlockSpec((Q_DIM, D_MODEL), lambda i: (0, 0)),
            pl.BlockSpec((1, D_MODEL), lambda i: (0, 0)),
            pl.BlockSpec((TM, D_MODEL), lambda i: (i, 0)),
            _mod_spec(layer),
            pl.BlockSpec((1, D_MODEL), lambda i: (0, 0)),
        ] + _router_specs(),
        out_specs=_epilogue_out_specs(),
        out_shape=_epilogue_out_shapes(),
        compiler_params=_cparams("arbitrary"),
        name="attn_out_router",
    )(attn, w_o, b_o.reshape(1, D_MODEL), x, mod, g_ffn.reshape(1, D_MODEL), *_router_operands(w_r, b_r))


def _conv_out_kernel(u_ref, up_ref, wdw_ref, bdw_ref, lng_ref, lnb_ref, wo_ref, bo_ref,
                     x_ref, mod_ref, g_ref, wrh_ref, wrl_ref, br_ref, *rest):
    outs = rest[:N_EPILOGUE_OUTS]
    ext_ref, y_ref = rest[N_EPILOGUE_OUTS:]
    i = pl.program_id(0)
    first_tile = (i % (SEQ // TM)) == 0
    ext_ref[0:CONV_HALO, :] = jnp.where(first_tile, 0.0, up_ref[...])
    ext_ref[CONV_HALO:, :] = u_ref[...]
    lead = CONV_HALO - (CONV_WIDTH - 1)
    span = CONV_ROWS + CONV_HALO

    def chunk(r, carry):
        r0 = pl.multiple_of(r * CONV_ROWS, CONV_ROWS)
        for cb in range(ROW_TILES):
            cols = slice(cb * LANES, (cb + 1) * LANES)
            e = ext_ref[pl.ds(r0, span), cols]
            acc = jnp.zeros((CONV_ROWS, LANES), jnp.float32)
            for sub in range(SUBLANES):
                shifted = e if sub == 0 else pltpu.roll(e, span - sub, 0)
                for a in range(span // SUBLANES):
                    tap = a * SUBLANES + sub - lead
                    if 0 <= tap < CONV_WIDTH:
                        acc = acc + wdw_ref[tap:tap + 1, cols] * shifted[a * SUBLANES:a * SUBLANES + CONV_ROWS]
            y_ref[pl.ds(r0, CONV_ROWS), cols] = acc
        return carry

    lax.fori_loop(0, TM // CONV_ROWS, chunk, 0)
    y = y_ref[...] + bdw_ref[...]
    mu = jnp.mean(y, axis=-1, keepdims=True)
    var = jnp.mean(jnp.square(y - mu), axis=-1, keepdims=True)
    z = (y - mu) * lax.rsqrt(var + NORM_EPS) * lng_ref[...] + lnb_ref[...]
    z = z * jax.nn.sigmoid(z)
    m = jnp.dot(z.astype(jnp.bfloat16), wo_ref[...], preferred_element_type=jnp.float32) + bo_ref[...]
    _residual_ffn_router(m, x_ref[...], mod_ref[...], g_ref[...], wrh_ref[...], wrl_ref[...], br_ref[...],
                         outs)


def _conv_out(u, w_dw, b_dw, ln_g, ln_b, w_o, b_o, x, mod, g_ffn, w_r, b_r, layer):
    halo_blocks = TM // CONV_HALO
    row = lambda a: a.reshape(1, D_MODEL)
    return pl.pallas_call(
        _conv_out_kernel,
        grid=(N_TOK // TM,),
        in_specs=[
            pl.BlockSpec((TM, D_MODEL), lambda i: (i, 0)),
            pl.BlockSpec((CONV_HALO, D_MODEL), lambda i: (jnp.maximum(i * halo_blocks - 1, 0), 0)),
            pl.BlockSpec((CONV_WIDTH, D_MODEL), lambda i: (0, 0)),
            pl.BlockSpec((1, D_MODEL), lambda i: (0, 0)),
            pl.BlockSpec((1, D_MODEL), lambda i: (0, 0)),
            pl.BlockSpec((1, D_MODEL), lambda i: (0, 0)),
            pl.BlockSpec((D_MODEL, D_MODEL), lambda i: (0, 0)),
            pl.BlockSpec((1, D_MODEL), lambda i: (0, 0)),
            pl.BlockSpec((TM, D_MODEL), lambda i: (i, 0)),
            _mod_spec(layer),
            pl.BlockSpec((1, D_MODEL), lambda i: (0, 0)),
        ] + _router_specs(),
        out_specs=_epilogue_out_specs(),
        out_shape=_epilogue_out_shapes(),
        scratch_shapes=[
            pltpu.VMEM((TM + CONV_HALO, D_MODEL), jnp.float32),
            pltpu.VMEM((TM, D_MODEL), jnp.float32),
        ],
        compiler_params=_cparams("arbitrary"),
        name="conv_out_router",
    )(u, u, w_dw, row(b_dw), row(ln_g), row(ln_b), w_o, row(b_o), x, mod, row(g_ffn),
      *_router_operands(w_r, b_r))


def _routing(units):
    per_tile = units.reshape(N_TILES, N_EXPERTS).astype(jnp.int32)
    seg_off = jnp.cumsum(per_tile, axis=1) - per_tile
    used = jnp.sum(per_tile, axis=1)
    zero = jnp.zeros((N_EXPERTS, 1), jnp.int32)
    cum = jnp.concatenate([zero, jnp.cumsum(per_tile.T, axis=1)], axis=1)
    total = cum[:, -1]
    n_blocks = (total + BLK_UNITS - 1) // BLK_UNITS
    blk_end = jnp.cumsum(n_blocks)
    blk_start = blk_end - n_blocks
    n_live = blk_end[-1]
    blk = jnp.arange(N_BLKP, dtype=jnp.int32)
    blk_exp = jnp.minimum(jnp.sum(blk[:, None] >= blk_end[None, :], axis=1), N_EXPERTS - 1)
    blk_exp = jnp.where(blk < n_live, blk_exp, blk_exp[n_live - 1]).astype(jnp.int32)
    per_expert = jnp.concatenate([cum, seg_off.T - cum[:, :N_TILES], blk_start[:, None], total[:, None]], axis=1)
    picked = jnp.dot(jax.nn.one_hot(blk_exp, N_EXPERTS, dtype=jnp.float32), per_expert.astype(jnp.float32),
                     precision=lax.Precision.HIGHEST).astype(jnp.int32)
    starts = picked[:, :N_TILES + 1]
    shift = picked[:, N_TILES + 1:2 * N_TILES + 1]
    blk_at = (blk - picked[:, -2]) * BLK_UNITS
    at = blk_at[:, None] + jnp.arange(BLK_UNITS, dtype=jnp.int32)[None, :]
    tile = jnp.minimum(jnp.sum(starts[:, None, 1:] <= at[:, :, None], axis=-1), N_TILES - 1)
    in_tile = tile[:, :, None] == jnp.arange(N_TILES, dtype=jnp.int32)[None, None, :]
    local = tile * TILE_UNITS + at + jnp.sum(jnp.where(in_tile, shift[:, None, :], 0), axis=-1)
    unit_map = jnp.where(at < picked[:, -1:], local, -1)
    experts = jnp.arange(N_EXPERTS, dtype=jnp.int32)
    owners = lax.cummin(jnp.where(n_blocks > 0, experts, N_EXPERTS), reverse=True)
    next_exp = jnp.concatenate([owners[1:], jnp.full((1,), N_EXPERTS, jnp.int32)])
    next_exp = jnp.where(next_exp < N_EXPERTS, next_exp, -1)
    i32 = lambda a: a.astype(jnp.int32)
    held = jnp.clip(picked[:, -1] - blk_at, 0, BLK_UNITS)
    return dict(blk_exp=blk_exp, unit_map=i32(unit_map.reshape(-1)), held=i32(held), used=i32(used),
                next_exp=i32(next_exp))


UNIT_WORDS = SEG // 2


def _as_units(words_2d):
    return words_2d.reshape(words_2d.shape[0] // UNIT_WORDS, UNIT_WORDS, words_2d.shape[1])


def _moe_kernel(exp_ref, map_ref, held_ref, used_ref, next_ref,
                xs_hbm, win_hbm, bin_ref, wout_hbm, bout_ref, ys_hbm,
                xbuf, ybuf, zbuf, win_f32, wout_f32, win_bf, wout_bf, gsem, ssem, zsem, wsem,
                *, layer):
    b = pl.program_id(0)
    last = pl.num_programs(0) - 1

    def zero_tail(tile, act):
        used = used_ref[tile]
        n = TILE_UNITS - used
        for size in _bits_down(SLACK_UNITS):
            @pl.when((n & size) != 0)
            def _():
                start = n & ~(2 * size - 1)
                act(pltpu.make_async_copy(zbuf.at[pl.ds(0, size)],
                                          ys_hbm.at[pl.ds(tile * TILE_UNITS + used + start, size)], zsem))

    def gather(block, buf):
        for u in range(BLK_UNITS):
            local = jnp.maximum(map_ref[block * BLK_UNITS + u], 0)
            pltpu.make_async_copy(xs_hbm.at[local], xbuf.at[buf, u], gsem.at[buf]).start()

    def scatter(block, buf):
        trash = N_TILES * TILE_UNITS + buf * BLK_UNITS
        for u in range(BLK_UNITS):
            local = map_ref[block * BLK_UNITS + u]
            local = jnp.where(local >= 0, local, trash + u)
            pltpu.make_async_copy(ybuf.at[buf, u], ys_hbm.at[local], ssem.at[buf]).start()

    def wait_block(buffers, sem, buf):
        pltpu.make_async_copy(buffers.at[buf], buffers.at[buf], sem.at[buf]).wait()

    def weight_copies(e):
        return (pltpu.make_async_copy(win_hbm.at[layer, e], win_f32, wsem.at[0]),
                pltpu.make_async_copy(wout_hbm.at[layer, e], wout_f32, wsem.at[1]))

    @pl.when(b == 0)
    def _():
        for copy in weight_copies(exp_ref[0]):
            copy.start(priority=1)

    new_expert = jnp.logical_or(b == 0, exp_ref[b] != exp_ref[jnp.maximum(b - 1, 0)])

    @pl.when(new_expert)
    def _():
        e = exp_ref[b]
        for copy in weight_copies(e):
            copy.wait()
        win_bf[...] = win_f32[...].astype(jnp.bfloat16)
        wout_bf[...] = wout_f32[...].astype(jnp.bfloat16)
        upcoming = next_ref[e]

        @pl.when(upcoming >= 0)
        def _():
            for copy in weight_copies(upcoming):
                copy.start(priority=1)

    @pl.when(b == 0)
    def _():
        for ahead in range(AHEAD):
            gather(ahead, ahead)
        ybuf[...] = jnp.zeros_like(ybuf)
        zbuf[...] = jnp.zeros_like(zbuf)
        for piece in range(N_BUF * BLK_UNITS // SLACK_UNITS):
            trash = pltpu.make_async_copy(zbuf.at[pl.ds(0, SLACK_UNITS)],
                                          ys_hbm.at[pl.ds(N_TILES * TILE_UNITS + piece * SLACK_UNITS, SLACK_UNITS)], zsem)
            trash.start()
            trash.wait()

    @pl.when(jnp.logical_and(b >= 1, b <= N_TILES))
    def _():
        zero_tail(b - 1, lambda copy: copy.wait())

    @pl.when(b < N_TILES)
    def _():
        zero_tail(b, lambda copy: copy.start())

    slot = b % N_BUF
    wait_block(xbuf, gsem, slot)

    @pl.when(b >= N_BUF)
    def _():
        wait_block(ybuf, ssem, slot)

    gather(jnp.minimum(b + AHEAD, last), (b + AHEAD) % N_BUF)

    @pl.when(b >= 1)
    def _():
        scatter(b - 1, (b - 1) % N_BUF)

    def mlp(rows):
        x = pltpu.bitcast(xbuf[slot, 0:rows // SEG].reshape(rows // 2, D_MODEL), jnp.bfloat16)
        u = jnp.dot(x, win_bf[...], preferred_element_type=jnp.float32) + bin_ref[...]
        glu = jnp.minimum(u[:, :D_EXPERT], SWIGLU_LIMIT)
        lin = jnp.clip(u[:, D_EXPERT:], -SWIGLU_LIMIT, SWIGLU_LIMIT)
        act = glu * jax.nn.sigmoid(SWIGLU_ALPHA * glu) * (lin + 1.0)
        y = jnp.dot(act.astype(jnp.bfloat16), wout_bf[...], preferred_element_type=jnp.float32) + bout_ref[...]
        ybuf[slot, 0:rows // SEG] = _as_units(pltpu.bitcast(y.astype(jnp.bfloat16), jnp.uint32))

    held = held_ref[b]
    pl.when(held > BLK_UNITS // 2)(functools.partial(mlp, BM))
    pl.when(jnp.logical_and(held > 0, held <= BLK_UNITS // 2))(functools.partial(mlp, BM // 2))

    @pl.when(b == last)
    def _():
        scatter(b, slot)
        for k in range(N_BUF):
            wait_block(ybuf, ssem, (b + k) % N_BUF)
        for k in range(1, AHEAD + 1):
            wait_block(xbuf, gsem, (b + k) % N_BUF)


def _moe_experts(xs_local, route, w_in, b_in, w_out, b_out, layer):
    by_expert = lambda b, ex, *_: (layer, ex[b], 0, 0)
    tables = [route[k] for k in ("blk_exp", "unit_map", "held", "used", "next_exp")]
    grid_spec = pltpu.PrefetchScalarGridSpec(
        num_scalar_prefetch=len(tables),
        grid=(N_BLKP,),
        in_specs=[
            pl.BlockSpec(memory_space=pl.ANY),
            pl.BlockSpec(memory_space=pl.ANY),
            pl.BlockSpec((None, None, 1, 2 * D_EXPERT), by_expert),
            pl.BlockSpec(memory_space=pl.ANY),
            pl.BlockSpec((None, None, 1, D_MODEL), by_expert),
        ],
        out_specs=pl.BlockSpec(memory_space=pl.ANY),
        scratch_shapes=[
            pltpu.VMEM((N_BUF, BLK_UNITS, UNIT_WORDS, D_MODEL), jnp.uint32),
            pltpu.VMEM((N_BUF, BLK_UNITS, UNIT_WORDS, D_MODEL), jnp.uint32),
            pltpu.VMEM((SLACK_UNITS, UNIT_WORDS, D_MODEL), jnp.uint32),
            pltpu.VMEM((D_MODEL, 2 * D_EXPERT), jnp.float32),
            pltpu.VMEM((D_EXPERT, D_MODEL), jnp.float32),
            pltpu.VMEM((D_MODEL, 2 * D_EXPERT), jnp.bfloat16),
            pltpu.VMEM((D_EXPERT, D_MODEL), jnp.bfloat16),
            pltpu.SemaphoreType.DMA((N_BUF,)),
            pltpu.SemaphoreType.DMA((N_BUF,)),
            pltpu.SemaphoreType.DMA(()),
            pltpu.SemaphoreType.DMA((2,)),
        ],
    )
    ys_units = pl.pallas_call(
        functools.partial(_moe_kernel, layer=layer),
        grid_spec=grid_spec,
        out_shape=jax.ShapeDtypeStruct((N_TILES * TILE_UNITS + N_BUF * BLK_UNITS, UNIT_WORDS, D_MODEL), jnp.uint32),
        compiler_params=_cparams("arbitrary"),
        name="moe_experts",
    )(*tables, _as_units(xs_local), w_in, b_in.reshape(DEPTH, N_EXPERTS, 1, 2 * D_EXPERT), w_out,
      b_out.reshape(DEPTH, N_EXPERTS, 1, D_MODEL))
    return ys_units.reshape((LOCAL_ROWS + N_BUF * BM) // 2, D_MODEL)


def _combine_kernel(ys_ref, pos_ref, gate_ref, x_ref, mod_ref, gfin_ref, o_ref, *, final_norm):
    pos = pos_ref[...].astype(jnp.int16)
    gates = gate_ref[...].astype(jnp.bfloat16)
    q = lax.broadcasted_iota(jnp.int32, (TM, TILE_ROWS), 1).astype(jnp.int16)
    weights = jnp.zeros((TM, TILE_ROWS), jnp.bfloat16)
    for k in range(TOP_K):
        weights = jnp.where(q == pos[:, k:k + 1], gates[:, k:k + 1], weights)
    ys = pltpu.bitcast(ys_ref[...], jnp.bfloat16)
    mixed = jnp.dot(weights, ys, preferred_element_type=jnp.float32)
    x_new = x_ref[...] + mod_ref[5:6, :] * mixed
    if final_norm:
        x_new = x_new * lax.rsqrt(jnp.mean(x_new * x_new, axis=-1, keepdims=True) + NORM_EPS) * gfin_ref[...]
    o_ref[...] = x_new


def _combine(ys_local, pos, gates, x, mod, g_final, layer, *, final_norm):
    return pl.pallas_call(
        functools.partial(_combine_kernel, final_norm=final_norm),
        grid=(N_TILES,),
        in_specs=[
            pl.BlockSpec((TILE_ROWS // 2, D_MODEL), lambda i: (i, 0)),
            pl.BlockSpec((TM, TOP_K), lambda i: (i, 0)),
            pl.BlockSpec((TM, TOP_K), lambda i: (i, 0)),
            pl.BlockSpec((TM, D_MODEL), lambda i: (i, 0)),
            _mod_spec(layer),
            pl.BlockSpec((1, D_MODEL), lambda i: (0, 0)),
        ],
        out_specs=pl.BlockSpec((TM, D_MODEL), lambda i: (i, 0)),
        out_shape=jax.ShapeDtypeStruct((N_TOK, D_MODEL), jnp.float32),
        compiler_params=_cparams("arbitrary"),
        name="moe_combine",
    )(ys_local, pos, gates, x, mod, g_final.reshape(1, D_MODEL))


def kernel(x, c, w_mod, b_mod, norm_mix_g, norm_ffn_g, attn_w_qkv, attn_b_qkv, attn_w_o, attn_b_o,
           attn_sinks, rel_bias_table, conv_w_in, conv_b_in, conv_w_dw, conv_b_dw, conv_ln_g, conv_ln_b,
           conv_w_out, conv_b_out, router_w, router_b, moe_w_in, moe_b_in, moe_w_out, moe_b_out,
           final_norm_g):
    bf16 = jnp.bfloat16
    mod = _modulation(c, w_mod, b_mod)
    bias = _rel_bias(rel_bias_table)
    xf = x.reshape(N_TOK, D_MODEL)
    for i in range(DEPTH):
        j = i // 2
        if i % 2 == 0:
            qkv = _in_proj(xf, mod, norm_mix_g[i], attn_w_qkv[j].astype(bf16), attn_b_qkv[j], i,
                           glu=False, out_dtype=bf16)
            attn = _attention(qkv, bias, attn_sinks, j)
            xf, xs_local, pos, gates, units = _attn_out(attn, attn_w_o[j].astype(bf16), attn_b_o[j], xf, mod,
                                                        norm_ffn_g[i], router_w[i], router_b[i], i)
        else:
            u = _in_proj(xf, mod, norm_mix_g[i], conv_w_in[j].astype(bf16), conv_b_in[j], i,
                         glu=True, out_dtype=jnp.float32)
            xf, xs_local, pos, gates, units = _conv_out(u, conv_w_dw[j], conv_b_dw[j], conv_ln_g[j], conv_ln_b[j],
                                                        conv_w_out[j].astype(bf16), conv_b_out[j], xf, mod,
                                                        norm_ffn_g[i], router_w[i], router_b[i], i)
        route = _routing(units)
        ys_local = _moe_experts(xs_local, route, moe_w_in, moe_b_in, moe_w_out, moe_b_out, i)
        xf = _combine(ys_local, pos, gates, xf, mod, final_norm_g, i, final_norm=(i == DEPTH - 1))
    return xf.reshape(BATCH, SEQ, D_MODEL)
```

```python
import functools
import math

import jax
import jax.numpy as jnp
from jax import lax
from jax.experimental import pallas as pl
from jax.experimental.pallas import tpu as pltpu

D_MODEL = 1024
BATCH = 8
SEQ = 2048
DEPTH = 4
N_TOK = BATCH * SEQ
HEAD_DIM = 64
N_Q_HEADS = 16
N_KV_HEADS = 4
GQA_GROUP = 4
WINDOW = 128
Q_DIM = 1024
KV_DIM = 256
QKV_DIM = 1536
NUM_BUCKETS = 32
MAX_EXACT = 16
MAX_DISTANCE = 128
CONV_WIDTH = 31
N_EXPERTS = 32
TOP_K = 4
D_EXPERT = 1024
SWIGLU_LIMIT = 7.0
SWIGLU_ALPHA = 1.702
NORM_EPS = 1e-5
NEG_INF = -1e30

LANES = 128
SUBLANES = 8
ROW_TILES = D_MODEL // LANES
N_SLOT = N_TOK * TOP_K

TM = 512
TM_ATTN = WINDOW
N_TILES = N_TOK // TM
SEG = 16
TILE_ROWS = -(-(TM * TOP_K + N_EXPERTS * (SEG - 1)) // 256) * 256
TILE_UNITS = TILE_ROWS // SEG
BM = 512
BLK_UNITS = BM // SEG
N_BUF = 3
AHEAD = 2
SLACK_UNITS = TILE_UNITS - TM * TOP_K // SEG


def _bits_down(n):
    return tuple(1 << i for i in reversed(range(n.bit_length())))
N_BLKP = N_TILES * TILE_ROWS // BM + N_EXPERTS
LOCAL_ROWS = N_TILES * TILE_ROWS
CONV_HALO = 32
CONV_ROWS = 256

VMEM_LIMIT = 56 * 1024 * 1024


def _cparams(*sem):
    return pltpu.CompilerParams(dimension_semantics=sem, vmem_limit_bytes=VMEM_LIMIT)


def _mod_kernel(c_ref, w_ref, b_ref, o_ref):
    c = c_ref[...]
    c_act = (c * jax.nn.sigmoid(c)).astype(jnp.bfloat16)
    o_ref[...] = jnp.dot(c_act, w_ref[...].astype(jnp.bfloat16),
                         preferred_element_type=jnp.float32) + b_ref[...]


def _modulation(c, w_mod, b_mod):
    out = pl.pallas_call(
        _mod_kernel,
        grid=(DEPTH, 6),
        in_specs=[
            pl.BlockSpec((BATCH, D_MODEL), lambda i, j: (0, 0)),
            pl.BlockSpec((None, D_MODEL, D_MODEL), lambda i, j: (i, 0, j)),
            pl.BlockSpec((None, None, 1, D_MODEL), lambda i, j: (i, j, 0, 0)),
        ],
        out_specs=pl.BlockSpec((None, None, BATCH, D_MODEL), lambda i, j: (i, j, 0, 0)),
        out_shape=jax.ShapeDtypeStruct((DEPTH, 6, BATCH, D_MODEL), jnp.float32),
        compiler_params=_cparams("arbitrary", "arbitrary"),
        name="adaln_modulation",
    )(c, w_mod, b_mod.reshape(DEPTH, 6, 1, D_MODEL))
    return jnp.transpose(out, (0, 2, 1, 3))


def _mod_spec(layer):
    return pl.BlockSpec((None, None, 6, D_MODEL), lambda i: (layer, i // (SEQ // TM), 0, 0))


def _rms_modulate(x, g, scale, shift):
    y = x * lax.rsqrt(jnp.mean(x * x, axis=-1, keepdims=True) + NORM_EPS)
    return (y * g) * (1.0 + scale) + shift


def _in_proj_kernel(x_ref, mod_ref, g_ref, w_ref, b_ref, o_ref, *, glu):
    h = _rms_modulate(x_ref[...], g_ref[...], mod_ref[1:2, :], mod_ref[0:1, :])
    u = jnp.dot(h.astype(jnp.bfloat16), w_ref[...], preferred_element_type=jnp.float32) + b_ref[...]
    if glu:
        half = u.shape[-1] // 2
        u = u[:, :half] * jax.nn.sigmoid(u[:, half:])
    o_ref[...] = u.astype(o_ref.dtype)


def _in_proj(x, mod, g, w, b, layer, *, glu, out_dtype):
    n_in = w.shape[-1]
    n_out = n_in // 2 if glu else n_in
    return pl.pallas_call(
        functools.partial(_in_proj_kernel, glu=glu),
        grid=(N_TOK // TM,),
        in_specs=[
            pl.BlockSpec((TM, D_MODEL), lambda i: (i, 0)),
            _mod_spec(layer),
            pl.BlockSpec((1, D_MODEL), lambda i: (0, 0)),
            pl.BlockSpec((D_MODEL, n_in), lambda i: (0, 0)),
            pl.BlockSpec((1, n_in), lambda i: (0, 0)),
        ],
        out_specs=pl.BlockSpec((TM, n_out), lambda i: (i, 0)),
        out_shape=jax.ShapeDtypeStruct((N_TOK, n_out), out_dtype),
        compiler_params=_cparams("arbitrary"),
        name="mixer_in_proj_glu" if glu else "mixer_in_proj",
    )(x, mod, g.reshape(1, D_MODEL), w, b.reshape(1, n_in))


def _band_geometry():
    dist = jnp.arange(WINDOW)[:, None] + WINDOW - jnp.arange(2 * WINDOW)[None, :]
    n = jnp.maximum(dist, 0)
    nf = jnp.maximum(n, 1).astype(jnp.float32)
    large = MAX_EXACT + (jnp.log(nf / MAX_EXACT) / math.log(MAX_DISTANCE / MAX_EXACT)
                         * (NUM_BUCKETS - MAX_EXACT)).astype(jnp.int32)
    large = jnp.minimum(large, NUM_BUCKETS - 1)
    bucket = jnp.where(n < MAX_EXACT, n, large)
    return dist, bucket


def _bias_kernel(table_ref, bucket_ref, dist_ref, o_ref):
    first = pl.program_id(0)
    h = pl.program_id(1)
    bucket = bucket_ref[...]
    dist = dist_ref[...]
    acc = jnp.zeros(bucket.shape, jnp.float32)
    for b in range(NUM_BUCKETS):
        acc = jnp.where(bucket == b, table_ref[b, h], acc)
    col = lax.broadcasted_iota(jnp.int32, bucket.shape, 1)
    visible = (dist >= 0) & (dist < WINDOW) & jnp.logical_or(col >= WINDOW, first == 0)
    o_ref[...] = jnp.where(visible, acc, NEG_INF)


def _rel_bias(table):
    dist, bucket = _band_geometry()
    return pl.pallas_call(
        _bias_kernel,
        grid=(2, N_Q_HEADS),
        in_specs=[
            pl.BlockSpec(memory_space=pltpu.SMEM),
            pl.BlockSpec((WINDOW, 2 * WINDOW), lambda f, h: (0, 0)),
            pl.BlockSpec((WINDOW, 2 * WINDOW), lambda f, h: (0, 0)),
        ],
        out_specs=pl.BlockSpec((None, None, WINDOW, 2 * WINDOW), lambda f, h: (f, h, 0, 0)),
        out_shape=jax.ShapeDtypeStruct((2, N_Q_HEADS, WINDOW, 2 * WINDOW), jnp.float32),
        compiler_params=_cparams("arbitrary", "arbitrary"),
        name="rel_bias",
    )(table, bucket.astype(jnp.int32), dist.astype(jnp.int32))


def _attn_kernel(sink_ref, q_ref, kv_ref, kvp_ref, bias_ref, o_ref, *, layer):
    i = pl.program_id(0)
    first_block = ((i % (SEQ // WINDOW)) == 0).astype(jnp.int32)
    q = q_ref[...]
    kv = jnp.concatenate([kvp_ref[...], kv_ref[...]], axis=0)
    outs = []
    for hk in range(N_KV_HEADS):
        k = kv[:, hk * HEAD_DIM:(hk + 1) * HEAD_DIM]
        v = kv[:, KV_DIM + hk * HEAD_DIM:KV_DIM + (hk + 1) * HEAD_DIM]
        for g in range(GQA_GROUP):
            hq = hk * GQA_GROUP + g
            qh = q[:, hq * HEAD_DIM:(hq + 1) * HEAD_DIM] * (HEAD_DIM ** -0.5)
            s = lax.dot_general(qh, k, (((1,), (1,)), ((), ())), preferred_element_type=jnp.float32)
            s = s + bias_ref[first_block, hq]
            sink = sink_ref[layer, hq]
            m = jnp.maximum(jnp.max(s, axis=-1, keepdims=True), sink)
            p = jnp.exp(s - m)
            denom = jnp.sum(p, axis=-1, keepdims=True) + jnp.exp(sink - m)
            o = jnp.dot(p.astype(jnp.bfloat16), v, preferred_element_type=jnp.float32)
            outs.append((o / denom).astype(jnp.bfloat16))
    o_ref[...] = jnp.concatenate(outs, axis=-1)


def _attention(qkv, bias, sinks, layer):
    kv_col = Q_DIM // (2 * KV_DIM)
    return pl.pallas_call(
        functools.partial(_attn_kernel, layer=layer),
        grid=(N_TOK // WINDOW,),
        in_specs=[
            pl.BlockSpec(memory_space=pltpu.SMEM),
            pl.BlockSpec((WINDOW, Q_DIM), lambda i: (i, 0)),
            pl.BlockSpec((WINDOW, 2 * KV_DIM), lambda i: (i, kv_col)),
            pl.BlockSpec((WINDOW, 2 * KV_DIM), lambda i: (jnp.maximum(i - 1, 0), kv_col)),
            pl.BlockSpec((2, N_Q_HEADS, WINDOW, 2 * WINDOW), lambda i: (0, 0, 0, 0)),
        ],
        out_specs=pl.BlockSpec((WINDOW, Q_DIM), lambda i: (i, 0)),
        out_shape=jax.ShapeDtypeStruct((N_TOK, Q_DIM), jnp.bfloat16),
        compiler_params=_cparams("arbitrary"),
        name="swa_attention",
    )(sinks, qkv, qkv, qkv, bias)


def _split_bf16(a):
    hi = a.astype(jnp.bfloat16)
    lo = (a - hi.astype(jnp.float32)).astype(jnp.bfloat16)
    return hi, lo


def _residual_ffn_router(m, x, mod, g_ffn, wr_hi, wr_lo, b_r, outs):
    xo_ref, xs_ref, pos_ref, gate_ref, units_ref = outs
    x_new = x + mod[2:3, :] * m
    xo_ref[...] = x_new
    h = _rms_modulate(x_new, g_ffn, mod[4:5, :], mod[3:4, :])
    rows = h.shape[0]
    h_hi, h_lo = _split_bf16(h)
    logits = (jnp.dot(h_hi, wr_hi, preferred_element_type=jnp.float32)
              + jnp.dot(h_hi, wr_lo, preferred_element_type=jnp.float32)
              + jnp.dot(h_lo, wr_hi, preferred_element_type=jnp.float32)) + b_r
    lane = lax.broadcasted_iota(jnp.int32, logits.shape, 1)
    vals, ids = [], []
    for _ in range(TOP_K):
        best = jnp.max(logits, axis=-1, keepdims=True)
        arg = jnp.min(jnp.where(logits == best, lane, N_EXPERTS), axis=-1, keepdims=True)
        vals.append(best)
        ids.append(arg)
        logits = jnp.where(lane == arg, -jnp.inf, logits)
    exps = [jnp.exp(v - vals[0]) for v in vals]
    total = exps[0] + exps[1] + exps[2] + exps[3]
    gate_ref[...] = jnp.concatenate([e / total for e in exps], axis=-1)

    member = jnp.zeros(logits.shape, jnp.float32)
    for arg in ids:
        member = member + (lane == arg).astype(jnp.float32)
    units = jnp.floor((jnp.sum(member, axis=0, keepdims=True) + (SEG - 1)) * (1.0 / SEG))
    units_ref[...] = units
    lower_experts = (lax.broadcasted_iota(jnp.int32, (N_EXPERTS, N_EXPERTS), 0)
                     < lax.broadcasted_iota(jnp.int32, (N_EXPERTS, N_EXPERTS), 1)).astype(jnp.bfloat16)
    seg_start = SEG * jnp.dot(jnp.broadcast_to(units, (SUBLANES, N_EXPERTS)).astype(jnp.bfloat16), lower_experts,
                              preferred_element_type=jnp.float32)[0:1, :]
    earlier = jnp.where(lax.broadcasted_iota(jnp.int32, (rows, rows), 0).astype(jnp.int16)
                        > lax.broadcasted_iota(jnp.int32, (rows, rows), 1).astype(jnp.int16),
                        jnp.ones((), jnp.bfloat16), jnp.zeros((), jnp.bfloat16))
    place = seg_start + jnp.dot(earlier, member.astype(jnp.bfloat16), preferred_element_type=jnp.float32)
    pos = [jnp.sum(jnp.where(lane == arg, place, 0.0), axis=-1, keepdims=True).astype(jnp.int32) for arg in ids]
    pos_ref[...] = jnp.concatenate(pos, axis=-1)

    column = lax.broadcasted_iota(jnp.int32, (rows, LANES), 1)
    pos_cols = jnp.zeros((rows, LANES), jnp.float32)
    for k in range(TOP_K):
        pos_cols = jnp.where(column == k, pos[k].astype(jnp.float32), pos_cols)
    pos_lanes = pos_cols.T
    p16 = [pos_lanes[k:k + 1, :].astype(jnp.int32).astype(jnp.int16) for k in range(TOP_K)]
    q = lax.broadcasted_iota(jnp.int32, (TILE_ROWS, rows), 0).astype(jnp.int16)
    hit = (q == p16[0]) | (q == p16[1]) | (q == p16[2]) | (q == p16[3])
    onehot = jnp.where(hit, jnp.ones((), jnp.bfloat16), jnp.zeros((), jnp.bfloat16))
    h_bf = h.astype(jnp.bfloat16)
    half = D_MODEL // 2
    for c in range(2):
        xs = jnp.dot(onehot, h_bf[:, c * half:(c + 1) * half], preferred_element_type=jnp.float32)
        xs_ref[:, c * half:(c + 1) * half] = pltpu.bitcast(xs.astype(jnp.bfloat16), jnp.uint32)


N_EPILOGUE_OUTS = 5


def _attn_out_kernel(a_ref, wo_ref, bo_ref, x_ref, mod_ref, g_ref, wrh_ref, wrl_ref, br_ref, *outs):
    m = jnp.dot(a_ref[...], wo_ref[...], preferred_element_type=jnp.float32) + bo_ref[...]
    _residual_ffn_router(m, x_ref[...], mod_ref[...], g_ref[...], wrh_ref[...], wrl_ref[...], br_ref[...], outs)


def _epilogue_out_specs():
    return [
        pl.BlockSpec((TM, D_MODEL), lambda i: (i, 0)),
        pl.BlockSpec((TILE_ROWS // 2, D_MODEL), lambda i: (i, 0)),
        pl.BlockSpec((TM, TOP_K), lambda i: (i, 0)),
        pl.BlockSpec((TM, TOP_K), lambda i: (i, 0)),
        pl.BlockSpec((None, 1, N_EXPERTS), lambda i: (i, 0, 0)),
    ]


def _epilogue_out_shapes():
    return [
        jax.ShapeDtypeStruct((N_TOK, D_MODEL), jnp.float32),
        jax.ShapeDtypeStruct((LOCAL_ROWS // 2, D_MODEL), jnp.uint32),
        jax.ShapeDtypeStruct((N_TOK, TOP_K), jnp.int32),
        jax.ShapeDtypeStruct((N_TOK, TOP_K), jnp.float32),
        jax.ShapeDtypeStruct((N_TILES, 1, N_EXPERTS), jnp.float32),
    ]


def _router_operands(w_r, b_r):
    hi, lo = _split_bf16(w_r)
    return hi, lo, b_r.reshape(1, N_EXPERTS)


def _router_specs():
    return [
        pl.BlockSpec((D_MODEL, N_EXPERTS), lambda i: (0, 0)),
        pl.BlockSpec((D_MODEL, N_EXPERTS), lambda i: (0, 0)),
        pl.BlockSpec((1, N_EXPERTS), lambda i: (0, 0)),
    ]


def _attn_out(attn, w_o, b_o, x, mod, g_ffn, w_r, b_r, layer):
    return pl.pallas_call(
        _attn_out_kernel,
        grid=(N_TOK // TM,),
        in_specs=[
            pl.BlockSpec((TM, Q_DIM), lambda i: (i, 0)),
            pl.BlockSpec((Q_DIM, D_MODEL), lambda i: (0, 0)),
            pl.BlockSpec((1, D_MODEL), lambda i: (0, 0)),
            pl.BlockSpec((TM, D_MODEL), lambda i: (i, 0)),
            _mod_spec(layer),
            pl.BlockSpec((1, D_MODEL), lambda i: (0, 0)),
        ] + _router_specs(),
        out_specs=_epilogue_out_specs(),
        out_shape=_epilogue_out_shapes(),
        compiler_params=_cparams("arbitrary"),
        name="attn_out_router",
    )(attn, w_o, b_o.reshape(1, D_MODEL), x, mod, g_ffn.reshape(1, D_MODEL), *_router_operands(w_r, b_r))


def _conv_out_kernel(u_ref, up_ref, wdw_ref, bdw_ref, lng_ref, lnb_ref, wo_ref, bo_ref,
                     x_ref, mod_ref, g_ref, wrh_ref, wrl_ref, br_ref, *rest):
    outs = rest[:N_EPILOGUE_OUTS]
    ext_ref, y_ref = rest[N_EPILOGUE_OUTS:]
    i = pl.program_id(0)
    first_tile = (i % (SEQ // TM)) == 0
    ext_ref[0:CONV_HALO, :] = jnp.where(first_tile, 0.0, up_ref[...])
    ext_ref[CONV_HALO:, :] = u_ref[...]
    lead = CONV_HALO - (CONV_WIDTH - 1)
    span = CONV_ROWS + CONV_HALO

    def chunk(r, carry):
        r0 = pl.multiple_of(r * CONV_ROWS, CONV_ROWS)
        for cb in range(ROW_TILES):
            cols = slice(cb * LANES, (cb + 1) * LANES)
            e = ext_ref[pl.ds(r0, span), cols]
            acc = jnp.zeros((CONV_ROWS, LANES), jnp.float32)
            for sub in range(SUBLANES):
                shifted = e if sub == 0 else pltpu.roll(e, span - sub, 0)
                for a in range(span // SUBLANES):
                    tap = a * SUBLANES + sub - lead
                    if 0 <= tap < CONV_WIDTH:
                        acc = acc + wdw_ref[tap:tap + 1, cols] * shifted[a * SUBLANES:a * SUBLANES + CONV_ROWS]
            y_ref[pl.ds(r0, CONV_ROWS), cols] = acc
        return carry

    lax.fori_loop(0, TM // CONV_ROWS, chunk, 0)
    y = y_ref[...] + bdw_ref[...]
    mu = jnp.mean(y, axis=-1, keepdims=True)
    var = jnp.mean(jnp.square(y - mu), axis=-1, keepdims=True)
    z = (y - mu) * lax.rsqrt(var + NORM_EPS) * lng_ref[...] + lnb_ref[...]
    z = z * jax.nn.sigmoid(z)
    m = jnp.dot(z.astype(jnp.bfloat16), wo_ref[...], preferred_element_type=jnp.float32) + bo_ref[...]
    _residual_ffn_router(m, x_ref[...], mod_ref[...], g_ref[...], wrh_ref[...], wrl_ref[...], br_ref[...],
                         outs)


def _conv_out(u, w_dw, b_dw, ln_g, ln_b, w_o, b_o, x, mod, g_ffn, w_r, b_r, layer):
    halo_blocks = TM // CONV_HALO
    row = lambda a: a.reshape(1, D_MODEL)
    return pl.pallas_call(
        _conv_out_kernel,
        grid=(N_TOK // TM,),
        in_specs=[
            pl.BlockSpec((TM, D_MODEL), lambda i: (i, 0)),
            pl.BlockSpec((CONV_HALO, D_MODEL), lambda i: (jnp.maximum(i * halo_blocks - 1, 0), 0)),
            pl.BlockSpec((CONV_WIDTH, D_MODEL), lambda i: (0, 0)),
            pl.BlockSpec((1, D_MODEL), lambda i: (0, 0)),
            pl.BlockSpec((1, D_MODEL), lambda i: (0, 0)),
            pl.BlockSpec((1, D_MODEL), lambda i: (0, 0)),
            pl.BlockSpec((D_MODEL, D_MODEL), lambda i: (0, 0)),
            pl.BlockSpec((1, D_MODEL), lambda i: (0, 0)),
            pl.BlockSpec((TM, D_MODEL), lambda i: (i, 0)),
            _mod_spec(layer),
            pl.BlockSpec((1, D_MODEL), lambda i: (0, 0)),
        ] + _router_specs(),
        out_specs=_epilogue_out_specs(),
        out_shape=_epilogue_out_shapes(),
        scratch_shapes=[
            pltpu.VMEM((TM + CONV_HALO, D_MODEL), jnp.float32),
            pltpu.VMEM((TM, D_MODEL), jnp.float32),
        ],
        compiler_params=_cparams("arbitrary"),
        name="conv_out_router",
    )(u, u, w_dw, row(b_dw), row(ln_g), row(ln_b), w_o, row(b_o), x, mod, row(g_ffn),
      *_router_operands(w_r, b_r))


def _routing(units):
    per_tile = units.reshape(N_TILES, N_EXPERTS).astype(jnp.int32)
    seg_off = jnp.cumsum(per_tile, axis=1) - per_tile
    used = jnp.sum(per_tile, axis=1)
    zero = jnp.zeros((N_EXPERTS, 1), jnp.int32)
    cum = jnp.concatenate([zero, jnp.cumsum(per_tile.T, axis=1)], axis=1)
    total = cum[:, -1]
    n_blocks = (total + BLK_UNITS - 1) // BLK_UNITS
    blk_end = jnp.cumsum(n_blocks)
    blk_start = blk_end - n_blocks
    n_live = blk_end[-1]
    blk = jnp.arange(N_BLKP, dtype=jnp.int32)
    blk_exp = jnp.minimum(jnp.sum(blk[:, None] >= blk_end[None, :], axis=1), N_EXPERTS - 1)
    blk_exp = jnp.where(blk < n_live, blk_exp, blk_exp[n_live - 1]).astype(jnp.int32)
    per_expert = jnp.concatenate([cum, seg_off.T - cum[:, :N_TILES], blk_start[:, None], total[:, None]], axis=1)
    picked = jnp.dot(jax.nn.one_hot(blk_exp, N_EXPERTS, dtype=jnp.float32), per_expert.astype(jnp.float32),
                     precision=lax.Precision.HIGHEST).astype(jnp.int32)
    starts = picked[:, :N_TILES + 1]
    shift = picked[:, N_TILES + 1:2 * N_TILES + 1]
    blk_at = (blk - picked[:, -2]) * BLK_UNITS
    at = blk_at[:, None] + jnp.arange(BLK_UNITS, dtype=jnp.int32)[None, :]
    tile = jnp.minimum(jnp.sum(starts[:, None, 1:] <= at[:, :, None], axis=-1), N_TILES - 1)
    in_tile = tile[:, :, None] == jnp.arange(N_TILES, dtype=jnp.int32)[None, None, :]
    local = tile * TILE_UNITS + at + jnp.sum(jnp.where(in_tile, shift[:, None, :], 0), axis=-1)
    unit_map = jnp.where(at < picked[:, -1:], local, -1)
    experts = jnp.arange(N_EXPERTS, dtype=jnp.int32)
    owners = lax.cummin(jnp.where(n_blocks > 0, experts, N_EXPERTS), reverse=True)
    next_exp = jnp.concatenate([owners[1:], jnp.full((1,), N_EXPERTS, jnp.int32)])
    next_exp = jnp.where(next_exp < N_EXPERTS, next_exp, -1)
    i32 = lambda a: a.astype(jnp.int32)
    held = jnp.clip(picked[:, -1] - blk_at, 0, BLK_UNITS)
    return dict(blk_exp=blk_exp, unit_map=i32(unit_map.reshape(-1)), held=i32(held), used=i32(used),
                next_exp=i32(next_exp))


UNIT_WORDS = SEG // 2


def _as_units(words_2d):
    return words_2d.reshape(words_2d.shape[0] // UNIT_WORDS, UNIT_WORDS, words_2d.shape[1])


def _moe_kernel(exp_ref, map_ref, held_ref, used_ref, next_ref,
                xs_hbm, win_hbm, bin_ref, wout_hbm, bout_ref, ys_hbm,
                xbuf, ybuf, zbuf, win_f32, wout_f32, win_bf, wout_bf, gsem, ssem, zsem, wsem,
                *, layer):
    b = pl.program_id(0)
    last = pl.num_programs(0) - 1

    def zero_tail(tile, act):
        used = used_ref[tile]
        n = TILE_UNITS - used
        for size in _bits_down(SLACK_UNITS):
            @pl.when((n & size) != 0)
            def _():
                start = n & ~(2 * size - 1)
                act(pltpu.make_async_copy(zbuf.at[pl.ds(0, size)],
                                          ys_hbm.at[pl.ds(tile * TILE_UNITS + used + start, size)], zsem))

    def gather(block, buf):
        for u in range(BLK_UNITS):
            local = jnp.maximum(map_ref[block * BLK_UNITS + u], 0)
            pltpu.make_async_copy(xs_hbm.at[local], xbuf.at[buf, u], gsem.at[buf]).start()

    def scatter(block, buf):
        trash = N_TILES * TILE_UNITS + buf * BLK_UNITS
        for u in range(BLK_UNITS):
            local = map_ref[block * BLK_UNITS + u]
            local = jnp.where(local >= 0, local, trash + u)
            pltpu.make_async_copy(ybuf.at[buf, u], ys_hbm.at[local], ssem.at[buf]).start()

    def wait_block(buffers, sem, buf):
        pltpu.make_async_copy(buffers.at[buf], buffers.at[buf], sem.at[buf]).wait()

    def weight_copies(e):
        return (pltpu.make_async_copy(win_hbm.at[layer, e], win_f32, wsem.at[0]),
                pltpu.make_async_copy(wout_hbm.at[layer, e], wout_f32, wsem.at[1]))

    @pl.when(b == 0)
    def _():
        for copy in weight_copies(exp_ref[0]):
            copy.start(priority=1)

    new_expert = jnp.logical_or(b == 0, exp_ref[b] != exp_ref[jnp.maximum(b - 1, 0)])

    @pl.when(new_expert)
    def _():
        e = exp_ref[b]
        for copy in weight_copies(e):
            copy.wait()
        win_bf[...] = win_f32[...].astype(jnp.bfloat16)
        wout_bf[...] = wout_f32[...].astype(jnp.bfloat16)
        upcoming = next_ref[e]

        @pl.when(upcoming >= 0)
        def _():
            for copy in weight_copies(upcoming):
                copy.start(priority=1)

    @pl.when(b == 0)
    def _():
        for ahead in range(AHEAD):
            gather(ahead, ahead)
        ybuf[...] = jnp.zeros_like(ybuf)
        zbuf[...] = jnp.zeros_like(zbuf)
        for piece in range(N_BUF * BLK_UNITS // SLACK_UNITS):
            trash = pltpu.make_async_copy(zbuf.at[pl.ds(0, SLACK_UNITS)],
                                          ys_hbm.at[pl.ds(N_TILES * TILE_UNITS + piece * SLACK_UNITS, SLACK_UNITS)], zsem)
            trash.start()
            trash.wait()

    @pl.when(jnp.logical_and(b >= 1, b <= N_TILES))
    def _():
        zero_tail(b - 1, lambda copy: copy.wait())

    @pl.when(b < N_TILES)
    def _():
        zero_tail(b, lambda copy: copy.start())

    slot = b % N_BUF
    wait_block(xbuf, gsem, slot)

    @pl.when(b >= N_BUF)
    def _():
        wait_block(ybuf, ssem, slot)

    gather(jnp.minimum(b + AHEAD, last), (b + AHEAD) % N_BUF)

    @pl.when(b >= 1)
    def _():
        scatter(b - 1, (b - 1) % N_BUF)

    def mlp(rows):
        x = pltpu.bitcast(xbuf[slot, 0:rows // SEG].reshape(rows // 2, D_MODEL), jnp.bfloat16)
        u = jnp.dot(x, win_bf[...], preferred_element_type=jnp.float32) + bin_ref[...]
        glu = jnp.minimum(u[:, :D_EXPERT], SWIGLU_LIMIT)
        lin = jnp.clip(u[:, D_EXPERT:], -SWIGLU_LIMIT, SWIGLU_LIMIT)
        act = glu * jax.nn.sigmoid(SWIGLU_ALPHA * glu) * (lin + 1.0)
        y = jnp.dot(act.astype(jnp.bfloat16), wout_bf[...], preferred_element_type=jnp.float32) + bout_ref[...]
        ybuf[slot, 0:rows // SEG] = _as_units(pltpu.bitcast(y.astype(jnp.bfloat16), jnp.uint32))

    held = held_ref[b]
    pl.when(held > BLK_UNITS // 2)(functools.partial(mlp, BM))
    pl.when(jnp.logical_and(held > 0, held <= BLK_UNITS // 2))(functools.partial(mlp, BM // 2))

    @pl.when(b == last)
    def _():
        scatter(b, slot)
        for k in range(N_BUF):
            wait_block(ybuf, ssem, (b + k) % N_BUF)
        for k in range(1, AHEAD + 1):
            wait_block(xbuf, gsem, (b + k) % N_BUF)


def _moe_experts(xs_local, route, w_in, b_in, w_out, b_out, layer):
    by_expert = lambda b, ex, *_: (layer, ex[b], 0, 0)
    tables = [route[k] for k in ("blk_exp", "unit_map", "held", "used", "next_exp")]
    grid_spec = pltpu.PrefetchScalarGridSpec(
        num_scalar_prefetch=len(tables),
        grid=(N_BLKP,),
        in_specs=[
            pl.BlockSpec(memory_space=pl.ANY),
            pl.BlockSpec(memory_space=pl.ANY),
            pl.BlockSpec((None, None, 1, 2 * D_EXPERT), by_expert),
            pl.BlockSpec(memory_space=pl.ANY),
            pl.BlockSpec((None, None, 1, D_MODEL), by_expert),
        ],
        out_specs=pl.BlockSpec(memory_space=pl.ANY),
        scratch_shapes=[
            pltpu.VMEM((N_BUF, BLK_UNITS, UNIT_WORDS, D_MODEL), jnp.uint32),
            pltpu.VMEM((N_BUF, BLK_UNITS, UNIT_WORDS, D_MODEL), jnp.uint32),
            pltpu.VMEM((SLACK_UNITS, UNIT_WORDS, D_MODEL), jnp.uint32),
            pltpu.VMEM((D_MODEL, 2 * D_EXPERT), jnp.float32),
            pltpu.VMEM((D_EXPERT, D_MODEL), jnp.float32),
            pltpu.VMEM((D_MODEL, 2 * D_EXPERT), jnp.bfloat16),
            pltpu.VMEM((D_EXPERT, D_MODEL), jnp.bfloat16),
            pltpu.SemaphoreType.DMA((N_BUF,)),
            pltpu.SemaphoreType.DMA((N_BUF,)),
            pltpu.SemaphoreType.DMA(()),
            pltpu.SemaphoreType.DMA((2,)),
        ],
    )
    ys_units = pl.pallas_call(
        functools.partial(_moe_kernel, layer=layer),
        grid_spec=grid_spec,
        out_shape=jax.ShapeDtypeStruct((N_TILES * TILE_UNITS + N_BUF * BLK_UNITS, UNIT_WORDS, D_MODEL), jnp.uint32),
        compiler_params=_cparams("arbitrary"),
        name="moe_experts",
    )(*tables, _as_units(xs_local), w_in, b_in.reshape(DEPTH, N_EXPERTS, 1, 2 * D_EXPERT), w_out,
      b_out.reshape(DEPTH, N_EXPERTS, 1, D_MODEL))
    return ys_units.reshape((LOCAL_ROWS + N_BUF * BM) // 2, D_MODEL)


def _combine_kernel(ys_ref, pos_ref, gate_ref, x_ref, mod_ref, gfin_ref, o_ref, *, final_norm):
    pos = pos_ref[...].astype(jnp.int16)
    gates = gate_ref[...].astype(jnp.bfloat16)
    q = lax.broadcasted_iota(jnp.int32, (TM, TILE_ROWS), 1).astype(jnp.int16)
    weights = jnp.zeros((TM, TILE_ROWS), jnp.bfloat16)
    for k in range(TOP_K):
        weights = jnp.where(q == pos[:, k:k + 1], gates[:, k:k + 1], weights)
    ys = pltpu.bitcast(ys_ref[...], jnp.bfloat16)
    mixed = jnp.dot(weights, ys, preferred_element_type=jnp.float32)
    x_new = x_ref[...] + mod_ref[5:6, :] * mixed
    if final_norm:
        x_new = x_new * lax.rsqrt(jnp.mean(x_new * x_new, axis=-1, keepdims=True) + NORM_EPS) * gfin_ref[...]
    o_ref[...] = x_new


def _combine(ys_local, pos, gates, x, mod, g_final, layer, *, final_norm):
    return pl.pallas_call(
        functools.partial(_combine_kernel, final_norm=final_norm),
        grid=(N_TILES,),
        in_specs=[
            pl.BlockSpec((TILE_ROWS // 2, D_MODEL), lambda i: (i, 0)),
            pl.BlockSpec((TM, TOP_K), lambda i: (i, 0)),
            pl.BlockSpec((TM, TOP_K), lambda i: (i, 0)),
            pl.BlockSpec((TM, D_MODEL), lambda i: (i, 0)),
            _mod_spec(layer),
            pl.BlockSpec((1, D_MODEL), lambda i: (0, 0)),
        ],
        out_specs=pl.BlockSpec((TM, D_MODEL), lambda i: (i, 0)),
        out_shape=jax.ShapeDtypeStruct((N_TOK, D_MODEL), jnp.float32),
        compiler_params=_cparams("arbitrary"),
        name="moe_combine",
    )(ys_local, pos, gates, x, mod, g_final.reshape(1, D_MODEL))


def kernel(x, c, w_mod, b_mod, norm_mix_g, norm_ffn_g, attn_w_qkv, attn_b_qkv, attn_w_o, attn_b_o,
           attn_sinks, rel_bias_table, conv_w_in, conv_b_in, conv_w_dw, conv_b_dw, conv_ln_g, conv_ln_b,
           conv_w_out, conv_b_out, router_w, router_b, moe_w_in, moe_b_in, moe_w_out, moe_b_out,
           final_norm_g):
    bf16 = jnp.bfloat16
    mod = _modulation(c, w_mod, b_mod)
    bias = _rel_bias(rel_bias_table)
    xf = x.reshape(N_TOK, D_MODEL)
    for i in range(DEPTH):
        j = i // 2
        if i % 2 == 0:
            qkv = _in_proj(xf, mod, norm_mix_g[i], attn_w_qkv[j].astype(bf16), attn_b_qkv[j], i,
                           glu=False, out_dtype=bf16)
            attn = _attention(qkv, bias, attn_sinks, j)
            xf, xs_local, pos, gates, units = _attn_out(attn, attn_w_o[j].astype(bf16), attn_b_o[j], xf, mod,
                                                        norm_ffn_g[i], router_w[i], router_b[i], i)
        else:
            u = _in_proj(xf, mod, norm_mix_g[i], conv_w_in[j].astype(bf16), conv_b_in[j], i,
                         glu=True, out_dtype=jnp.float32)
            xf, xs_local, pos, gates, units = _conv_out(u, conv_w_dw[j], conv_b_dw[j], conv_ln_g[j], conv_ln_b[j],
                                                        conv_w_out[j].astype(bf16), conv_b_out[j], xf, mod,
                                                        norm_ffn_g[i], router_w[i], router_b[i], i)
        route = _routing(units)
        ys_local = _moe_experts(xs_local, route, moe_w_in, moe_b_in, moe_w_out, moe_b_out, i)
        xf = _combine(ys_local, pos, gates, xf, mod, final_norm_g, i, final_norm=(i == DEPTH - 1))
    return xf.reshape(BATCH, SEQ, D_MODEL)
```

```python
import functools
import math

import jax
import jax.numpy as jnp
from jax import lax
from jax.experimental import pallas as pl
from jax.experimental.pallas import tpu as pltpu

D_MODEL = 1024
BATCH = 8
SEQ = 2048
DEPTH = 4
N_TOK = BATCH * SEQ
HEAD_DIM = 64
N_Q_HEADS = 16
N_KV_HEADS = 4
GQA_GROUP = 4
WINDOW = 128
Q_DIM = 1024
KV_DIM = 256
QKV_DIM = 1536
NUM_BUCKETS = 32
MAX_EXACT = 16
MAX_DISTANCE = 128
CONV_WIDTH = 31
N_EXPERTS = 32
TOP_K = 4
D_EXPERT = 1024
SWIGLU_LIMIT = 7.0
SWIGLU_ALPHA = 1.702
NORM_EPS = 1e-5
NEG_INF = -1e30

LANES = 128
SUBLANES = 8
ROW_TILES = D_MODEL // LANES
N_SLOT = N_TOK * TOP_K

TM = 512
TM_ATTN = WINDOW
N_TILES = N_TOK // TM
SEG = 16
TILE_ROWS = -(-(TM * TOP_K + N_EXPERTS * (SEG - 1)) // 256) * 256
TILE_UNITS = TILE_ROWS // SEG
BM = 512
BLK_UNITS = BM // SEG
N_BUF = 4
AHEAD = 3
SLACK_UNITS = TILE_UNITS - TM * TOP_K // SEG


def _bits_down(n):
    return tuple(1 << i for i in reversed(range(n.bit_length())))
N_BLKP = N_TILES * TILE_ROWS // BM + N_EXPERTS
LOCAL_ROWS = N_TILES * TILE_ROWS
CONV_HALO = 32
CONV_ROWS = 256

VMEM_LIMIT = 56 * 1024 * 1024


def _cparams(*sem):
    return pltpu.CompilerParams(dimension_semantics=sem, vmem_limit_bytes=VMEM_LIMIT)


def _mod_kernel(c_ref, w_ref, b_ref, o_ref):
    c = c_ref[...]
    c_act = (c * jax.nn.sigmoid(c)).astype(jnp.bfloat16)
    o_ref[...] = jnp.dot(c_act, w_ref[...].astype(jnp.bfloat16),
                         preferred_element_type=jnp.float32) + b_ref[...]


def _modulation(c, w_mod, b_mod):
    out = pl.pallas_call(
        _mod_kernel,
        grid=(DEPTH, 6),
        in_specs=[
            pl.BlockSpec((BATCH, D_MODEL), lambda i, j: (0, 0)),
            pl.BlockSpec((None, D_MODEL, D_MODEL), lambda i, j: (i, 0, j)),
            pl.BlockSpec((None, None, 1, D_MODEL), lambda i, j: (i, j, 0, 0)),
        ],
        out_specs=pl.BlockSpec((None, None, BATCH, D_MODEL), lambda i, j: (i, j, 0, 0)),
        out_shape=jax.ShapeDtypeStruct((DEPTH, 6, BATCH, D_MODEL), jnp.float32),
        compiler_params=_cparams("arbitrary", "arbitrary"),
        name="adaln_modulation",
    )(c, w_mod, b_mod.reshape(DEPTH, 6, 1, D_MODEL))
    return jnp.transpose(out, (0, 2, 1, 3))


def _mod_spec(layer):
    return pl.BlockSpec((None, None, 6, D_MODEL), lambda i: (layer, i // (SEQ // TM), 0, 0))


def _rms_modulate(x, g, scale, shift):
    y = x * lax.rsqrt(jnp.mean(x * x, axis=-1, keepdims=True) + NORM_EPS)
    return (y * g) * (1.0 + scale) + shift


def _in_proj_kernel(x_ref, mod_ref, g_ref, w_ref, b_ref, o_ref, *, glu):
    h = _rms_modulate(x_ref[...], g_ref[...], mod_ref[1:2, :], mod_ref[0:1, :])
    u = jnp.dot(h.astype(jnp.bfloat16), w_ref[...], preferred_element_type=jnp.float32) + b_ref[...]
    if glu:
        half = u.shape[-1] // 2
        u = u[:, :half] * jax.nn.sigmoid(u[:, half:])
    o_ref[...] = u.astype(o_ref.dtype)


def _in_proj(x, mod, g, w, b, layer, *, glu, out_dtype):
    n_in = w.shape[-1]
    n_out = n_in // 2 if glu else n_in
    return pl.pallas_call(
        functools.partial(_in_proj_kernel, glu=glu),
        grid=(N_TOK // TM,),
        in_specs=[
            pl.BlockSpec((TM, D_MODEL), lambda i: (i, 0)),
            _mod_spec(layer),
            pl.BlockSpec((1, D_MODEL), lambda i: (0, 0)),
            pl.BlockSpec((D_MODEL, n_in), lambda i: (0, 0)),
            pl.BlockSpec((1, n_in), lambda i: (0, 0)),
        ],
        out_specs=pl.BlockSpec((TM, n_out), lambda i: (i, 0)),
        out_shape=jax.ShapeDtypeStruct((N_TOK, n_out), out_dtype),
        compiler_params=_cparams("arbitrary"),
        name="mixer_in_proj_glu" if glu else "mixer_in_proj",
    )(x, mod, g.reshape(1, D_MODEL), w, b.reshape(1, n_in))


def _band_geometry():
    dist = jnp.arange(WINDOW)[:, None] + WINDOW - jnp.arange(2 * WINDOW)[None, :]
    n = jnp.maximum(dist, 0)
    nf = jnp.maximum(n, 1).astype(jnp.float32)
    large = MAX_EXACT + (jnp.log(nf / MAX_EXACT) / math.log(MAX_DISTANCE / MAX_EXACT)
                         * (NUM_BUCKETS - MAX_EXACT)).astype(jnp.int32)
    large = jnp.minimum(large, NUM_BUCKETS - 1)
    bucket = jnp.where(n < MAX_EXACT, n, large)
    return dist, bucket


def _bias_kernel(table_ref, bucket_ref, dist_ref, o_ref):
    first = pl.program_id(0)
    h = pl.program_id(1)
    bucket = bucket_ref[...]
    dist = dist_ref[...]
    acc = jnp.zeros(bucket.shape, jnp.float32)
    for b in range(NUM_BUCKETS):
        acc = jnp.where(bucket == b, table_ref[b, h], acc)
    col = lax.broadcasted_iota(jnp.int32, bucket.shape, 1)
    visible = (dist >= 0) & (dist < WINDOW) & jnp.logical_or(col >= WINDOW, first == 0)
    o_ref[...] = jnp.where(visible, acc, NEG_INF)


def _rel_bias(table):
    dist, bucket = _band_geometry()
    return pl.pallas_call(
        _bias_kernel,
        grid=(2, N_Q_HEADS),
        in_specs=[
            pl.BlockSpec(memory_space=pltpu.SMEM),
            pl.BlockSpec((WINDOW, 2 * WINDOW), lambda f, h: (0, 0)),
            pl.BlockSpec((WINDOW, 2 * WINDOW), lambda f, h: (0, 0)),
        ],
        out_specs=pl.BlockSpec((None, None, WINDOW, 2 * WINDOW), lambda f, h: (f, h, 0, 0)),
        out_shape=jax.ShapeDtypeStruct((2, N_Q_HEADS, WINDOW, 2 * WINDOW), jnp.float32),
        compiler_params=_cparams("arbitrary", "arbitrary"),
        name="rel_bias",
    )(table, bucket.astype(jnp.int32), dist.astype(jnp.int32))


def _attn_kernel(sink_ref, q_ref, kv_ref, kvp_ref, bias_ref, o_ref, *, layer):
    i = pl.program_id(0)
    first_block = ((i % (SEQ // WINDOW)) == 0).astype(jnp.int32)
    q = q_ref[...]
    kv = jnp.concatenate([kvp_ref[...], kv_ref[...]], axis=0)
    outs = []
    for hk in range(N_KV_HEADS):
        k = kv[:, hk * HEAD_DIM:(hk + 1) * HEAD_DIM]
        v = kv[:, KV_DIM + hk * HEAD_DIM:KV_DIM + (hk + 1) * HEAD_DIM]
        for g in range(GQA_GROUP):
            hq = hk * GQA_GROUP + g
            qh = q[:, hq * HEAD_DIM:(hq + 1) * HEAD_DIM] * (HEAD_DIM ** -0.5)
            s = lax.dot_general(qh, k, (((1,), (1,)), ((), ())), preferred_element_type=jnp.float32)
            s = s + bias_ref[first_block, hq]
            sink = sink_ref[layer, hq]
            m = jnp.maximum(jnp.max(s, axis=-1, keepdims=True), sink)
            p = jnp.exp(s - m)
            denom = jnp.sum(p, axis=-1, keepdims=True) + jnp.exp(sink - m)
            o = jnp.dot(p.astype(jnp.bfloat16), v, preferred_element_type=jnp.float32)
            outs.append((o / denom).astype(jnp.bfloat16))
    o_ref[...] = jnp.concatenate(outs, axis=-1)


def _attention(qkv, bias, sinks, layer):
    kv_col = Q_DIM // (2 * KV_DIM)
    return pl.pallas_call(
        functools.partial(_attn_kernel, layer=layer),
        grid=(N_TOK // WINDOW,),
        in_specs=[
            pl.BlockSpec(memory_space=pltpu.SMEM),
            pl.BlockSpec((WINDOW, Q_DIM), lambda i: (i, 0)),
            pl.BlockSpec((WINDOW, 2 * KV_DIM), lambda i: (i, kv_col)),
            pl.BlockSpec((WINDOW, 2 * KV_DIM), lambda i: (jnp.maximum(i - 1, 0), kv_col)),
            pl.BlockSpec((2, N_Q_HEADS, WINDOW, 2 * WINDOW), lambda i: (0, 0, 0, 0)),
        ],
        out_specs=pl.BlockSpec((WINDOW, Q_DIM), lambda i: (i, 0)),
        out_shape=jax.ShapeDtypeStruct((N_TOK, Q_DIM), jnp.bfloat16),
        compiler_params=_cparams("arbitrary"),
        name="swa_attention",
    )(sinks, qkv, qkv, qkv, bias)


def _split_bf16(a):
    hi = a.astype(jnp.bfloat16)
    lo = (a - hi.astype(jnp.float32)).astype(jnp.bfloat16)
    return hi, lo


def _residual_ffn_router(m, x, mod, g_ffn, wr_hi, wr_lo, b_r, outs):
    xo_ref, xs_ref, pos_ref, gate_ref, units_ref = outs
    x_new = x + mod[2:3, :] * m
    xo_ref[...] = x_new
    h = _rms_modulate(x_new, g_ffn, mod[4:5, :], mod[3:4, :])
    rows = h.shape[0]
    h_hi, h_lo = _split_bf16(h)
    logits = (jnp.dot(h_hi, wr_hi, preferred_element_type=jnp.float32)
              + jnp.dot(h_hi, wr_lo, preferred_element_type=jnp.float32)
              + jnp.dot(h_lo, wr_hi, preferred_element_type=jnp.float32)) + b_r
    lane = lax.broadcasted_iota(jnp.int32, logits.shape, 1)
    vals, ids = [], []
    for _ in range(TOP_K):
        best = jnp.max(logits, axis=-1, keepdims=True)
        arg = jnp.min(jnp.where(logits == best, lane, N_EXPERTS), axis=-1, keepdims=True)
        vals.append(best)
        ids.append(arg)
        logits = jnp.where(lane == arg, -jnp.inf, logits)
    exps = [jnp.exp(v - vals[0]) for v in vals]
    total = exps[0] + exps[1] + exps[2] + exps[3]
    gate_ref[...] = jnp.concatenate([e / total for e in exps], axis=-1)

    member = jnp.zeros(logits.shape, jnp.float32)
    for arg in ids:
        member = member + (lane == arg).astype(jnp.float32)
    units = jnp.floor((jnp.sum(member, axis=0, keepdims=True) + (SEG - 1)) * (1.0 / SEG))
    units_ref[...] = units
    lower_experts = (lax.broadcasted_iota(jnp.int32, (N_EXPERTS, N_EXPERTS), 0)
                     < lax.broadcasted_iota(jnp.int32, (N_EXPERTS, N_EXPERTS), 1)).astype(jnp.bfloat16)
    seg_start = SEG * jnp.dot(jnp.broadcast_to(units, (SUBLANES, N_EXPERTS)).astype(jnp.bfloat16), lower_experts,
                              preferred_element_type=jnp.float32)[0:1, :]
    earlier = jnp.where(lax.broadcasted_iota(jnp.int32, (rows, rows), 0).astype(jnp.int16)
                        > lax.broadcasted_iota(jnp.int32, (rows, rows), 1).astype(jnp.int16),
                        jnp.ones((), jnp.bfloat16), jnp.zeros((), jnp.bfloat16))
    place = seg_start + jnp.dot(earlier, member.astype(jnp.bfloat16), preferred_element_type=jnp.float32)
    pos = [jnp.sum(jnp.where(lane == arg, place, 0.0), axis=-1, keepdims=True).astype(jnp.int32) for arg in ids]
    pos_ref[...] = jnp.concatenate(pos, axis=-1)

    column = lax.broadcasted_iota(jnp.int32, (rows, LANES), 1)
    pos_cols = jnp.zeros((rows, LANES), jnp.float32)
    for k in range(TOP_K):
        pos_cols = jnp.where(column == k, pos[k].astype(jnp.float32), pos_cols)
    pos_lanes = pos_cols.T
    p16 = [pos_lanes[k:k + 1, :].astype(jnp.int32).astype(jnp.int16) for k in range(TOP_K)]
    q = lax.broadcasted_iota(jnp.int32, (TILE_ROWS, rows), 0).astype(jnp.int16)
    hit = (q == p16[0]) | (q == p16[1]) | (q == p16[2]) | (q == p16[3])
    onehot = jnp.where(hit, jnp.ones((), jnp.bfloat16), jnp.zeros((), jnp.bfloat16))
    h_bf = h.astype(jnp.bfloat16)
    half = D_MODEL // 2
    for c in range(2):
        xs = jnp.dot(onehot, h_bf[:, c * half:(c + 1) * half], preferred_element_type=jnp.float32)
        xs_ref[:, c * half:(c + 1) * half] = pltpu.bitcast(xs.astype(jnp.bfloat16), jnp.uint32)


N_EPILOGUE_OUTS = 5


def _attn_out_kernel(a_ref, wo_ref, bo_ref, x_ref, mod_ref, g_ref, wrh_ref, wrl_ref, br_ref, *outs):
    m = jnp.dot(a_ref[...], wo_ref[...], preferred_element_type=jnp.float32) + bo_ref[...]
    _residual_ffn_router(m, x_ref[...], mod_ref[...], g_ref[...], wrh_ref[...], wrl_ref[...], br_ref[...], outs)


def _epilogue_out_specs():
    return [
        pl.BlockSpec((TM, D_MODEL), lambda i: (i, 0)),
        pl.BlockSpec((TILE_ROWS // 2, D_MODEL), lambda i: (i, 0)),
        pl.BlockSpec((TM, TOP_K), lambda i: (i, 0)),
        pl.BlockSpec((TM, TOP_K), lambda i: (i, 0)),
        pl.BlockSpec((None, 1, N_EXPERTS), lambda i: (i, 0, 0)),
    ]


def _epilogue_out_shapes():
    return [
        jax.ShapeDtypeStruct((N_TOK, D_MODEL), jnp.float32),
        jax.ShapeDtypeStruct((LOCAL_ROWS // 2, D_MODEL), jnp.uint32),
        jax.ShapeDtypeStruct((N_TOK, TOP_K), jnp.int32),
        jax.ShapeDtypeStruct((N_TOK, TOP_K), jnp.float32),
        jax.ShapeDtypeStruct((N_TILES, 1, N_EXPERTS), jnp.float32),
    ]


def _router_operands(w_r, b_r):
    hi, lo = _split_bf16(w_r)
    return hi, lo, b_r.reshape(1, N_EXPERTS)


def _router_specs():
    return [
        pl.BlockSpec((D_MODEL, N_EXPERTS), lambda i: (0, 0)),
        pl.BlockSpec((D_MODEL, N_EXPERTS), lambda i: (0, 0)),
        pl.BlockSpec((1, N_EXPERTS), lambda i: (0, 0)),
    ]


def _attn_out(attn, w_o, b_o, x, mod, g_ffn, w_r, b_r, layer):
    return pl.pallas_call(
        _attn_out_kernel,
        grid=(N_TOK // TM,),
        in_specs=[
            pl.BlockSpec((TM, Q_DIM), lambda i: (i, 0)),
            pl.BlockSpec((Q_DIM, D_MODEL), lambda i: (0, 0)),
            pl.BlockSpec((1, D_MODEL), lambda i: (0, 0)),
            pl.BlockSpec((TM, D_MODEL), lambda i: (i, 0)),
            _mod_spec(layer),
            pl.BlockSpec((1, D_MODEL), lambda i: (0, 0)),
        ] + _router_specs(),
        out_specs=_epilogue_out_specs(),
        out_shape=_epilogue_out_shapes(),
        compiler_params=_cparams("arbitrary"),
        name="attn_out_router",
    )(attn, w_o, b_o.reshape(1, D_MODEL), x, mod, g_ffn.reshape(1, D_MODEL), *_router_operands(w_r, b_r))


def _conv_out_kernel(u_ref, up_ref, wdw_ref, bdw_ref, lng_ref, lnb_ref, wo_ref, bo_ref,
                     x_ref, mod_ref, g_ref, wrh_ref, wrl_ref, br_ref, *rest):
    outs = rest[:N_EPILOGUE_OUTS]
    ext_ref, y_ref = rest[N_EPILOGUE_OUTS:]
    i = pl.program_id(0)
    first_tile = (i % (SEQ // TM)) == 0
    ext_ref[0:CONV_HALO, :] = jnp.where(first_tile, 0.0, up_ref[...])
    ext_ref[CONV_HALO:, :] = u_ref[...]
    lead = CONV_HALO - (CONV_WIDTH - 1)
    span = CONV_ROWS + CONV_HALO

    def chunk(r, carry):
        r0 = pl.multiple_of(r * CONV_ROWS, CONV_ROWS)
        for cb in range(ROW_TILES):
            cols = slice(cb * LANES, (cb + 1) * LANES)
            e = ext_ref[pl.ds(r0, span), cols]
            acc = jnp.zeros((CONV_ROWS, LANES), jnp.float32)
            for sub in range(SUBLANES):
                shifted = e if sub == 0 else pltpu.roll(e, span - sub, 0)
                for a in range(span // SUBLANES):
                    tap = a * SUBLANES + sub - lead
                    if 0 <= tap < CONV_WIDTH:
                        acc = acc + wdw_ref[tap:tap + 1, cols] * shifted[a * SUBLANES:a * SUBLANES + CONV_ROWS]
            y_ref[pl.ds(r0, CONV_ROWS), cols] = acc
        return carry

    lax.fori_loop(0, TM // CONV_ROWS, chunk, 0)
    y = y_ref[...] + bdw_ref[...]
    mu = jnp.mean(y, axis=-1, keepdims=True)
    var = jnp.mean(jnp.square(y - mu), axis=-1, keepdims=True)
    z = (y - mu) * lax.rsqrt(var + NORM_EPS) * lng_ref[...] + lnb_ref[...]
    z = z * jax.nn.sigmoid(z)
    m = jnp.dot(z.astype(jnp.bfloat16), wo_ref[...], preferred_element_type=jnp.float32) + bo_ref[...]
    _residual_ffn_router(m, x_ref[...], mod_ref[...], g_ref[...], wrh_ref[...], wrl_ref[...], br_ref[...],
                         outs)


def _conv_out(u, w_dw, b_dw, ln_g, ln_b, w_o, b_o, x, mod, g_ffn, w_r, b_r, layer):
    halo_blocks = TM // CONV_HALO
    row = lambda a: a.reshape(1, D_MODEL)
    return pl.pallas_call(
        _conv_out_kernel,
        grid=(N_TOK // TM,),
        in_specs=[
            pl.BlockSpec((TM, D_MODEL), lambda i: (i, 0)),
            pl.BlockSpec((CONV_HALO, D_MODEL), lambda i: (jnp.maximum(i * halo_blocks - 1, 0), 0)),
            pl.BlockSpec((CONV_WIDTH, D_MODEL), lambda i: (0, 0)),
            pl.BlockSpec((1, D_MODEL), lambda i: (0, 0)),
            pl.BlockSpec((1, D_MODEL), lambda i: (0, 0)),
            pl.BlockSpec((1, D_MODEL), lambda i: (0, 0)),
            pl.BlockSpec((D_MODEL, D_MODEL), lambda i: (0, 0)),
            pl.BlockSpec((1, D_MODEL), lambda i: (0, 0)),
            pl.BlockSpec((TM, D_MODEL), lambda i: (i, 0)),
            _mod_spec(layer),
            pl.BlockSpec((1, D_MODEL), lambda i: (0, 0)),
        ] + _router_specs(),
        out_specs=_epilogue_out_specs(),
        out_shape=_epilogue_out_shapes(),
        scratch_shapes=[
            pltpu.VMEM((TM + CONV_HALO, D_MODEL), jnp.float32),
            pltpu.VMEM((TM, D_MODEL), jnp.float32),
        ],
        compiler_params=_cparams("arbitrary"),
        name="conv_out_router",
    )(u, u, w_dw, row(b_dw), row(ln_g), row(ln_b), w_o, row(b_o), x, mod, row(g_ffn),
      *_router_operands(w_r, b_r))


def _routing(units):
    per_tile = units.reshape(N_TILES, N_EXPERTS).astype(jnp.int32)
    seg_off = jnp.cumsum(per_tile, axis=1) - per_tile
    used = jnp.sum(per_tile, axis=1)
    zero = jnp.zeros((N_EXPERTS, 1), jnp.int32)
    cum = jnp.concatenate([zero, jnp.cumsum(per_tile.T, axis=1)], axis=1)
    total = cum[:, -1]
    n_blocks = (total + BLK_UNITS - 1) // BLK_UNITS
    blk_end = jnp.cumsum(n_blocks)
    blk_start = blk_end - n_blocks
    n_live = blk_end[-1]
    blk = jnp.arange(N_BLKP, dtype=jnp.int32)
    blk_exp = jnp.minimum(jnp.sum(blk[:, None] >= blk_end[None, :], axis=1), N_EXPERTS - 1)
    blk_exp = jnp.where(blk < n_live, blk_exp, blk_exp[n_live - 1]).astype(jnp.int32)
    per_expert = jnp.concatenate([cum, seg_off.T - cum[:, :N_TILES], blk_start[:, None], total[:, None]], axis=1)
    picked = jnp.dot(jax.nn.one_hot(blk_exp, N_EXPERTS, dtype=jnp.float32), per_expert.astype(jnp.float32),
                     precision=lax.Precision.HIGHEST).astype(jnp.int32)
    starts = picked[:, :N_TILES + 1]
    shift = picked[:, N_TILES + 1:2 * N_TILES + 1]
    blk_at = (blk - picked[:, -2]) * BLK_UNITS
    at = blk_at[:, None] + jnp.arange(BLK_UNITS, dtype=jnp.int32)[None, :]
    tile = jnp.minimum(jnp.sum(starts[:, None, 1:] <= at[:, :, None], axis=-1), N_TILES - 1)
    in_tile = tile[:, :, None] == jnp.arange(N_TILES, dtype=jnp.int32)[None, None, :]
    local = tile * TILE_UNITS + at + jnp.sum(jnp.where(in_tile, shift[:, None, :], 0), axis=-1)
    unit_map = jnp.where(at < picked[:, -1:], local, -1)
    experts = jnp.arange(N_EXPERTS, dtype=jnp.int32)
    owners = lax.cummin(jnp.where(n_blocks > 0, experts, N_EXPERTS), reverse=True)
    next_exp = jnp.concatenate([owners[1:], jnp.full((1,), N_EXPERTS, jnp.int32)])
    next_exp = jnp.where(next_exp < N_EXPERTS, next_exp, -1)
    i32 = lambda a: a.astype(jnp.int32)
    held = jnp.clip(picked[:, -1] - blk_at, 0, BLK_UNITS)
    return dict(blk_exp=blk_exp, unit_map=i32(unit_map.reshape(-1)), held=i32(held), used=i32(used),
                next_exp=i32(next_exp))


UNIT_WORDS = SEG // 2


def _as_units(words_2d):
    return words_2d.reshape(words_2d.shape[0] // UNIT_WORDS, UNIT_WORDS, words_2d.shape[1])


def _moe_kernel(exp_ref, map_ref, held_ref, used_ref, next_ref,
                xs_hbm, win_hbm, bin_ref, wout_hbm, bout_ref, ys_hbm,
                xbuf, ybuf, zbuf, win_f32, wout_f32, win_bf, wout_bf, gsem, ssem, zsem, wsem,
                *, layer):
    b = pl.program_id(0)
    last = pl.num_programs(0) - 1

    def zero_tail(tile, act):
        used = used_ref[tile]
        n = TILE_UNITS - used
        for size in _bits_down(SLACK_UNITS):
            @pl.when((n & size) != 0)
            def _():
                start = n & ~(2 * size - 1)
                act(pltpu.make_async_copy(zbuf.at[pl.ds(0, size)],
                                          ys_hbm.at[pl.ds(tile * TILE_UNITS + used + start, size)], zsem))

    def gather(block, buf):
        for u in range(BLK_UNITS):
            local = jnp.maximum(map_ref[block * BLK_UNITS + u], 0)
            pltpu.make_async_copy(xs_hbm.at[local], xbuf.at[buf, u], gsem.at[buf]).start()

    def scatter(block, buf):
        trash = N_TILES * TILE_UNITS + buf * BLK_UNITS
        for u in range(BLK_UNITS):
            local = map_ref[block * BLK_UNITS + u]
            local = jnp.where(local >= 0, local, trash + u)
            pltpu.make_async_copy(ybuf.at[buf, u], ys_hbm.at[local], ssem.at[buf]).start()

    def wait_block(buffers, sem, buf):
        pltpu.make_async_copy(buffers.at[buf], buffers.at[buf], sem.at[buf]).wait()

    def weight_copies(e):
        return (pltpu.make_async_copy(win_hbm.at[layer, e], win_f32, wsem.at[0]),
                pltpu.make_async_copy(wout_hbm.at[layer, e], wout_f32, wsem.at[1]))

    @pl.when(b == 0)
    def _():
        for copy in weight_copies(exp_ref[0]):
            copy.start(priority=1)

    new_expert = jnp.logical_or(b == 0, exp_ref[b] != exp_ref[jnp.maximum(b - 1, 0)])

    @pl.when(new_expert)
    def _():
        e = exp_ref[b]
        for copy in weight_copies(e):
            copy.wait()
        win_bf[...] = win_f32[...].astype(jnp.bfloat16)
        wout_bf[...] = wout_f32[...].astype(jnp.bfloat16)
        upcoming = next_ref[e]

        @pl.when(upcoming >= 0)
        def _():
            for copy in weight_copies(upcoming):
                copy.start(priority=1)

    @pl.when(b == 0)
    def _():
        for ahead in range(AHEAD):
            gather(ahead, ahead)
        ybuf[...] = jnp.zeros_like(ybuf)
        zbuf[...] = jnp.zeros_like(zbuf)
        for piece in range(N_BUF * BLK_UNITS // SLACK_UNITS):
            trash = pltpu.make_async_copy(zbuf.at[pl.ds(0, SLACK_UNITS)],
                                          ys_hbm.at[pl.ds(N_TILES * TILE_UNITS + piece * SLACK_UNITS, SLACK_UNITS)], zsem)
            trash.start()
            trash.wait()

    @pl.when(jnp.logical_and(b >= 1, b <= N_TILES))
    def _():
        zero_tail(b - 1, lambda copy: copy.wait())

    @pl.when(b < N_TILES)
    def _():
        zero_tail(b, lambda copy: copy.start())

    slot = b % N_BUF
    wait_block(xbuf, gsem, slot)

    @pl.when(b >= N_BUF)
    def _():
        wait_block(ybuf, ssem, slot)

    gather(jnp.minimum(b + AHEAD, last), (b + AHEAD) % N_BUF)

    @pl.when(b >= 1)
    def _():
        scatter(b - 1, (b - 1) % N_BUF)

    def mlp(rows):
        x = pltpu.bitcast(xbuf[slot, 0:rows // SEG].reshape(rows // 2, D_MODEL), jnp.bfloat16)
        u = jnp.dot(x, win_bf[...], preferred_element_type=jnp.float32) + bin_ref[...]
        glu = jnp.minimum(u[:, :D_EXPERT], SWIGLU_LIMIT)
        lin = jnp.clip(u[:, D_EXPERT:], -SWIGLU_LIMIT, SWIGLU_LIMIT)
        act = glu * jax.nn.sigmoid(SWIGLU_ALPHA * glu) * (lin + 1.0)
        y = jnp.dot(act.astype(jnp.bfloat16), wout_bf[...], preferred_element_type=jnp.float32) + bout_ref[...]
        ybuf[slot, 0:rows // SEG] = _as_units(pltpu.bitcast(y.astype(jnp.bfloat16), jnp.uint32))

    held = held_ref[b]
    pl.when(held > BLK_UNITS // 2)(functools.partial(mlp, BM))
    pl.when(jnp.logical_and(held > 0, held <= BLK_UNITS // 2))(functools.partial(mlp, BM // 2))

    @pl.when(b == last)
    def _():
        scatter(b, slot)
        for k in range(N_BUF):
            wait_block(ybuf, ssem, (b + k) % N_BUF)
        for k in range(1, AHEAD + 1):
            wait_block(xbuf, gsem, (b + k) % N_BUF)


def _moe_experts(xs_local, route, w_in, b_in, w_out, b_out, layer):
    by_expert = lambda b, ex, *_: (layer, ex[b], 0, 0)
    tables = [route[k] for k in ("blk_exp", "unit_map", "held", "used", "next_exp")]
    grid_spec = pltpu.PrefetchScalarGridSpec(
        num_scalar_prefetch=len(tables),
        grid=(N_BLKP,),
        in_specs=[
            pl.BlockSpec(memory_space=pl.ANY),
            pl.BlockSpec(memory_space=pl.ANY),
            pl.BlockSpec((None, None, 1, 2 * D_EXPERT), by_expert),
            pl.BlockSpec(memory_space=pl.ANY),
            pl.BlockSpec((None, None, 1, D_MODEL), by_expert),
        ],
        out_specs=pl.BlockSpec(memory_space=pl.ANY),
        scratch_shapes=[
            pltpu.VMEM((N_BUF, BLK_UNITS, UNIT_WORDS, D_MODEL), jnp.uint32),
            pltpu.VMEM((N_BUF, BLK_UNITS, UNIT_WORDS, D_MODEL), jnp.uint32),
            pltpu.VMEM((SLACK_UNITS, UNIT_WORDS, D_MODEL), jnp.uint32),
            pltpu.VMEM((D_MODEL, 2 * D_EXPERT), jnp.float32),
            pltpu.VMEM((D_EXPERT, D_MODEL), jnp.float32),
            pltpu.VMEM((D_MODEL, 2 * D_EXPERT), jnp.bfloat16),
            pltpu.VMEM((D_EXPERT, D_MODEL), jnp.bfloat16),
            pltpu.SemaphoreType.DMA((N_BUF,)),
            pltpu.SemaphoreType.DMA((N_BUF,)),
            pltpu.SemaphoreType.DMA(()),
            pltpu.SemaphoreType.DMA((2,)),
        ],
    )
    ys_units = pl.pallas_call(
        functools.partial(_moe_kernel, layer=layer),
        grid_spec=grid_spec,
        out_shape=jax.ShapeDtypeStruct((N_TILES * TILE_UNITS + N_BUF * BLK_UNITS, UNIT_WORDS, D_MODEL), jnp.uint32),
        compiler_params=_cparams("arbitrary"),
        name="moe_experts",
    )(*tables, _as_units(xs_local), w_in, b_in.reshape(DEPTH, N_EXPERTS, 1, 2 * D_EXPERT), w_out,
      b_out.reshape(DEPTH, N_EXPERTS, 1, D_MODEL))
    return ys_units.reshape((LOCAL_ROWS + N_BUF * BM) // 2, D_MODEL)


def _combine_kernel(ys_ref, pos_ref, gate_ref, x_ref, mod_ref, gfin_ref, o_ref, *, final_norm):
    pos = pos_ref[...].astype(jnp.int16)
    gates = gate_ref[...].astype(jnp.bfloat16)
    q = lax.broadcasted_iota(jnp.int32, (TM, TILE_ROWS), 1).astype(jnp.int16)
    weights = jnp.zeros((TM, TILE_ROWS), jnp.bfloat16)
    for k in range(TOP_K):
        weights = jnp.where(q == pos[:, k:k + 1], gates[:, k:k + 1], weights)
    ys = pltpu.bitcast(ys_ref[...], jnp.bfloat16)
    mixed = jnp.dot(weights, ys, preferred_element_type=jnp.float32)
    x_new = x_ref[...] + mod_ref[5:6, :] * mixed
    if final_norm:
        x_new = x_new * lax.rsqrt(jnp.mean(x_new * x_new, axis=-1, keepdims=True) + NORM_EPS) * gfin_ref[...]
    o_ref[...] = x_new


def _combine(ys_local, pos, gates, x, mod, g_final, layer, *, final_norm):
    return pl.pallas_call(
        functools.partial(_combine_kernel, final_norm=final_norm),
        grid=(N_TILES,),
        in_specs=[
            pl.BlockSpec((TILE_ROWS // 2, D_MODEL), lambda i: (i, 0)),
            pl.BlockSpec((TM, TOP_K), lambda i: (i, 0)),
            pl.BlockSpec((TM, TOP_K), lambda i: (i, 0)),
            pl.BlockSpec((TM, D_MODEL), lambda i: (i, 0)),
            _mod_spec(layer),
            pl.BlockSpec((1, D_MODEL), lambda i: (0, 0)),
        ],
        out_specs=pl.BlockSpec((TM, D_MODEL), lambda i: (i, 0)),
        out_shape=jax.ShapeDtypeStruct((N_TOK, D_MODEL), jnp.float32),
        compiler_params=_cparams("arbitrary"),
        name="moe_combine",
    )(ys_local, pos, gates, x, mod, g_final.reshape(1, D_MODEL))


def kernel(x, c, w_mod, b_mod, norm_mix_g, norm_ffn_g, attn_w_qkv, attn_b_qkv, attn_w_o, attn_b_o,
           attn_sinks, rel_bias_table, conv_w_in, conv_b_in, conv_w_dw, conv_b_dw, conv_ln_g, conv_ln_b,
           conv_w_out, conv_b_out, router_w, router_b, moe_w_in, moe_b_in, moe_w_out, moe_b_out,
           final_norm_g):
    bf16 = jnp.bfloat16
    mod = _modulation(c, w_mod, b_mod)
    bias = _rel_bias(rel_bias_table)
    xf = x.reshape(N_TOK, D_MODEL)
    for i in range(DEPTH):
        j = i // 2
        if i % 2 == 0:
            qkv = _in_proj(xf, mod, norm_mix_g[i], attn_w_qkv[j].astype(bf16), attn_b_qkv[j], i,
                           glu=False, out_dtype=bf16)
            attn = _attention(qkv, bias, attn_sinks, j)
            xf, xs_local, pos, gates, units = _attn_out(attn, attn_w_o[j].astype(bf16), attn_b_o[j], xf, mod,
                                                        norm_ffn_g[i], router_w[i], router_b[i], i)
        else:
            u = _in_proj(xf, mod, norm_mix_g[i], conv_w_in[j].astype(bf16), conv_b_in[j], i,
                         glu=True, out_dtype=jnp.float32)
            xf, xs_local, pos, gates, units = _conv_out(u, conv_w_dw[j], conv_b_dw[j], conv_ln_g[j], conv_ln_b[j],
                                                        conv_w_out[j].astype(bf16), conv_b_out[j], xf, mod,
                                                        norm_ffn_g[i], router_w[i], router_b[i], i)
        route = _routing(units)
        ys_local = _moe_experts(xs_local, route, moe_w_in, moe_b_in, moe_w_out, moe_b_out, i)
        xf = _combine(ys_local, pos, gates, xf, mod, final_norm_g, i, final_norm=(i == DEPTH - 1))
    return xf.reshape(BATCH, SEQ, D_MODEL)
```
